```python
import math
import jax, jax.numpy as jnp
from jax import lax
import numpy as np

D_MODEL = 2048
BATCH = 4
SEQ = 4096
DEPTH = 1

D_SSM = D_MODEL
SSM_HEAD_DIM = 64
SSM_HEADS = D_SSM // SSM_HEAD_DIM
SSM_GROUPS = 4
D_STATE = 128
SSM_CONV = 7
CHUNK = 128
D_XBC = D_SSM + 2 * SSM_GROUPS * D_STATE
D_CONF = D_MODEL
CONF_KERNEL = 31
D_MIX = D_SSM + D_CONF
D_IN_PROJ = D_SSM + D_XBC + 2 * SSM_HEADS + 2 * D_CONF
DT_MIN = 1e-3
DT_MAX = 1e-1
N_EXPERTS = 32
TOP_K = 4
D_FF = D_MODEL
SWIGLU_LIMIT = 7.0
SWIGLU_ALPHA = 1.702
MOE_BLOCK = 256
EPS = 1e-5

kernel_name = "hymba_style_ssd_conformer_moe_encoder_block"


def rms_norm(x, w):
    xf = x.astype(jnp.float32)
    y = xf * lax.rsqrt(jnp.mean(xf * xf, axis=-1, keepdims=True) + EPS)
    return (y * w.astype(jnp.float32)).astype(x.dtype)


def depthwise_conv(u, w, b):
    width, ch = w.shape
    pad = width // 2
    out = lax.conv_general_dilated(
        u, w[:, None, :].astype(u.dtype), window_strides=(1,), padding=[(pad, pad)],
        dimension_numbers=("NWC", "WIO", "NWC"), feature_group_count=ch)
    return out + b.astype(u.dtype)


def ssd_chunked(xh, dt, a, bm, cm):
    bsz, seq, n_heads, hd = xh.shape
    g, n = bm.shape[2], bm.shape[3]
    k = n_heads // g
    nc = seq // CHUNK
    xdt = (xh * dt[..., None]).reshape(bsz, nc, CHUNK, g, k, hd)
    a_cum = jnp.cumsum((dt * a).reshape(bsz, nc, CHUNK, g, k), axis=2)
    bc = bm.reshape(bsz, nc, CHUNK, g, n)
    cc = cm.reshape(bsz, nc, CHUNK, g, n)
    seg = a_cum[:, :, :, None] - a_cum[:, :, None]
    tril = jnp.tril(jnp.ones((CHUNK, CHUNK), dtype=bool))[:, :, None, None]
    decay = jnp.exp(jnp.where(tril, seg, -jnp.inf))
    cb = jnp.einsum("bclgn,bcsgn->bclsg", cc, bc)
    y_diag = jnp.einsum("bclsgk,bcsgkp->bclgkp", cb[..., None] * decay, xdt)
    decay_to_end = jnp.exp(a_cum[:, :, -1:] - a_cum)
    states = jnp.einsum("bcsgn,bcsgkp->bcgkpn", bc, xdt * decay_to_end[..., None])
    chunk_decay = jnp.exp(a_cum[:, :, -1])

    def step(h, inp):
        s_c, d_c = inp
        return d_c[..., None, None] * h + s_c, h

    init = jnp.zeros((bsz, g, k, hd, n), dtype=xh.dtype)
    _, prev = lax.scan(step, init, (jnp.moveaxis(states, 1, 0), jnp.moveaxis(chunk_decay, 1, 0)))
    prev = jnp.moveaxis(prev, 0, 1)
    y_off = jnp.einsum("bclgn,bcgkpn->bclgkp", cc, prev) * jnp.exp(a_cum)[..., None]
    return (y_diag + y_off).reshape(bsz, seq, n_heads, hd)


def ssd_branch(z, xbc, dt_raw, conv_w, conv_b, dt_bias_f, dt_bias_b, a_log_f, a_log_b, d_skip, norm_w):
    f32 = jnp.float32
    bsz, seq, _ = z.shape
    xbc = jax.nn.silu(depthwise_conv(xbc, conv_w, conv_b))
    xs, bm, cm = jnp.split(xbc, [D_SSM, D_SSM + SSM_GROUPS * D_STATE], axis=-1)
    xh = xs.reshape(bsz, seq, SSM_HEADS, SSM_HEAD_DIM).astype(f32)
    bm = bm.reshape(bsz, seq, SSM_GROUPS, D_STATE).astype(f32)
    cm = cm.reshape(bsz, seq, SSM_GROUPS, D_STATE).astype(f32)
    dt_f_raw, dt_b_raw = jnp.split(dt_raw.astype(f32), 2, axis=-1)
    dt_f = jax.nn.softplus(dt_f_raw + dt_bias_f.astype(f32))
    dt_b = jax.nn.softplus(dt_b_raw + dt_bias_b.astype(f32))
    a_f = -jnp.exp(a_log_f.astype(f32))
    a_b = -jnp.exp(a_log_b.astype(f32))
    flip = lambda t: jnp.flip(t, axis=1)
    y_f = ssd_chunked(xh, dt_f, a_f, bm, cm)
    y_b = flip(ssd_chunked(flip(xh), flip(dt_b), a_b, flip(bm), flip(cm)))
    y = y_f + y_b + d_skip.astype(f32)[:, None] * xh
    y = y.reshape(bsz, seq, D_SSM) * jax.nn.silu(z.astype(f32))
    yg = y.reshape(bsz, seq, SSM_GROUPS, D_SSM // SSM_GROUPS)
    yg = yg * lax.rsqrt(jnp.mean(yg * yg, axis=-1, keepdims=True) + EPS)
    return (yg.reshape(bsz, seq, D_SSM) * norm_w.astype(f32)).astype(z.dtype)


def conformer_branch(u, dw_w, dw_b, ln_w, ln_b):
    a, g = jnp.split(u, 2, axis=-1)
    v = depthwise_conv(a * jax.nn.sigmoid(g), dw_w, dw_b)
    vf = v.astype(jnp.float32)
    mu = jnp.mean(vf, axis=-1, keepdims=True)
    var = jnp.mean(jnp.square(vf - mu), axis=-1, keepdims=True)
    vf = (vf - mu) * lax.rsqrt(var + EPS) * ln_w.astype(jnp.float32) + ln_b.astype(jnp.float32)
    return jax.nn.silu(vf).astype(u.dtype)


def moe(h, w_router, b_router, w_gate, b_gate, w_up, b_up, w_down, b_down):
    t = h.shape[0]
    tk = t * TOP_K
    n_blocks = -(-(tk + N_EXPERTS * (MOE_BLOCK - 1)) // MOE_BLOCK)
    n_rows = n_blocks * MOE_BLOCK
    logits = (h @ w_router + b_router).astype(jnp.float32)
    top_vals, top_idx = lax.top_k(logits, TOP_K)
    gates = jax.nn.softmax(top_vals, axis=-1).reshape(-1)
    flat_e = top_idx.reshape(-1)
    flat_tok = jnp.arange(tk, dtype=jnp.int32) // TOP_K
    order = jnp.argsort(flat_e)
    sorted_e = flat_e[order]
    counts = jnp.bincount(flat_e, length=N_EXPERTS)
    padded = ((counts + MOE_BLOCK - 1) // MOE_BLOCK) * MOE_BLOCK
    group_start = jnp.cumsum(counts) - counts
    padded_end = jnp.cumsum(padded)
    padded_start = padded_end - padded
    dest = padded_start[sorted_e] + (jnp.arange(tk, dtype=jnp.int32) - group_start[sorted_e])
    row_tok = jnp.zeros((n_rows,), jnp.int32).at[dest].set(flat_tok[order])
    row_gate = jnp.zeros((n_rows,), jnp.float32).at[dest].set(gates[order])
    block_start = jnp.arange(n_blocks, dtype=jnp.int32) * MOE_BLOCK
    block_expert = jnp.minimum(jnp.searchsorted(padded_end, block_start, side="right"), N_EXPERTS - 1)
    xs = h[row_tok].reshape(n_blocks, MOE_BLOCK, h.shape[-1])

    def expert_block(args):
        xb, e = args
        gate = jnp.minimum(xb @ w_gate[e] + b_gate[e], SWIGLU_LIMIT)
        up = jnp.clip(xb @ w_up[e] + b_up[e], -SWIGLU_LIMIT, SWIGLU_LIMIT)
        act = (up + 1.0) * gate * jax.nn.sigmoid(SWIGLU_ALPHA * gate)
        return act @ w_down[e] + b_down[e]

    y_rows = lax.map(expert_block, (xs, block_expert)).reshape(n_rows, -1)
    out = jnp.zeros_like(h).at[row_tok].add(y_rows * row_gate[:, None].astype(h.dtype))
    return out


def setup_inputs(seed: int = 0) -> dict:
    key = jax.random.key(seed)
    ks = jax.random.split(key, 26)
    f32 = jnp.float32
    nrm = lambda k, shape, s: jax.random.normal(k, shape, f32) * s

    def dt_bias(k):
        dt = jnp.exp(jax.random.uniform(k, (DEPTH, SSM_HEADS), f32, math.log(DT_MIN), math.log(DT_MAX)))
        return dt + jnp.log(-jnp.expm1(-dt))

    return {
        "x": nrm(ks[0], (BATCH, SEQ, D_MODEL), 1.0),
        "norm_mix_w": 1.0 + nrm(ks[1], (DEPTH, D_MODEL), 0.02),
        "w_in": nrm(ks[2], (DEPTH, D_MODEL, D_IN_PROJ), D_MODEL ** -0.5),
        "conv_ssm_w": nrm(ks[3], (DEPTH, SSM_CONV, D_XBC), SSM_CONV ** -0.5),
        "conv_ssm_b": nrm(ks[4], (DEPTH, D_XBC), 0.01),
        "dt_bias_fwd": dt_bias(ks[5]),
        "dt_bias_bwd": dt_bias(ks[6]),
        "a_log_fwd": jnp.log(jax.random.uniform(ks[7], (DEPTH, SSM_HEADS), f32, 1.0, 16.0)),
        "a_log_bwd": jnp.log(jax.random.uniform(ks[8], (DEPTH, SSM_HEADS), f32, 1.0, 16.0)),
        "d_skip": 1.0 + nrm(ks[9], (DEPTH, SSM_HEADS), 0.1),
        "ssm_norm_w": 1.0 + nrm(ks[10], (DEPTH, D_SSM), 0.02),
        "conf_dw_w": nrm(ks[11], (DEPTH, CONF_KERNEL, D_CONF), CONF_KERNEL ** -0.5),
        "conf_dw_b": nrm(ks[12], (DEPTH, D_CONF), 0.01),
        "conf_ln_w": 1.0 + nrm(ks[13], (DEPTH, D_CONF), 0.02),
        "conf_ln_b": nrm(ks[14], (DEPTH, D_CONF), 0.01),
        "w_out": nrm(ks[15], (DEPTH, D_MIX, D_MODEL), D_MIX ** -0.5),
        "norm_ffn_w": 1.0 + nrm(ks[16], (DEPTH, D_MODEL), 0.02),
        "w_router": nrm(ks[17], (DEPTH, D_MODEL, N_EXPERTS), D_MODEL ** -0.5),
        "b_router": nrm(ks[18], (DEPTH, N_EXPERTS), 0.01),
        "w_gate": nrm(ks[19], (DEPTH, N_EXPERTS, D_MODEL, D_FF), D_MODEL ** -0.5),
        "b_gate": nrm(ks[20], (DEPTH, N_EXPERTS, D_FF), 0.01),
        "w_up": nrm(ks[21], (DEPTH, N_EXPERTS, D_MODEL, D_FF), D_MODEL ** -0.5),
        "b_up": nrm(ks[22], (DEPTH, N_EXPERTS, D_FF), 0.01),
        "w_down": nrm(ks[23], (DEPTH, N_EXPERTS, D_FF, D_MODEL), D_FF ** -0.5),
        "b_down": nrm(ks[24], (DEPTH, N_EXPERTS, D_MODEL), 0.01),
        "norm_final_w": 1.0 + nrm(ks[25], (D_MODEL,), 0.02),
    }


def reference(x, norm_mix_w, w_in, conv_ssm_w, conv_ssm_b, dt_bias_fwd, dt_bias_bwd, a_log_fwd,
              a_log_bwd, d_skip, ssm_norm_w, conf_dw_w, conf_dw_b, conf_ln_w, conf_ln_b, w_out,
              norm_ffn_w, w_router, b_router, w_gate, b_gate, w_up, b_up, w_down, b_down,
              norm_final_w):
    split_at = [D_SSM, D_SSM + D_XBC, D_SSM + D_XBC + 2 * SSM_HEADS]
    for layer in range(DEPTH):
        h = rms_norm(x, norm_mix_w[layer])
        proj = jnp.einsum("bld,de->ble", h, w_in[layer])
        z, xbc, dt_raw, conf_in = jnp.split(proj, split_at, axis=-1)
        y_ssm = ssd_branch(z, xbc, dt_raw, conv_ssm_w[layer], conv_ssm_b[layer],
                           dt_bias_fwd[layer], dt_bias_bwd[layer], a_log_fwd[layer],
                           a_log_bwd[layer], d_skip[layer], ssm_norm_w[layer])
        y_conf = conformer_branch(conf_in, conf_dw_w[layer], conf_dw_b[layer],
                                  conf_ln_w[layer], conf_ln_b[layer])
        mixed = jnp.concatenate([y_ssm, y_conf], axis=-1)
        x = x + jnp.einsum("ble,ed->bld", mixed, w_out[layer])
        h = rms_norm(x, norm_ffn_w[layer]).reshape(-1, D_MODEL)
        y = moe(h, w_router[layer], b_router[layer], w_gate[layer], b_gate[layer],
                w_up[layer], b_up[layer], w_down[layer], b_down[layer])
        x = x + y.reshape(x.shape)
    return rms_norm(x, norm_final_w)
```

```python
import functools
import math

import jax
import jax.numpy as jnp
from jax import lax
from jax.experimental import pallas as pl
from jax.experimental.pallas import tpu as pltpu

F32 = jnp.float32
BF16 = jnp.bfloat16

EPS = 1e-5
HEAD_DIM = 64
SSM_GROUPS = 4
D_STATE = 128
CHUNK = 128
SSM_CONV = 7
CONF_KERNEL = 31
TOP_K = 4
SWIGLU_LIMIT = 7.0
SWIGLU_ALPHA = 1.702

LANES = 128
SUBLANES = 8
VMEM_LIMIT_BYTES = 56 * 1024 * 1024
HALO = 16
MOE_ROWS = 512

HIGHEST = lax.Precision.HIGHEST


def _cparams(sem):
    return pltpu.CompilerParams(dimension_semantics=sem, vmem_limit_bytes=VMEM_LIMIT_BYTES)


def _sigmoid(v):
    return 1.0 / (1.0 + jnp.exp(-v))


def _silu(v):
    return v * _sigmoid(v)


def _softplus(v):
    return jnp.maximum(v, 0.0) + jnp.log(1.0 + jnp.exp(-jnp.abs(v)))


def _inproj_kernel(x_ref, nw_ref, w_ref, wdt_ref, wdtt_ref, o_ref, dt_ref, dtt_ref, h_scr):
    @pl.when(pl.program_id(1) == 0)
    def _():
        x = x_ref[...]
        ms = jnp.mean(x * x, axis=-1, keepdims=True)
        h = x * lax.rsqrt(ms + EPS) * nw_ref[...]
        h_scr[...] = h.astype(BF16)
        dt_ref[...] = jnp.dot(h, wdt_ref[...], precision=HIGHEST, preferred_element_type=F32)
        dtt_ref[...] = lax.dot_general(
            wdtt_ref[...], h, (((1,), (1,)), ((), ())), precision=HIGHEST, preferred_element_type=F32)

    o_ref[...] = jnp.dot(h_scr[...], w_ref[...], preferred_element_type=F32).astype(o_ref.dtype)


def _in_proj(x2d, norm_w, w_main, w_dt, w_dtt, tm, tn):
    t, d = x2d.shape
    n_main = w_main.shape[1]
    n_dt = w_dt.shape[1]
    return pl.pallas_call(
        _inproj_kernel,
        grid=(t // tm, n_main // tn),
        in_specs=[
            pl.BlockSpec((tm, d), lambda m, n: (m, 0)),
            pl.BlockSpec((1, d), lambda m, n: (0, 0)),
            pl.BlockSpec((d, tn), lambda m, n: (0, n)),
            pl.BlockSpec((d, n_dt), lambda m, n: (0, 0)),
            pl.BlockSpec((n_dt, d), lambda m, n: (0, 0)),
        ],
        out_specs=[
            pl.BlockSpec((tm, tn), lambda m, n: (m, n)),
            pl.BlockSpec((tm, n_dt), lambda m, n: (m, 0)),
            pl.BlockSpec((n_dt, tm), lambda m, n: (0, m)),
        ],
        out_shape=[
            jax.ShapeDtypeStruct((t, n_main), BF16),
            jax.ShapeDtypeStruct((t, n_dt), F32),
            jax.ShapeDtypeStruct((n_dt, t), F32),
        ],
        scratch_shapes=[pltpu.VMEM((tm, d), BF16)],
        compiler_params=_cparams(("parallel", "arbitrary")),
        name="in_proj",
    )(x2d, norm_w, w_main, w_dt, w_dtt)


def _fill_halo_scratch(scr, cur, prev, nxt, lt):
    l = pl.program_id(1)
    last = pl.num_programs(1) - 1
    scr[HALO:HALO + lt, :] = cur
    scr[0:HALO, :] = jnp.where(l > 0, prev, 0.0)
    scr[HALO + lt:HALO + lt + HALO, :] = jnp.where(l < last, nxt, 0.0)


def _conv_ssm_kernel(cur_ref, prev_ref, next_ref, w_ref, b_ref, o_ref, scr, *, lt, rc):
    _fill_halo_scratch(scr, cur_ref[0].astype(F32), prev_ref[0].astype(F32), next_ref[0].astype(F32), lt)
    pad = SSM_CONV // 2
    w = w_ref[...]
    bias = b_ref[...]

    def body(i, carry):
        r0 = pl.multiple_of(i * rc, rc)
        win = scr[pl.ds(r0 + (HALO - SUBLANES), rc + 2 * SUBLANES), :]
        acc = jnp.zeros((rc, scr.shape[1]), F32)
        for k in range(SSM_CONV):
            o = SUBLANES - pad + k
            acc = acc + w[k:k + 1, :] * win[o:o + rc, :]
        o_ref[0, pl.ds(r0, rc), :] = _silu(acc + bias).astype(o_ref.dtype)
        return carry

    lax.fori_loop(0, lt // rc, body, 0)


def _conv_ssm(proj3, conv_w, conv_b, col0, lt, ct, rc=32):
    bsz, seq, _ = proj3.shape
    c = conv_w.shape[1]
    cb0 = col0 // ct
    hb = lt // HALO
    n_hb = seq // HALO
    kern = functools.partial(_conv_ssm_kernel, lt=lt, rc=rc)
    return pl.pallas_call(
        kern,
        grid=(bsz, seq // lt, c // ct),
        in_specs=[
            pl.BlockSpec((1, lt, ct), lambda b, l, j: (b, l, cb0 + j)),
            pl.BlockSpec((1, HALO, ct), lambda b, l, j: (b, jnp.maximum(l * hb - 1, 0), cb0 + j)),
            pl.BlockSpec((1, HALO, ct), lambda b, l, j: (b, jnp.minimum((l + 1) * hb, n_hb - 1), cb0 + j)),
            pl.BlockSpec((SSM_CONV, ct), lambda b, l, j: (0, j)),
            pl.BlockSpec((1, ct), lambda b, l, j: (0, j)),
        ],
        out_specs=pl.BlockSpec((1, lt, ct), lambda b, l, j: (b, l, j)),
        out_shape=jax.ShapeDtypeStruct((bsz, seq, c), BF16),
        scratch_shapes=[pltpu.VMEM((lt + 2 * HALO, ct), F32)],
        compiler_params=_cparams(("parallel", "parallel", "parallel")),
        name="conv_ssm",
    )(proj3, proj3, proj3, conv_w, conv_b)


def _conformer_kernel(a_ref, g_ref, ap_ref, gp_ref, an_ref, gn_ref, w_ref, b_ref, lnw_ref, lnb_ref,
                      o_ref, scr, sh_scr, v_scr, *, lt, rc, lc):
    def glu(a, g):
        return a.astype(F32) * _sigmoid(g.astype(F32))

    _fill_halo_scratch(scr, glu(a_ref[0], g_ref[0]), glu(ap_ref[0], gp_ref[0]), glu(an_ref[0], gn_ref[0]), lt)
    c = scr.shape[1]
    pad = CONF_KERNEL // 2
    n_sh = lt + 2 * HALO - SUBLANES
    for r0 in range(0, n_sh, rc):
        n = min(rc, n_sh - r0)
        for j in range(c // lc):
            win = scr[r0:r0 + n + SUBLANES, j * lc:(j + 1) * lc]
            for m in range(1, SUBLANES):
                sh_scr[m - 1, r0:r0 + n, j * lc:(j + 1) * lc] = win[m:m + n, :]

    def conv_body(i, carry):
        r0 = pl.multiple_of(i * rc, rc)
        for j in range(c // lc):
            lanes = slice(j * lc, (j + 1) * lc)
            acc = jnp.zeros((rc, lc), F32)
            for k in range(CONF_KERNEL):
                o = HALO - pad + k
                m = o % SUBLANES
                base = o - m
                if m == 0:
                    u = scr[pl.ds(r0 + base, rc), lanes]
                else:
                    u = sh_scr[m - 1, pl.ds(r0 + base, rc), lanes]
                acc = acc + w_ref[k:k + 1, lanes] * u
            v_scr[pl.ds(r0, rc), lanes] = acc + b_ref[:, lanes]
        return carry

    lax.fori_loop(0, lt // rc, conv_body, 0)
    lnw = lnw_ref[...]
    lnb = lnb_ref[...]
    rn = 2 * SUBLANES

    def norm_body(i, carry):
        r0 = pl.multiple_of(i * rn, rn)
        v = v_scr[pl.ds(r0, rn), :]
        mu = jnp.mean(v, axis=-1, keepdims=True)
        vc = v - mu
        var = jnp.mean(vc * vc, axis=-1, keepdims=True)
        y = vc * lax.rsqrt(var + EPS) * lnw + lnb
        o_ref[0, pl.ds(r0, rn), :] = _silu(y).astype(o_ref.dtype)
        return carry

    lax.fori_loop(0, lt // rn, norm_body, 0)


def _conformer(proj3, dw_w, dw_b, ln_w, ln_b, col_a, col_g, lt, rc=64, lc=512):
    bsz, seq, _ = proj3.shape
    c = dw_w.shape[1]
    ja = col_a // c
    jg = col_g // c
    hb = lt // HALO
    n_hb = seq // HALO
    kern = functools.partial(_conformer_kernel, lt=lt, rc=rc, lc=lc)

    def cur(j):
        return pl.BlockSpec((1, lt, c), lambda b, l: (b, l, j))

    def prev(j):
        return pl.BlockSpec((1, HALO, c), lambda b, l: (b, jnp.maximum(l * hb - 1, 0), j))

    def nxt(j):
        return pl.BlockSpec((1, HALO, c), lambda b, l: (b, jnp.minimum((l + 1) * hb, n_hb - 1), j))

    vec = pl.BlockSpec((1, c), lambda b, l: (0, 0))
    return pl.pallas_call(
        kern,
        grid=(bsz, seq // lt),
        in_specs=[cur(ja), cur(jg), prev(ja), prev(jg), nxt(ja), nxt(jg),
                  pl.BlockSpec((CONF_KERNEL, c), lambda b, l: (0, 0)), vec, vec, vec],
        out_specs=pl.BlockSpec((1, lt, c), lambda b, l: (b, l, 0)),
        out_shape=jax.ShapeDtypeStruct((bsz, seq, c), BF16),
        scratch_shapes=[pltpu.VMEM((lt + 2 * HALO, c), F32),
                        pltpu.VMEM((SUBLANES - 1, lt + 2 * HALO, c), F32),
                        pltpu.VMEM((lt, c), F32)],
        compiler_params=_cparams(("parallel", "parallel")),
        name="conformer",
    )(proj3, proj3, proj3, proj3, proj3, proj3, dw_w, dw_b, ln_w, ln_b)


def _split3(v):
    hi = v.astype(BF16).astype(F32)
    r = v - hi
    mid = r.astype(BF16).astype(F32)
    lo = (r - mid).astype(BF16).astype(F32)
    return hi, mid, lo


def _ssd_kernel(*refs, reverse, final, n_heads):
    if final:
        (xs_ref, bm_ref, cm_ref, dt_ref, dtt_ref, dtb_ref, dtbt_ref, alog_ref, alogt_ref, tri_ref, trit_ref,
         e3_ref, yf_ref, z_ref, dsk_ref, nw_ref, o_ref, state_scr) = refs
    else:
        (xs_ref, bm_ref, cm_ref, dt_ref, dtt_ref, dtb_ref, dtbt_ref, alog_ref, alogt_ref, tri_ref, trit_ref,
         e3_ref, o_ref, state_scr) = refs
    q = CHUNK
    hg = n_heads
    d_ssm = n_heads * HEAD_DIM
    gw = d_ssm // SSM_GROUPS

    @pl.when(pl.program_id(1) == 0)
    def _():
        state_scr[...] = jnp.zeros_like(state_scr)

    lane = lax.broadcasted_iota(jnp.int32, (q, LANES), 1)
    sub = lax.broadcasted_iota(jnp.int32, (LANES, q), 0)

    def sel3_lanes(v):
        hi, mid, lo = _split3(v)
        return jnp.where(lane < hg, hi, jnp.where(lane < 2 * hg, mid, jnp.where(lane < 3 * hg, lo, 0.0))).astype(BF16)

    def sel3_rows(v):
        hi, mid, lo = _split3(v)
        return jnp.where(sub < hg, hi, jnp.where(sub < 2 * hg, mid, jnp.where(sub < 3 * hg, lo, 0.0))).astype(BF16)

    a_row = -jnp.exp(alog_ref[...])
    dt = _softplus(dt_ref[...] + dtb_ref[...])
    da = dt * a_row
    a_col = -jnp.exp(alogt_ref[...])
    dat = _softplus(dtt_ref[...] + dtbt_ref[...]) * a_col

    tri = tri_ref[...]
    cum3 = jnp.dot(tri, sel3_lanes(da), preferred_element_type=F32)
    cum = cum3
    for r in (1, 2, 3):
        cum = cum + pltpu.roll(cum3, r * hg, axis=1)
    cumt3 = jnp.dot(sel3_rows(dat), trit_ref[...], preferred_element_type=F32)
    cumt = cumt3[0:hg] + cumt3[hg:2 * hg] + cumt3[2 * hg:3 * hg]

    tot_row = 0 if reverse else q - 1
    total = cum[tot_row:tot_row + 1, :]

    e3 = e3_ref[...]
    lhs = jnp.concatenate(
        [sel3_lanes(dt), sel3_lanes(jnp.exp(cum)), sel3_lanes(jnp.exp(total - cum)),
         sel3_lanes(jnp.broadcast_to(jnp.exp(total), (q, LANES)))], axis=0)
    ex = jnp.dot(lhs, e3, preferred_element_type=F32)
    dt_x = ex[0:q]
    ecum_x = ex[q:2 * q]
    edte_x = ex[2 * q:3 * q]
    cdec_x = ex[3 * q:3 * q + 1]

    xs = xs_ref[0].astype(F32)
    xdt = xs * dt_x
    xdt_b = xdt.astype(BF16)
    xdte_b = (xdt * edte_x).astype(BF16)
    bm = bm_ref[0]
    cm = cm_ref[0]

    li = lax.broadcasted_iota(jnp.int32, (q, q), 0)
    si = lax.broadcasted_iota(jnp.int32, (q, q), 1)
    mask = (li <= si) if reverse else (li >= si)
    lane_lo = lax.broadcasted_iota(jnp.int32, (q, LANES), 1) < HEAD_DIM

    hpg = n_heads // SSM_GROUPS
    y_parts = []
    for g in range(SSM_GROUPS):
        bg = bm[:, g * D_STATE:(g + 1) * D_STATE]
        cg = cm[:, g * D_STATE:(g + 1) * D_STATE]
        cb = lax.dot_general(cg, bg, (((1,), (1,)), ((), ())), preferred_element_type=F32)
        st = state_scr[:, g * gw:(g + 1) * gw]
        y_off = jnp.dot(cg, st.astype(BF16), preferred_element_type=F32) * ecum_x[:, g * gw:(g + 1) * gw]
        diag = []
        for pair in range(hpg // 2):
            h0 = g * hpg + 2 * pair
            ls = []
            for h in (h0, h0 + 1):
                seg = cum[:, h:h + 1] - cumt[h:h + 1, :]
                ls.append((cb * jnp.exp(jnp.where(mask, seg, -jnp.inf))).astype(BF16))
            l2 = jnp.concatenate(ls, axis=1)
            xp = xdt_b[:, h0 * HEAD_DIM:(h0 + 2) * HEAD_DIM]
            zero = jnp.zeros_like(xp)
            r2 = jnp.concatenate([jnp.where(lane_lo, xp, zero), jnp.where(lane_lo, zero, xp)], axis=0)
            diag.append(jnp.dot(l2, r2, preferred_element_type=F32))
        y_parts.append(jnp.concatenate(diag, axis=1) + y_off)
        contrib = lax.dot_general(bg, xdte_b[:, g * gw:(g + 1) * gw], (((0,), (0,)), ((), ())),
                                  preferred_element_type=F32)
        state_scr[:, g * gw:(g + 1) * gw] = st * cdec_x[:, g * gw:(g + 1) * gw] + contrib
    y = jnp.concatenate(y_parts, axis=1)

    if not final:
        o_ref[0] = y
    else:
        y = y + yf_ref[0] + dsk_ref[...] * xs
        y = y * _silu(z_ref[0].astype(F32))
        outs = []
        for g in range(SSM_GROUPS):
            yg = y[:, g * gw:(g + 1) * gw]
            ms = jnp.mean(yg * yg, axis=-1, keepdims=True)
            outs.append(yg * lax.rsqrt(ms + EPS))
        o_ref[0] = (jnp.concatenate(outs, axis=1) * nw_ref[...]).astype(o_ref.dtype)


def _ssd(xbc3, dt2, dtt2, dtb, dtbt, alog, alogt, e3, *, reverse, n_heads, final_args=None):
    bsz, seq, _ = xbc3.shape
    d_ssm = n_heads * HEAD_DIM
    gn = SSM_GROUPS * D_STATE
    nc = seq // CHUNK
    q = CHUNK
    d = 1 if reverse else 0
    final = final_args is not None

    def cidx(c):
        return (nc - 1 - c) if reverse else c

    li = jnp.arange(q)[:, None]
    ji = jnp.arange(q)[None, :]
    tri = ((ji >= li) if reverse else (ji <= li)).astype(BF16)
    trit = tri.T

    const = lambda shape: pl.BlockSpec(shape, lambda b, c: (0,) * len(shape))
    in_specs = [
        pl.BlockSpec((1, q, d_ssm), lambda b, c: (b, cidx(c), 0)),
        pl.BlockSpec((1, q, gn), lambda b, c: (b, cidx(c), d_ssm // gn)),
        pl.BlockSpec((1, q, gn), lambda b, c: (b, cidx(c), d_ssm // gn + 1)),
        pl.BlockSpec((q, LANES), lambda b, c: (b * nc + cidx(c), d)),
        pl.BlockSpec((LANES, q), lambda b, c: (d, b * nc + cidx(c))),
        const((1, LANES)), const((LANES, q)), const((1, LANES)), const((LANES, q)),
        const((q, q)), const((q, q)), const((LANES, d_ssm)),
    ]
    args = [xbc3, xbc3, xbc3, dt2, dtt2, dtb, dtbt, alog, alogt, tri, trit, e3]
    if final:
        yf3, proj3, dskip_x, norm_w = final_args
        in_specs += [
            pl.BlockSpec((1, q, d_ssm), lambda b, c: (b, cidx(c), 0)),
            pl.BlockSpec((1, q, d_ssm), lambda b, c: (b, cidx(c), 0)),
            const((1, d_ssm)), const((1, d_ssm)),
        ]
        args += [yf3, proj3, dskip_x, norm_w]
    kern = functools.partial(_ssd_kernel, reverse=reverse, final=final, n_heads=n_heads)
    return pl.pallas_call(
        kern,
        grid=(bsz, nc),
        in_specs=in_specs,
        out_specs=pl.BlockSpec((1, q, d_ssm), lambda b, c: (b, cidx(c), 0)),
        out_shape=jax.ShapeDtypeStruct((bsz, seq, d_ssm), BF16 if final else F32),
        scratch_shapes=[pltpu.VMEM((D_STATE, d_ssm), F32)],
        compiler_params=_cparams(("parallel", "arbitrary")),
        name="ssd_bwd" if reverse else "ssd_fwd",
    )(*args)


def _outproj_kernel(ys_ref, yc_ref, wa_ref, wb_ref, x_ref, nw_ref, wr_ref, br_ref, x1_ref, lg_ref):
    acc = jnp.dot(ys_ref[...], wa_ref[...], preferred_element_type=F32)
    acc = acc + jnp.dot(yc_ref[...], wb_ref[...], preferred_element_type=F32)
    x1 = x_ref[...] + acc
    x1_ref[...] = x1
    ms = jnp.mean(x1 * x1, axis=-1, keepdims=True)
    h = x1 * lax.rsqrt(ms + EPS) * nw_ref[...]
    lg_ref[...] = jnp.dot(h, wr_ref[...], precision=HIGHEST, preferred_element_type=F32) + br_ref[...]


def _out_proj(y_ssm, y_conf, w_out, x2d, norm_w, w_router, b_router, tm):
    t, d = x2d.shape
    ka = y_ssm.shape[1]
    ne = w_router.shape[1]
    return pl.pallas_call(
        _outproj_kernel,
        grid=(t // tm,),
        in_specs=[
            pl.BlockSpec((tm, ka), lambda m: (m, 0)),
            pl.BlockSpec((tm, ka), lambda m: (m, 0)),
            pl.BlockSpec((ka, d), lambda m: (0, 0)),
            pl.BlockSpec((ka, d), lambda m: (1, 0)),
            pl.BlockSpec((tm, d), lambda m: (m, 0)),
            pl.BlockSpec((1, d), lambda m: (0, 0)),
            pl.BlockSpec((d, ne), lambda m: (0, 0)),
            pl.BlockSpec((1, ne), lambda m: (0, 0)),
        ],
        out_specs=[
            pl.BlockSpec((tm, d), lambda m: (m, 0)),
            pl.BlockSpec((tm, ne), lambda m: (m, 0)),
        ],
        out_shape=[
            jax.ShapeDtypeStruct((t, d), F32),
            jax.ShapeDtypeStruct((t, ne), F32),
        ],
        compiler_params=_cparams(("parallel",)),
        name="out_proj",
    )(y_ssm, y_conf, w_out, w_out, x2d, norm_w, w_router, b_router)


def _dispatch_kernel(idx_ref, x1_hbm, nw_ref, o_ref, buf, sem, *, bm):
    def body(j, carry):
        r = idx_ref[0, 0, j]
        pltpu.make_async_copy(x1_hbm.at[pl.ds(r, 1), :], buf.at[pl.ds(j, 1), :], sem).start()
        return carry

    lax.fori_loop(0, bm, body, 0)
    pltpu.make_async_copy(x1_hbm.at[pl.ds(0, bm), :], buf, sem).wait()
    x = buf[...]
    ms = jnp.mean(x * x, axis=-1, keepdims=True)
    o_ref[...] = (x * lax.rsqrt(ms + EPS) * nw_ref[...]).astype(o_ref.dtype)


def _dispatch(row_tok, x1, norm_w, bm):
    n_rows = row_tok.shape[0]
    t, d = x1.shape
    nb = n_rows // bm
    idx3 = row_tok.reshape(nb, 1, bm)
    kern = functools.partial(_dispatch_kernel, bm=bm)
    return pl.pallas_call(
        kern,
        grid=(nb,),
        in_specs=[
            pl.BlockSpec((1, 1, bm), lambda i: (i, 0, 0), memory_space=pltpu.SMEM),
            pl.BlockSpec(memory_space=pl.ANY),
            pl.BlockSpec((1, d), lambda i: (0, 0)),
        ],
        out_specs=pl.BlockSpec((bm, d), lambda i: (i, 0)),
        out_shape=jax.ShapeDtypeStruct((n_rows, d), BF16),
        scratch_shapes=[pltpu.VMEM((bm, d), F32), pltpu.SemaphoreType.DMA(())],
        compiler_params=_cparams(("arbitrary",)),
        name="moe_dispatch",
    )(idx3, x1, norm_w)


def _moe_up_kernel(be_ref, nu_ref, x_ref, wg_ref, bg_ref, wu_ref, bu_ref, o_ref):
    @pl.when(pl.program_id(1) < nu_ref[0])
    def _():
        x = x_ref[...]
        gate = jnp.dot(x, wg_ref[0].astype(BF16), preferred_element_type=F32) + bg_ref[0]
        up = jnp.dot(x, wu_ref[0].astype(BF16), preferred_element_type=F32) + bu_ref[0]
        gate = jnp.minimum(gate, SWIGLU_LIMIT)
        up = jnp.clip(up, -SWIGLU_LIMIT, SWIGLU_LIMIT)
        act = (up + 1.0) * gate * _sigmoid(SWIGLU_ALPHA * gate)
        o_ref[...] = act.astype(o_ref.dtype)


def _moe_down_kernel(be_ref, nu_ref, a_ref, wd_ref, bd_ref, o_ref):
    @pl.when(pl.program_id(1) < nu_ref[0])
    def _():
        y = jnp.dot(a_ref[...], wd_ref[0].astype(BF16), preferred_element_type=F32) + bd_ref[0]
        o_ref[...] = y.astype(o_ref.dtype)


def _blk(b, nu_ref):
    return jnp.minimum(b, nu_ref[0] - 1)


def _moe_up(block_expert, n_used, xs, w_gate, b_gate3, w_up, b_up3, bm, tn):
    n_rows, d = xs.shape
    dff = w_gate.shape[2]
    nb = n_rows // bm
    wspec = pl.BlockSpec((1, d, tn), lambda n, b, be, nu: (be[_blk(b, nu)], 0, n))
    bspec = pl.BlockSpec((1, 1, tn), lambda n, b, be, nu: (be[_blk(b, nu)], 0, n))
    return pl.pallas_call(
        _moe_up_kernel,
        grid_spec=pltpu.PrefetchScalarGridSpec(
            num_scalar_prefetch=2,
            grid=(dff // tn, nb),
            in_specs=[pl.BlockSpec((bm, d), lambda n, b, be, nu: (_blk(b, nu), 0)), wspec, bspec, wspec, bspec],
            out_specs=pl.BlockSpec((bm, tn), lambda n, b, be, nu: (_blk(b, nu), n)),
        ),
        out_shape=jax.ShapeDtypeStruct((n_rows, dff), BF16),
        compiler_params=_cparams(("parallel", "arbitrary")),
        name="moe_up",
    )(block_expert, n_used, xs, w_gate, b_gate3, w_up, b_up3)


def _moe_down(block_expert, n_used, act, w_down, b_down3, bm, tn):
    n_rows, dff = act.shape
    d = w_down.shape[2]
    nb = n_rows // bm
    return pl.pallas_call(
        _moe_down_kernel,
        grid_spec=pltpu.PrefetchScalarGridSpec(
            num_scalar_prefetch=2,
            grid=(d // tn, nb),
            in_specs=[
                pl.BlockSpec((bm, dff), lambda n, b, be, nu: (_blk(b, nu), 0)),
                pl.BlockSpec((1, dff, tn), lambda n, b, be, nu: (be[_blk(b, nu)], 0, n)),
                pl.BlockSpec((1, 1, tn), lambda n, b, be, nu: (be[_blk(b, nu)], 0, n)),
            ],
            out_specs=pl.BlockSpec((bm, tn), lambda n, b, be, nu: (_blk(b, nu), n)),
        ),
        out_shape=jax.ShapeDtypeStruct((n_rows, d), F32),
        compiler_params=_cparams(("parallel", "arbitrary")),
        name="moe_down",
    )(block_expert, n_used, act, w_down, b_down3)


def _combine_kernel(idx_ref, y_hbm, x1_ref, g_ref, nw_ref, o_ref, buf, sem, *, tm):
    def body(t, carry):
        for k in range(TOP_K):
            r = idx_ref[0, k, t]
            pltpu.make_async_copy(y_hbm.at[pl.ds(r, 1), :], buf.at[pl.ds(k * tm + t, 1), :], sem).start()
        return carry

    lax.fori_loop(0, tm, body, 0)
    pltpu.make_async_copy(y_hbm.at[pl.ds(0, TOP_K * tm), :], buf, sem).wait()
    acc = x1_ref[...]
    g = g_ref[...]
    for k in range(TOP_K):
        acc = acc + g[:, k:k + 1] * buf[k * tm:(k + 1) * tm, :]
    ms = jnp.mean(acc * acc, axis=-1, keepdims=True)
    o_ref[...] = acc * lax.rsqrt(ms + EPS) * nw_ref[...]


def _combine(dest_kt, y_rows, x1, gates, norm_w, tm):
    t, d = x1.shape
    nt = t // tm
    idx3 = dest_kt.reshape(TOP_K, nt, tm).transpose(1, 0, 2)
    kern = functools.partial(_combine_kernel, tm=tm)
    return pl.pallas_call(
        kern,
        grid=(nt,),
        in_specs=[
            pl.BlockSpec((1, TOP_K, tm), lambda i: (i, 0, 0), memory_space=pltpu.SMEM),
            pl.BlockSpec(memory_space=pl.ANY),
            pl.BlockSpec((tm, d), lambda i: (i, 0)),
            pl.BlockSpec((tm, TOP_K), lambda i: (i, 0)),
            pl.BlockSpec((1, d), lambda i: (0, 0)),
        ],
        out_specs=pl.BlockSpec((tm, d), lambda i: (i, 0)),
        out_shape=jax.ShapeDtypeStruct((t, d), F32),
        scratch_shapes=[pltpu.VMEM((TOP_K * tm, d), F32), pltpu.SemaphoreType.DMA(())],
        compiler_params=_cparams(("arbitrary",)),
        name="moe_combine",
    )(idx3, y_rows, x1, gates, norm_w)


def _route(logits, n_experts, bm, n_blocks):
    t = logits.shape[0]
    top_vals, top_idx = lax.top_k(logits, TOP_K)
    gates = jax.nn.softmax(top_vals, axis=-1)
    flat_e = top_idx.reshape(-1)
    onehot = (flat_e[:, None] == jnp.arange(n_experts, dtype=flat_e.dtype)[None, :]).astype(jnp.int32)
    csum = jnp.cumsum(onehot, axis=0)
    rank = jnp.sum(csum * onehot, axis=1) - 1
    counts = csum[-1]
    padded = ((counts + bm - 1) // bm) * bm
    padded_end = jnp.cumsum(padded)
    padded_start = padded_end - padded
    dest = (padded_start[flat_e] + rank).astype(jnp.int32)
    row_tok = jnp.zeros((n_blocks * bm,), jnp.int32).at[dest].set(
        jnp.arange(t * TOP_K, dtype=jnp.int32) // TOP_K)
    n_used = (padded_end[-1] // bm).astype(jnp.int32)
    block_start = jnp.arange(n_blocks, dtype=jnp.int32) * bm
    block_expert = jnp.minimum(jnp.searchsorted(padded_end, block_start, side="right"), n_experts - 1)
    block_expert = block_expert.astype(jnp.int32)
    dest_kt = dest.reshape(t, TOP_K).T
    return gates, dest_kt, row_tok, block_expert, n_used.reshape(1)


def _pick(n, candidates):
    for c in candidates:
        if n % c == 0:
            return c
    return n


def _layer(x, norm_mix_w, w_in, conv_ssm_w, conv_ssm_b, dt_bias_fwd, dt_bias_bwd, a_log_fwd, a_log_bwd,
           d_skip, ssm_norm_w, conf_dw_w, conf_dw_b, conf_ln_w, conf_ln_b, w_out, norm_ffn_w, w_router,
           b_router, w_gate, b_gate, w_up, b_up, w_down, b_down):
    bsz, seq, d = x.shape
    t = bsz * seq
    n_heads = dt_bias_fwd.shape[0]
    d_ssm = n_heads * HEAD_DIM
    d_xbc = conv_ssm_w.shape[1]
    d_conf = conf_dw_w.shape[1]
    n_experts = w_gate.shape[0]
    row = lambda v: v.reshape(1, -1).astype(F32)

    c_dt = d_ssm + d_xbc
    col_conf = d_ssm
    col_xbc = d_ssm + 2 * d_conf
    w_main = jnp.concatenate(
        [w_in[:, :d_ssm], w_in[:, c_dt + 2 * n_heads:], w_in[:, d_ssm:c_dt]], axis=1).astype(BF16)
    wf = w_in[:, c_dt:c_dt + n_heads]
    wb = w_in[:, c_dt + n_heads:c_dt + 2 * n_heads]
    zpad = jnp.zeros((d, LANES - 3 * n_heads), F32)
    w_dt = jnp.concatenate([wf, wf, wf, zpad, wb, wb, wb, zpad], axis=1)
    rep3 = lambda v: jnp.concatenate([v, v, v, jnp.zeros((LANES - 3 * n_heads,), F32)])
    x2d = x.reshape(t, d)

    proj, dt2, dtt2 = _in_proj(x2d, row(norm_mix_w), w_main, w_dt, w_dt.T, _pick(t, (512, 256, 128)),
                               _pick(w_main.shape[1], (1024, 512, 256, 128)))
    proj3 = proj.reshape(bsz, seq, -1)

    xbc = _conv_ssm(proj3, conv_ssm_w, row(conv_ssm_b), col_xbc, _pick(seq, (512, 256, 128)), 512)
    y_conf = _conformer(proj3, conf_dw_w, row(conf_dw_b), row(conf_ln_w), row(conf_ln_b),
                        col_conf, col_conf + d_conf, _pick(seq, (256, 128)))

    e3 = ((jnp.arange(LANES)[:, None] % n_heads == jnp.arange(d_ssm)[None, :] // HEAD_DIM)
          & (jnp.arange(LANES)[:, None] < 3 * n_heads)).astype(BF16)
    prm = {}
    for name, bias, alog in (("f", dt_bias_fwd, a_log_fwd), ("b", dt_bias_bwd, a_log_bwd)):
        b3 = rep3(bias.astype(F32))
        a3 = rep3(alog.astype(F32))
        prm[name] = (b3.reshape(1, LANES), jnp.broadcast_to(b3[:, None], (LANES, CHUNK)),
                     a3.reshape(1, LANES), jnp.broadcast_to(a3[:, None], (LANES, CHUNK)))
    y_f = _ssd(xbc, dt2, dtt2, *prm["f"], e3, reverse=False, n_heads=n_heads)
    dskip_x = jnp.repeat(d_skip.astype(F32), HEAD_DIM).reshape(1, d_ssm)
    y_ssm = _ssd(xbc, dt2, dtt2, *prm["b"], e3, reverse=True, n_heads=n_heads,
                 final_args=(y_f, proj3, dskip_x, row(ssm_norm_w)))

    wr = jnp.zeros((d, LANES), F32).at[:, :n_experts].set(w_router.astype(F32))
    br = jnp.zeros((1, LANES), F32).at[0, :n_experts].set(b_router.astype(F32))
    x1, logits = _out_proj(y_ssm.reshape(t, d_ssm), y_conf.reshape(t, d_conf), w_out.astype(BF16), x2d,
                           row(norm_ffn_w), wr, br, _pick(t, (256, 128)))

    bm = MOE_ROWS
    n_blocks = -(-(t * TOP_K + n_experts * (bm - 1)) // bm)
    gates, dest_kt, row_tok, block_expert, n_used = _route(logits[:, :n_experts], n_experts, bm, n_blocks)
    xs = _dispatch(row_tok, x1, row(norm_ffn_w), bm)
    act = _moe_up(block_expert, n_used, xs, w_gate, b_gate[:, None, :], w_up, b_up[:, None, :], bm, 512)
    y_rows = _moe_down(block_expert, n_used, act, w_down, b_down[:, None, :], bm, 512)
    return x1, gates, dest_kt, y_rows


def kernel(x, norm_mix_w, w_in, conv_ssm_w, conv_ssm_b, dt_bias_fwd, dt_bias_bwd, a_log_fwd, a_log_bwd, d_skip,
           ssm_norm_w, conf_dw_w, conf_dw_b, conf_ln_w, conf_ln_b, w_out, norm_ffn_w, w_router, b_router, w_gate,
           b_gate, w_up, b_up, w_down, b_down, norm_final_w):
    assert w_in.shape[0] == 1, "a single layer is supported"
    bsz, seq, d = x.shape
    t = bsz * seq
    x1, gates, dest_kt, y_rows = _layer(
        x, norm_mix_w[0], w_in[0], conv_ssm_w[0], conv_ssm_b[0], dt_bias_fwd[0], dt_bias_bwd[0], a_log_fwd[0],
        a_log_bwd[0], d_skip[0], ssm_norm_w[0], conf_dw_w[0], conf_dw_b[0], conf_ln_w[0], conf_ln_b[0], w_out[0],
        norm_ffn_w[0], w_router[0], b_router[0], w_gate[0], b_gate[0], w_up[0], b_up[0], w_down[0], b_down[0])
    out = _combine(dest_kt, y_rows, x1, gates, norm_final_w.reshape(1, d).astype(F32), _pick(t, (256, 128)))
    return out.reshape(bsz, seq, d)
```

```python
import functools
import math

import jax
import jax.numpy as jnp
from jax import lax
from jax.experimental import pallas as pl
from jax.experimental.pallas import tpu as pltpu

F32 = jnp.float32
BF16 = jnp.bfloat16

EPS = 1e-5
HEAD_DIM = 64
SSM_GROUPS = 4
D_STATE = 128
CHUNK = 128
SSM_CONV = 7
CONF_KERNEL = 31
TOP_K = 4
SWIGLU_LIMIT = 7.0
SWIGLU_ALPHA = 1.702

LANES = 128
SUBLANES = 8
VMEM_LIMIT_BYTES = 56 * 1024 * 1024
HALO = 16
MOE_ROWS = 512

HIGHEST = lax.Precision.HIGHEST


def _cparams(sem):
    return pltpu.CompilerParams(dimension_semantics=sem, vmem_limit_bytes=VMEM_LIMIT_BYTES)


def _sigmoid(v):
    return 1.0 / (1.0 + jnp.exp(-v))


def _silu(v):
    return v * _sigmoid(v)


def _softplus(v):
    return jnp.maximum(v, 0.0) + jnp.log(1.0 + jnp.exp(-jnp.abs(v)))


def _split2(v):
    hi = v.astype(BF16)
    lo = (v - hi.astype(F32)).astype(BF16)
    return hi, lo


def _dot_split(a_hi, a_lo, b_hi_ref, b_lo_ref):
    b_hi = b_hi_ref[...]
    acc = jnp.dot(a_hi, b_hi, preferred_element_type=F32)
    acc = acc + jnp.dot(a_lo, b_hi, preferred_element_type=F32)
    return acc + jnp.dot(a_hi, b_lo_ref[...], preferred_element_type=F32)


def _inproj_kernel(x_ref, nw_ref, w_ref, wdt_hi_ref, wdt_lo_ref, o_ref, dt_ref, dtt_ref, h_scr):
    @pl.when(pl.program_id(1) == 0)
    def _():
        x = x_ref[...]
        ms = jnp.mean(x * x, axis=-1, keepdims=True)
        h = x * lax.rsqrt(ms + EPS) * nw_ref[...]
        h_hi, h_lo = _split2(h)
        h_scr[...] = h_hi
        dt = _dot_split(h_hi, h_lo, wdt_hi_ref, wdt_lo_ref)
        dt_ref[...] = dt
        dtt_ref[...] = dt.T

    o_ref[...] = jnp.dot(h_scr[...], w_ref[...], preferred_element_type=F32).astype(o_ref.dtype)


def _in_proj(x2d, norm_w, w_main, w_dt_hi, w_dt_lo, tm, tn):
    t, d = x2d.shape
    n_main = w_main.shape[1]
    n_dt = w_dt_hi.shape[1]
    return pl.pallas_call(
        _inproj_kernel,
        grid=(t // tm, n_main // tn),
        in_specs=[
            pl.BlockSpec((tm, d), lambda m, n: (m, 0)),
            pl.BlockSpec((1, d), lambda m, n: (0, 0)),
            pl.BlockSpec((d, tn), lambda m, n: (0, n)),
            pl.BlockSpec((d, n_dt), lambda m, n: (0, 0)),
            pl.BlockSpec((d, n_dt), lambda m, n: (0, 0)),
        ],
        out_specs=[
            pl.BlockSpec((tm, tn), lambda m, n: (m, n)),
            pl.BlockSpec((tm, n_dt), lambda m, n: (m, 0)),
            pl.BlockSpec((n_dt, tm), lambda m, n: (0, m)),
        ],
        out_shape=[
            jax.ShapeDtypeStruct((t, n_main), BF16),
            jax.ShapeDtypeStruct((t, n_dt), F32),
            jax.ShapeDtypeStruct((n_dt, t), F32),
        ],
        scratch_shapes=[pltpu.VMEM((tm, d), BF16)],
        compiler_params=_cparams(("parallel", "arbitrary")),
        name="in_proj",
    )(x2d, norm_w, w_main, w_dt_hi, w_dt_lo)


def _fill_halo_scratch(scr, cur, prev, nxt, lt):
    l = pl.program_id(1)
    last = pl.num_programs(1) - 1
    scr[HALO:HALO + lt, :] = cur
    scr[0:HALO, :] = jnp.where(l > 0, prev, 0.0)
    scr[HALO + lt:HALO + lt + HALO, :] = jnp.where(l < last, nxt, 0.0)


def _conv_ssm_kernel(cur_ref, prev_ref, next_ref, w_ref, b_ref, o_ref, scr, *, lt, rc):
    _fill_halo_scratch(scr, cur_ref[0].astype(F32), prev_ref[0].astype(F32), next_ref[0].astype(F32), lt)
    pad = SSM_CONV // 2
    w = w_ref[...]
    bias = b_ref[...]

    def body(i, carry):
        r0 = pl.multiple_of(i * rc, rc)
        win = scr[pl.ds(r0 + (HALO - SUBLANES), rc + 2 * SUBLANES), :]
        acc = jnp.zeros((rc, scr.shape[1]), F32)
        for k in range(SSM_CONV):
            o = SUBLANES - pad + k
            acc = acc + w[k:k + 1, :] * win[o:o + rc, :]
        o_ref[0, pl.ds(r0, rc), :] = _silu(acc + bias).astype(o_ref.dtype)
        return carry

    lax.fori_loop(0, lt // rc, body, 0)


def _conv_ssm(proj3, conv_w, conv_b, col0, lt, ct, rc=32):
    bsz, seq, _ = proj3.shape
    c = conv_w.shape[1]
    cb0 = col0 // ct
    hb = lt // HALO
    n_hb = seq // HALO
    kern = functools.partial(_conv_ssm_kernel, lt=lt, rc=rc)
    return pl.pallas_call(
        kern,
        grid=(bsz, seq // lt, c // ct),
        in_specs=[
            pl.BlockSpec((1, lt, ct), lambda b, l, j: (b, l, cb0 + j)),
            pl.BlockSpec((1, HALO, ct), lambda b, l, j: (b, jnp.maximum(l * hb - 1, 0), cb0 + j)),
            pl.BlockSpec((1, HALO, ct), lambda b, l, j: (b, jnp.minimum((l + 1) * hb, n_hb - 1), cb0 + j)),
            pl.BlockSpec((SSM_CONV, ct), lambda b, l, j: (0, j)),
            pl.BlockSpec((1, ct), lambda b, l, j: (0, j)),
        ],
        out_specs=pl.BlockSpec((1, lt, ct), lambda b, l, j: (b, l, j)),
        out_shape=jax.ShapeDtypeStruct((bsz, seq, c), BF16),
        scratch_shapes=[pltpu.VMEM((lt + 2 * HALO, ct), F32)],
        compiler_params=_cparams(("parallel", "parallel", "parallel")),
        name="conv_ssm",
    )(proj3, proj3, proj3, conv_w, conv_b)


def _conformer_kernel(a_ref, g_ref, ap_ref, gp_ref, an_ref, gn_ref, w_ref, b_ref, lnw_ref, lnb_ref,
                      o_ref, scr, sh_scr, v_scr, *, lt, rc, lc):
    def glu(a, g):
        return a.astype(F32) * _sigmoid(g.astype(F32))

    _fill_halo_scratch(scr, glu(a_ref[0], g_ref[0]), glu(ap_ref[0], gp_ref[0]), glu(an_ref[0], gn_ref[0]), lt)
    c = scr.shape[1]
    pad = CONF_KERNEL // 2
    n_sh = lt + 2 * HALO - SUBLANES
    for r0 in range(0, n_sh, rc):
        n = min(rc, n_sh - r0)
        for j in range(c // lc):
            win = scr[r0:r0 + n + SUBLANES, j * lc:(j + 1) * lc]
            for m in range(1, SUBLANES):
                sh_scr[m - 1, r0:r0 + n, j * lc:(j + 1) * lc] = win[m:m + n, :]

    def conv_body(i, carry):
        r0 = pl.multiple_of(i * rc, rc)
        for j in range(c // lc):
            lanes = slice(j * lc, (j + 1) * lc)
            acc = jnp.zeros((rc, lc), F32)
            for k in range(CONF_KERNEL):
                o = HALO - pad + k
                m = o % SUBLANES
                base = o - m
                if m == 0:
                    u = scr[pl.ds(r0 + base, rc), lanes]
                else:
                    u = sh_scr[m - 1, pl.ds(r0 + base, rc), lanes]
                acc = acc + w_ref[k:k + 1, lanes] * u
            v_scr[pl.ds(r0, rc), lanes] = acc + b_ref[:, lanes]
        return carry

    lax.fori_loop(0, lt // rc, conv_body, 0)
    lnw = lnw_ref[...]
    lnb = lnb_ref[...]
    rn = 2 * SUBLANES

    def norm_body(i, carry):
        r0 = pl.multiple_of(i * rn, rn)
        v = v_scr[pl.ds(r0, rn), :]
        mu = jnp.mean(v, axis=-1, keepdims=True)
        vc = v - mu
        var = jnp.mean(vc * vc, axis=-1, keepdims=True)
        y = vc * lax.rsqrt(var + EPS) * lnw + lnb
        o_ref[0, pl.ds(r0, rn), :] = _silu(y).astype(o_ref.dtype)
        return carry

    lax.fori_loop(0, lt // rn, norm_body, 0)


def _conformer(proj3, dw_w, dw_b, ln_w, ln_b, col_a, col_g, lt, rc=64, lc=512):
    bsz, seq, _ = proj3.shape
    c = dw_w.shape[1]
    ja = col_a // c
    jg = col_g // c
    hb = lt // HALO
    n_hb = seq // HALO
    kern = functools.partial(_conformer_kernel, lt=lt, rc=rc, lc=lc)

    def cur(j):
        return pl.BlockSpec((1, lt, c), lambda b, l: (b, l, j))

    def prev(j):
        return pl.BlockSpec((1, HALO, c), lambda b, l: (b, jnp.maximum(l * hb - 1, 0), j))

    def nxt(j):
        return pl.BlockSpec((1, HALO, c), lambda b, l: (b, jnp.minimum((l + 1) * hb, n_hb - 1), j))

    vec = pl.BlockSpec((1, c), lambda b, l: (0, 0))
    return pl.pallas_call(
        kern,
        grid=(bsz, seq // lt),
        in_specs=[cur(ja), cur(jg), prev(ja), prev(jg), nxt(ja), nxt(jg),
                  pl.BlockSpec((CONF_KERNEL, c), lambda b, l: (0, 0)), vec, vec, vec],
        out_specs=pl.BlockSpec((1, lt, c), lambda b, l: (b, l, 0)),
        out_shape=jax.ShapeDtypeStruct((bsz, seq, c), BF16),
        scratch_shapes=[pltpu.VMEM((lt + 2 * HALO, c), F32),
                        pltpu.VMEM((SUBLANES - 1, lt + 2 * HALO, c), F32),
                        pltpu.VMEM((lt, c), F32)],
        compiler_params=_cparams(("parallel", "parallel")),
        name="conformer",
    )(proj3, proj3, proj3, proj3, proj3, proj3, dw_w, dw_b, ln_w, ln_b)


def _split3(v):
    hi = v.astype(BF16).astype(F32)
    r = v - hi
    mid = r.astype(BF16).astype(F32)
    lo = (r - mid).astype(BF16).astype(F32)
    return hi, mid, lo


def _ssd_kernel(*refs, reverse, final, n_heads):
    if final:
        (xs_ref, bm_ref, cm_ref, dt_ref, dtt_ref, dtb_ref, dtbt_ref, alog_ref, alogt_ref, tri_ref, trit_ref,
         e3_ref, yf_ref, z_ref, dsk_ref, nw_ref, o_ref, state_scr) = refs
    else:
        (xs_ref, bm_ref, cm_ref, dt_ref, dtt_ref, dtb_ref, dtbt_ref, alog_ref, alogt_ref, tri_ref, trit_ref,
         e3_ref, o_ref, state_scr) = refs
    q = CHUNK
    hg = n_heads
    d_ssm = n_heads * HEAD_DIM
    gw = d_ssm // SSM_GROUPS

    @pl.when(pl.program_id(1) == 0)
    def _():
        state_scr[...] = jnp.zeros_like(state_scr)

    lane = lax.broadcasted_iota(jnp.int32, (q, LANES), 1)
    sub = lax.broadcasted_iota(jnp.int32, (LANES, q), 0)

    def sel3_lanes(v):
        hi, mid, lo = _split3(v)
        return jnp.where(lane < hg, hi, jnp.where(lane < 2 * hg, mid, jnp.where(lane < 3 * hg, lo, 0.0))).astype(BF16)

    def sel3_rows(v):
        hi, mid, lo = _split3(v)
        return jnp.where(sub < hg, hi, jnp.where(sub < 2 * hg, mid, jnp.where(sub < 3 * hg, lo, 0.0))).astype(BF16)

    a_row = -jnp.exp(alog_ref[...])
    dt = _softplus(dt_ref[...] + dtb_ref[...])
    da = dt * a_row
    a_col = -jnp.exp(alogt_ref[...])
    dat = _softplus(dtt_ref[...] + dtbt_ref[...]) * a_col

    tri = tri_ref[...]
    cum3 = jnp.dot(tri, sel3_lanes(da), preferred_element_type=F32)
    cum = cum3
    for r in (1, 2, 3):
        cum = cum + pltpu.roll(cum3, r * hg, axis=1)
    cumt3 = jnp.dot(sel3_rows(dat), trit_ref[...], preferred_element_type=F32)
    cumt = cumt3[0:hg] + cumt3[hg:2 * hg] + cumt3[2 * hg:3 * hg]

    tot_row = 0 if reverse else q - 1
    total = cum[tot_row:tot_row + 1, :]

    e3 = e3_ref[...]
    lhs = jnp.concatenate(
        [sel3_lanes(dt), sel3_lanes(jnp.exp(cum)), sel3_lanes(jnp.exp(total - cum)),
         sel3_lanes(jnp.broadcast_to(jnp.exp(total), (q, LANES)))], axis=0)
    ex = jnp.dot(lhs, e3, preferred_element_type=F32)
    dt_x = ex[0:q]
    ecum_x = ex[q:2 * q]
    edte_x = ex[2 * q:3 * q]
    cdec_x = ex[3 * q:3 * q + 1]

    xs = xs_ref[0].astype(F32)
    xdt = xs * dt_x
    xdt_b = xdt.astype(BF16)
    xdte_b = (xdt * edte_x).astype(BF16)
    bm = bm_ref[0]
    cm = cm_ref[0]

    li = lax.broadcasted_iota(jnp.int32, (q, q), 0)
    si = lax.broadcasted_iota(jnp.int32, (q, q), 1)
    mask = (li <= si) if reverse else (li >= si)
    lane_lo = lax.broadcasted_iota(jnp.int32, (q, LANES), 1) < HEAD_DIM

    hpg = n_heads // SSM_GROUPS
    y_parts = []
    for g in range(SSM_GROUPS):
        bg = bm[:, g * D_STATE:(g + 1) * D_STATE]
        cg = cm[:, g * D_STATE:(g + 1) * D_STATE]
        cb = lax.dot_general(cg, bg, (((1,), (1,)), ((), ())), preferred_element_type=F32)
        st = state_scr[:, g * gw:(g + 1) * gw]
        y_off = jnp.dot(cg, st.astype(BF16), preferred_element_type=F32) * ecum_x[:, g * gw:(g + 1) * gw]
        diag = []
        for pair in range(hpg // 2):
            h0 = g * hpg + 2 * pair
            ls = []
            for h in (h0, h0 + 1):
                seg = cum[:, h:h + 1] - cumt[h:h + 1, :]
                ls.append((cb * jnp.exp(jnp.where(mask, seg, -jnp.inf))).astype(BF16))
            l2 = jnp.concatenate(ls, axis=1)
            xp = xdt_b[:, h0 * HEAD_DIM:(h0 + 2) * HEAD_DIM]
            zero = jnp.zeros_like(xp)
            r2 = jnp.concatenate([jnp.where(lane_lo, xp, zero), jnp.where(lane_lo, zero, xp)], axis=0)
            diag.append(jnp.dot(l2, r2, preferred_element_type=F32))
        y_parts.append(jnp.concatenate(diag, axis=1) + y_off)
        contrib = lax.dot_general(bg, xdte_b[:, g * gw:(g + 1) * gw], (((0,), (0,)), ((), ())),
                                  preferred_element_type=F32)
        state_scr[:, g * gw:(g + 1) * gw] = st * cdec_x[:, g * gw:(g + 1) * gw] + contrib
    y = jnp.concatenate(y_parts, axis=1)

    if not final:
        o_ref[0] = y
    else:
        y = y + yf_ref[0] + dsk_ref[...] * xs
        y = y * _silu(z_ref[0].astype(F32))
        outs = []
        for g in range(SSM_GROUPS):
            yg = y[:, g * gw:(g + 1) * gw]
            ms = jnp.mean(yg * yg, axis=-1, keepdims=True)
            outs.append(yg * lax.rsqrt(ms + EPS))
        o_ref[0] = (jnp.concatenate(outs, axis=1) * nw_ref[...]).astype(o_ref.dtype)


def _ssd(xbc3, dt2, dtt2, dtb, dtbt, alog, alogt, e3, *, reverse, n_heads, final_args=None):
    bsz, seq, _ = xbc3.shape
    d_ssm = n_heads * HEAD_DIM
    gn = SSM_GROUPS * D_STATE
    nc = seq // CHUNK
    q = CHUNK
    d = 1 if reverse else 0
    final = final_args is not None

    def cidx(c):
        return (nc - 1 - c) if reverse else c

    li = jnp.arange(q)[:, None]
    ji = jnp.arange(q)[None, :]
    tri = ((ji >= li) if reverse else (ji <= li)).astype(BF16)
    trit = tri.T

    const = lambda shape: pl.BlockSpec(shape, lambda b, c: (0,) * len(shape))
    in_specs = [
        pl.BlockSpec((1, q, d_ssm), lambda b, c: (b, cidx(c), 0)),
        pl.BlockSpec((1, q, gn), lambda b, c: (b, cidx(c), d_ssm // gn)),
        pl.BlockSpec((1, q, gn), lambda b, c: (b, cidx(c), d_ssm // gn + 1)),
        pl.BlockSpec((q, LANES), lambda b, c: (b * nc + cidx(c), d)),
        pl.BlockSpec((LANES, q), lambda b, c: (d, b * nc + cidx(c))),
        const((1, LANES)), const((LANES, q)), const((1, LANES)), const((LANES, q)),
        const((q, q)), const((q, q)), const((LANES, d_ssm)),
    ]
    args = [xbc3, xbc3, xbc3, dt2, dtt2, dtb, dtbt, alog, alogt, tri, trit, e3]
    if final:
        yf3, proj3, dskip_x, norm_w = final_args
        in_specs += [
            pl.BlockSpec((1, q, d_ssm), lambda b, c: (b, cidx(c), 0)),
            pl.BlockSpec((1, q, d_ssm), lambda b, c: (b, cidx(c), 0)),
            const((1, d_ssm)), const((1, d_ssm)),
        ]
        args += [yf3, proj3, dskip_x, norm_w]
    kern = functools.partial(_ssd_kernel, reverse=reverse, final=final, n_heads=n_heads)
    return pl.pallas_call(
        kern,
        grid=(bsz, nc),
        in_specs=in_specs,
        out_specs=pl.BlockSpec((1, q, d_ssm), lambda b, c: (b, cidx(c), 0)),
        out_shape=jax.ShapeDtypeStruct((bsz, seq, d_ssm), BF16 if final else F32),
        scratch_shapes=[pltpu.VMEM((D_STATE, d_ssm), F32)],
        compiler_params=_cparams(("parallel", "arbitrary")),
        name="ssd_bwd" if reverse else "ssd_fwd",
    )(*args)


def _outproj_kernel(ys_ref, yc_ref, wa_ref, wb_ref, x_ref, nw_ref, wr_hi_ref, wr_lo_ref, br_ref, x1_ref, lg_ref):
    acc = jnp.dot(ys_ref[...], wa_ref[...], preferred_element_type=F32)
    acc = acc + jnp.dot(yc_ref[...], wb_ref[...], preferred_element_type=F32)
    x1 = x_ref[...] + acc
    x1_ref[...] = x1
    ms = jnp.mean(x1 * x1, axis=-1, keepdims=True)
    h_hi, h_lo = _split2(x1 * lax.rsqrt(ms + EPS) * nw_ref[...])
    lg_ref[...] = _dot_split(h_hi, h_lo, wr_hi_ref, wr_lo_ref) + br_ref[...]


def _out_proj(y_ssm, y_conf, w_out, x2d, norm_w, wr_hi, wr_lo, b_router, tm):
    t, d = x2d.shape
    ka = y_ssm.shape[1]
    ne = wr_hi.shape[1]
    resident = lambda shape, idx: pl.BlockSpec(shape, lambda m: idx, pipeline_mode=pl.Buffered(1))
    return pl.pallas_call(
        _outproj_kernel,
        grid=(t // tm,),
        in_specs=[
            pl.BlockSpec((tm, ka), lambda m: (m, 0)),
            pl.BlockSpec((tm, ka), lambda m: (m, 0)),
            resident((ka, d), (0, 0)),
            resident((ka, d), (1, 0)),
            pl.BlockSpec((tm, d), lambda m: (m, 0)),
            pl.BlockSpec((1, d), lambda m: (0, 0)),
            resident((d, ne), (0, 0)),
            resident((d, ne), (0, 0)),
            pl.BlockSpec((1, ne), lambda m: (0, 0)),
        ],
        out_specs=[
            pl.BlockSpec((tm, d), lambda m: (m, 0)),
            pl.BlockSpec((tm, ne), lambda m: (m, 0)),
        ],
        out_shape=[
            jax.ShapeDtypeStruct((t, d), F32),
            jax.ShapeDtypeStruct((t, ne), F32),
        ],
        compiler_params=_cparams(("parallel",)),
        name="out_proj",
    )(y_ssm, y_conf, w_out, w_out, x2d, norm_w, wr_hi, wr_lo, b_router)


def _dispatch_kernel(nu_ref, idx_ref, idxn_ref, x1_hbm, nw_ref, o_ref, buf, sem, *, bm):
    i = pl.program_id(0)
    nu = nu_ref[0]
    slot = lax.rem(i, 2)

    def start_block(ids_ref, s):
        def body(j, carry):
            r = ids_ref[0, 0, j]
            pltpu.make_async_copy(x1_hbm.at[pl.ds(r, 1), :], buf.at[s, pl.ds(j, 1), :], sem.at[s]).start()
            return carry

        lax.fori_loop(0, bm, body, 0, unroll=8)

    @pl.when(i == 0)
    def _():
        start_block(idx_ref, 0)

    @pl.when(i + 1 < nu)
    def _():
        start_block(idxn_ref, 1 - slot)

    @pl.when(i < nu)
    def _():
        pltpu.make_async_copy(x1_hbm.at[pl.ds(0, bm), :], buf.at[slot], sem.at[slot]).wait()
        x = buf[slot]
        ms = jnp.mean(x * x, axis=-1, keepdims=True)
        o_ref[...] = (x * lax.rsqrt(ms + EPS) * nw_ref[...]).astype(o_ref.dtype)


def _dispatch(row_tok, n_used, x1, norm_w, bm):
    n_rows = row_tok.shape[0]
    t, d = x1.shape
    nb = n_rows // bm
    idx3 = row_tok.reshape(nb, 1, bm)
    kern = functools.partial(_dispatch_kernel, bm=bm)
    return pl.pallas_call(
        kern,
        grid_spec=pltpu.PrefetchScalarGridSpec(
            num_scalar_prefetch=1,
            grid=(nb,),
            in_specs=[
                pl.BlockSpec((1, 1, bm), lambda i, nu: (i, 0, 0), memory_space=pltpu.SMEM),
                pl.BlockSpec((1, 1, bm), lambda i, nu: (jnp.minimum(i + 1, nb - 1), 0, 0), memory_space=pltpu.SMEM),
                pl.BlockSpec(memory_space=pl.ANY),
                pl.BlockSpec((1, d), lambda i, nu: (0, 0)),
            ],
            out_specs=pl.BlockSpec((bm, d), lambda i, nu: (jnp.minimum(i, nu[0] - 1), 0)),
            scratch_shapes=[pltpu.VMEM((2, bm, d), F32), pltpu.SemaphoreType.DMA((2,))],
        ),
        out_shape=jax.ShapeDtypeStruct((n_rows, d), BF16),
        compiler_params=_cparams(("arbitrary",)),
        name="moe_dispatch",
    )(n_used, idx3, idx3, x1, norm_w)


def _moe_up_kernel(be_ref, nu_ref, x_ref, wg_ref, bg_ref, wu_ref, bu_ref, o_ref):
    @pl.when(pl.program_id(1) < nu_ref[0])
    def _():
        x = x_ref[...]
        gate = jnp.dot(x, wg_ref[0].astype(BF16), preferred_element_type=F32) + bg_ref[0]
        up = jnp.dot(x, wu_ref[0].astype(BF16), preferred_element_type=F32) + bu_ref[0]
        gate = jnp.minimum(gate, SWIGLU_LIMIT)
        up = jnp.clip(up, -SWIGLU_LIMIT, SWIGLU_LIMIT)
        act = (up + 1.0) * gate * _sigmoid(SWIGLU_ALPHA * gate)
        o_ref[...] = act.astype(o_ref.dtype)


def _moe_down_kernel(be_ref, nu_ref, a_ref, wd_ref, bd_ref, o_ref):
    @pl.when(pl.program_id(1) < nu_ref[0])
    def _():
        y = jnp.dot(a_ref[...], wd_ref[0].astype(BF16), preferred_element_type=F32) + bd_ref[0]
        o_ref[...] = y.astype(o_ref.dtype)


def _blk(b, nu_ref):
    return jnp.minimum(b, nu_ref[0] - 1)


def _moe_up(block_expert, n_used, xs, w_gate, b_gate3, w_up, b_up3, bm, tn):
    n_rows, d = xs.shape
    dff = w_gate.shape[2]
    nb = n_rows // bm
    wspec = pl.BlockSpec((1, d, tn), lambda n, b, be, nu: (be[_blk(b, nu)], 0, n))
    bspec = pl.BlockSpec((1, 1, tn), lambda n, b, be, nu: (be[_blk(b, nu)], 0, n))
    return pl.pallas_call(
        _moe_up_kernel,
        grid_spec=pltpu.PrefetchScalarGridSpec(
            num_scalar_prefetch=2,
            grid=(dff // tn, nb),
            in_specs=[pl.BlockSpec((bm, d), lambda n, b, be, nu: (_blk(b, nu), 0)), wspec, bspec, wspec, bspec],
            out_specs=pl.BlockSpec((bm, tn), lambda n, b, be, nu: (_blk(b, nu), n)),
        ),
        out_shape=jax.ShapeDtypeStruct((n_rows, dff), BF16),
        compiler_params=_cparams(("parallel", "arbitrary")),
        name="moe_up",
    )(block_expert, n_used, xs, w_gate, b_gate3, w_up, b_up3)


def _moe_down(block_expert, n_used, act, w_down, b_down3, bm, tn):
    n_rows, dff = act.shape
    d = w_down.shape[2]
    nb = n_rows // bm
    return pl.pallas_call(
        _moe_down_kernel,
        grid_spec=pltpu.PrefetchScalarGridSpec(
            num_scalar_prefetch=2,
            grid=(d // tn, nb),
            in_specs=[
                pl.BlockSpec((bm, dff), lambda n, b, be, nu: (_blk(b, nu), 0)),
                pl.BlockSpec((1, dff, tn), lambda n, b, be, nu: (be[_blk(b, nu)], 0, n)),
                pl.BlockSpec((1, 1, tn), lambda n, b, be, nu: (be[_blk(b, nu)], 0, n)),
            ],
            out_specs=pl.BlockSpec((bm, tn), lambda n, b, be, nu: (_blk(b, nu), n)),
        ),
        out_shape=jax.ShapeDtypeStruct((n_rows, d), F32),
        compiler_params=_cparams(("parallel", "arbitrary")),
        name="moe_down",
    )(block_expert, n_used, act, w_down, b_down3)


def _combine_kernel(idx_ref, idxn_ref, y_hbm, x1_ref, g_ref, nw_ref, o_ref, buf, sem, *, tm):
    i = pl.program_id(0)
    slot = lax.rem(i, 2)

    def start_tile(ids_ref, s):
        def body(t, carry):
            for k in range(TOP_K):
                r = ids_ref[0, k, t]
                pltpu.make_async_copy(y_hbm.at[pl.ds(r, 1), :], buf.at[s, pl.ds(k * tm + t, 1), :],
                                      sem.at[s]).start()
            return carry

        lax.fori_loop(0, tm, body, 0, unroll=2)

    @pl.when(i == 0)
    def _():
        start_tile(idx_ref, 0)

    @pl.when(i + 1 < pl.num_programs(0))
    def _():
        start_tile(idxn_ref, 1 - slot)

    pltpu.make_async_copy(y_hbm.at[pl.ds(0, TOP_K * tm), :], buf.at[slot], sem.at[slot]).wait()
    acc = x1_ref[...]
    g = g_ref[...]
    for k in range(TOP_K):
        acc = acc + g[:, k:k + 1] * buf[slot, k * tm:(k + 1) * tm, :]
    ms = jnp.mean(acc * acc, axis=-1, keepdims=True)
    o_ref[...] = acc * lax.rsqrt(ms + EPS) * nw_ref[...]


def _combine(dest_kt, y_rows, x1, gates, norm_w, tm):
    t, d = x1.shape
    nt = t // tm
    idx3 = dest_kt.reshape(TOP_K, nt, tm).transpose(1, 0, 2)
    kern = functools.partial(_combine_kernel, tm=tm)
    return pl.pallas_call(
        kern,
        grid=(nt,),
        in_specs=[
            pl.BlockSpec((1, TOP_K, tm), lambda i: (i, 0, 0), memory_space=pltpu.SMEM),
            pl.BlockSpec((1, TOP_K, tm), lambda i: (jnp.minimum(i + 1, nt - 1), 0, 0), memory_space=pltpu.SMEM),
            pl.BlockSpec(memory_space=pl.ANY),
            pl.BlockSpec((tm, d), lambda i: (i, 0)),
            pl.BlockSpec((tm, TOP_K), lambda i: (i, 0)),
            pl.BlockSpec((1, d), lambda i: (0, 0)),
        ],
        out_specs=pl.BlockSpec((tm, d), lambda i: (i, 0)),
        out_shape=jax.ShapeDtypeStruct((t, d), F32),
        scratch_shapes=[pltpu.VMEM((2, TOP_K * tm, d), F32), pltpu.SemaphoreType.DMA((2,))],
        compiler_params=_cparams(("arbitrary",)),
        name="moe_combine",
    )(idx3, idx3, y_rows, x1, gates, norm_w)


def _route(logits, n_experts, bm, n_blocks):
    t = logits.shape[0]
    top_vals, top_idx = lax.top_k(logits, TOP_K)
    gates = jax.nn.softmax(top_vals, axis=-1)
    flat_e = top_idx.reshape(-1)
    onehot = (flat_e[:, None] == jnp.arange(n_experts, dtype=flat_e.dtype)[None, :]).astype(jnp.int32)
    csum = jnp.cumsum(onehot, axis=0)
    rank = jnp.sum(csum * onehot, axis=1) - 1
    counts = csum[-1]
    padded = ((counts + bm - 1) // bm) * bm
    padded_end = jnp.cumsum(padded)
    padded_start = padded_end - padded
    dest = (padded_start[flat_e] + rank).astype(jnp.int32)
    row_tok = jnp.zeros((n_blocks * bm,), jnp.int32).at[dest].set(
        jnp.arange(t * TOP_K, dtype=jnp.int32) // TOP_K)
    n_used = (padded_end[-1] // bm).astype(jnp.int32)
    block_start = jnp.arange(n_blocks, dtype=jnp.int32) * bm
    block_expert = jnp.minimum(jnp.searchsorted(padded_end, block_start, side="right"), n_experts - 1)
    block_expert = block_expert.astype(jnp.int32)
    dest_kt = dest.reshape(t, TOP_K).T
    return gates, dest_kt, row_tok, block_expert, n_used.reshape(1)


def _pick(n, candidates):
    for c in candidates:
        if n % c == 0:
            return c
    return n


def _layer(x, norm_mix_w, w_in, conv_ssm_w, conv_ssm_b, dt_bias_fwd, dt_bias_bwd, a_log_fwd, a_log_bwd,
           d_skip, ssm_norm_w, conf_dw_w, conf_dw_b, conf_ln_w, conf_ln_b, w_out, norm_ffn_w, w_router,
           b_router, w_gate, b_gate, w_up, b_up, w_down, b_down):
    bsz, seq, d = x.shape
    t = bsz * seq
    n_heads = dt_bias_fwd.shape[0]
    d_ssm = n_heads * HEAD_DIM
    d_xbc = conv_ssm_w.shape[1]
    d_conf = conf_dw_w.shape[1]
    n_experts = w_gate.shape[0]
    row = lambda v: v.reshape(1, -1).astype(F32)

    c_dt = d_ssm + d_xbc
    col_conf = d_ssm
    col_xbc = d_ssm + 2 * d_conf
    w_main = jnp.concatenate(
        [w_in[:, :d_ssm], w_in[:, c_dt + 2 * n_heads:], w_in[:, d_ssm:c_dt]], axis=1).astype(BF16)
    wf = w_in[:, c_dt:c_dt + n_heads]
    wb = w_in[:, c_dt + n_heads:c_dt + 2 * n_heads]
    zpad = jnp.zeros((d, LANES - 3 * n_heads), F32)
    w_dt = jnp.concatenate([wf, wf, wf, zpad, wb, wb, wb, zpad], axis=1)
    rep3 = lambda v: jnp.concatenate([v, v, v, jnp.zeros((LANES - 3 * n_heads,), F32)])
    x2d = x.reshape(t, d)

    def split2(w):
        hi = w.astype(BF16)
        return hi, (w - hi.astype(F32)).astype(BF16)

    proj, dt2, dtt2 = _in_proj(x2d, row(norm_mix_w), w_main, *split2(w_dt), _pick(t, (512, 256, 128)),
                               _pick(w_main.shape[1], (1024, 512, 256, 128)))
    proj3 = proj.reshape(bsz, seq, -1)

    xbc = _conv_ssm(proj3, conv_ssm_w, row(conv_ssm_b), col_xbc, _pick(seq, (512, 256, 128)), 512)
    y_conf = _conformer(proj3, conf_dw_w, row(conf_dw_b), row(conf_ln_w), row(conf_ln_b),
                        col_conf, col_conf + d_conf, _pick(seq, (256, 128)))

    e3 = ((jnp.arange(LANES)[:, None] % n_heads == jnp.arange(d_ssm)[None, :] // HEAD_DIM)
          & (jnp.arange(LANES)[:, None] < 3 * n_heads)).astype(BF16)
    prm = {}
    for name, bias, alog in (("f", dt_bias_fwd, a_log_fwd), ("b", dt_bias_bwd, a_log_bwd)):
        b3 = rep3(bias.astype(F32))
        a3 = rep3(alog.astype(F32))
        prm[name] = (b3.reshape(1, LANES), jnp.broadcast_to(b3[:, None], (LANES, CHUNK)),
                     a3.reshape(1, LANES), jnp.broadcast_to(a3[:, None], (LANES, CHUNK)))
    y_f = _ssd(xbc, dt2, dtt2, *prm["f"], e3, reverse=False, n_heads=n_heads)
    dskip_x = jnp.repeat(d_skip.astype(F32), HEAD_DIM).reshape(1, d_ssm)
    y_ssm = _ssd(xbc, dt2, dtt2, *prm["b"], e3, reverse=True, n_heads=n_heads,
                 final_args=(y_f, proj3, dskip_x, row(ssm_norm_w)))

    wr = jnp.zeros((d, LANES), F32).at[:, :n_experts].set(w_router.astype(F32))
    br = jnp.zeros((1, LANES), F32).at[0, :n_experts].set(b_router.astype(F32))
    x1, logits = _out_proj(y_ssm.reshape(t, d_ssm), y_conf.reshape(t, d_conf), w_out.astype(BF16), x2d,
                           row(norm_ffn_w), *split2(wr), br, _pick(t, (512, 256, 128)))

    bm = MOE_ROWS
    n_blocks = -(-(t * TOP_K + n_experts * (bm - 1)) // bm)
    gates, dest_kt, row_tok, block_expert, n_used = _route(logits[:, :n_experts], n_experts, bm, n_blocks)
    xs = _dispatch(row_tok, n_used, x1, row(norm_ffn_w), bm)
    act = _moe_up(block_expert, n_used, xs, w_gate, b_gate[:, None, :], w_up, b_up[:, None, :], bm, 512)
    y_rows = _moe_down(block_expert, n_used, act, w_down, b_down[:, None, :], bm, 1024)
    return x1, gates, dest_kt, y_rows


def kernel(x, norm_mix_w, w_in, conv_ssm_w, conv_ssm_b, dt_bias_fwd, dt_bias_bwd, a_log_fwd, a_log_bwd, d_skip,
           ssm_norm_w, conf_dw_w, conf_dw_b, conf_ln_w, conf_ln_b, w_out, norm_ffn_w, w_router, b_router, w_gate,
           b_gate, w_up, b_up, w_down, b_down, norm_final_w):
    assert w_in.shape[0] == 1, "a single layer is supported"
    bsz, seq, d = x.shape
    t = bsz * seq
    x1, gates, dest_kt, y_rows = _layer(
        x, norm_mix_w[0], w_in[0], conv_ssm_w[0], conv_ssm_b[0], dt_bias_fwd[0], dt_bias_bwd[0], a_log_fwd[0],
        a_log_bwd[0], d_skip[0], ssm_norm_w[0], conf_dw_w[0], conf_dw_b[0], conf_ln_w[0], conf_ln_b[0], w_out[0],
        norm_ffn_w[0], w_router[0], b_router[0], w_gate[0], b_gate[0], w_up[0], b_up[0], w_down[0], b_down[0])
    out = _combine(dest_kt, y_rows, x1, gates, norm_final_w.reshape(1, d).astype(F32), _pick(t, (256, 128)))
    return out.reshape(bsz, seq, d)
```

```python
import functools
import math

import jax
import jax.numpy as jnp
from jax import lax
from jax.experimental import pallas as pl
from jax.experimental.pallas import tpu as pltpu

F32 = jnp.float32
BF16 = jnp.bfloat16

EPS = 1e-5
HEAD_DIM = 64
SSM_GROUPS = 4
D_STATE = 128
CHUNK = 128
SSM_CONV = 7
CONF_KERNEL = 31
TOP_K = 4
SWIGLU_LIMIT = 7.0
SWIGLU_ALPHA = 1.702

LANES = 128
SUBLANES = 8
VMEM_LIMIT_BYTES = 56 * 1024 * 1024
HALO = 16
MOE_ROWS = 512
MOE_DOWN_TILE = 1024

HIGHEST = lax.Precision.HIGHEST


def _cparams(sem):
    return pltpu.CompilerParams(dimension_semantics=sem, vmem_limit_bytes=VMEM_LIMIT_BYTES)


def _sigmoid(v):
    return 1.0 / (1.0 + jnp.exp(-v))


def _silu(v):
    return v * _sigmoid(v)


def _softplus(v):
    return jnp.maximum(v, 0.0) + jnp.log(1.0 + jnp.exp(-jnp.abs(v)))


def _split2(v):
    hi = v.astype(BF16)
    lo = (v - hi.astype(F32)).astype(BF16)
    return hi, lo


def _dot_split(a_hi, a_lo, b_hi_ref, b_lo_ref):
    b_hi = b_hi_ref[...]
    acc = jnp.dot(a_hi, b_hi, preferred_element_type=F32)
    acc = acc + jnp.dot(a_lo, b_hi, preferred_element_type=F32)
    return acc + jnp.dot(a_hi, b_lo_ref[...], preferred_element_type=F32)


def _inproj_kernel(x_ref, nw_ref, w_ref, wdt_hi_ref, wdt_lo_ref, o_ref, dt_ref, dtt_ref, h_scr):
    @pl.when(pl.program_id(1) == 0)
    def _():
        x = x_ref[...]
        ms = jnp.mean(x * x, axis=-1, keepdims=True)
        h = x * lax.rsqrt(ms + EPS) * nw_ref[...]
        h_hi, h_lo = _split2(h)
        h_scr[...] = h_hi
        dt = _dot_split(h_hi, h_lo, wdt_hi_ref, wdt_lo_ref)
        dt_ref[...] = dt
        dtt_ref[...] = dt.T

    o_ref[...] = jnp.dot(h_scr[...], w_ref[0], preferred_element_type=F32).astype(o_ref.dtype)


def _in_proj(x2d, norm_w, w_tiles, w_dt_hi, w_dt_lo, tm):
    t, d = x2d.shape
    n_tiles, _, tn = w_tiles.shape
    n_main = n_tiles * tn
    n_dt = w_dt_hi.shape[1]
    return pl.pallas_call(
        _inproj_kernel,
        grid=(t // tm, n_tiles),
        in_specs=[
            pl.BlockSpec((tm, d), lambda m, n: (m, 0)),
            pl.BlockSpec((1, d), lambda m, n: (0, 0)),
            pl.BlockSpec((1, d, tn), lambda m, n: (n, 0, 0)),
            pl.BlockSpec((d, n_dt), lambda m, n: (0, 0)),
            pl.BlockSpec((d, n_dt), lambda m, n: (0, 0)),
        ],
        out_specs=[
            pl.BlockSpec((tm, tn), lambda m, n: (m, n)),
            pl.BlockSpec((tm, n_dt), lambda m, n: (m, 0)),
            pl.BlockSpec((n_dt, tm), lambda m, n: (0, m)),
        ],
        out_shape=[
            jax.ShapeDtypeStruct((t, n_main), BF16),
            jax.ShapeDtypeStruct((t, n_dt), F32),
            jax.ShapeDtypeStruct((n_dt, t), F32),
        ],
        scratch_shapes=[pltpu.VMEM((tm, d), BF16)],
        compiler_params=_cparams(("parallel", "arbitrary")),
        name="in_proj",
    )(x2d, norm_w, w_tiles, w_dt_hi, w_dt_lo)


def _fill_halo_scratch(scr, cur, prev, nxt, lt):
    l = pl.program_id(1)
    last = pl.num_programs(1) - 1
    scr[HALO:HALO + lt, :] = cur
    scr[0:HALO, :] = jnp.where(l > 0, prev, 0.0)
    scr[HALO + lt:HALO + lt + HALO, :] = jnp.where(l < last, nxt, 0.0)


def _conv_ssm_kernel(cur_ref, prev_ref, next_ref, w_ref, b_ref, o_ref, scr, *, lt, rc):
    _fill_halo_scratch(scr, cur_ref[0].astype(F32), prev_ref[0].astype(F32), next_ref[0].astype(F32), lt)
    pad = SSM_CONV // 2
    w = w_ref[...]
    bias = b_ref[...]

    def body(i, carry):
        r0 = pl.multiple_of(i * rc, rc)
        win = scr[pl.ds(r0 + (HALO - SUBLANES), rc + 2 * SUBLANES), :]
        acc = jnp.zeros((rc, scr.shape[1]), F32)
        for k in range(SSM_CONV):
            o = SUBLANES - pad + k
            acc = acc + w[k:k + 1, :] * win[o:o + rc, :]
        o_ref[0, pl.ds(r0, rc), :] = _silu(acc + bias).astype(o_ref.dtype)
        return carry

    lax.fori_loop(0, lt // rc, body, 0)


def _conv_ssm(proj3, conv_w, conv_b, col0, lt, ct, rc=32):
    bsz, seq, _ = proj3.shape
    c = conv_w.shape[1]
    cb0 = col0 // ct
    hb = lt // HALO
    n_hb = seq // HALO
    kern = functools.partial(_conv_ssm_kernel, lt=lt, rc=rc)
    return pl.pallas_call(
        kern,
        grid=(bsz, seq // lt, c // ct),
        in_specs=[
            pl.BlockSpec((1, lt, ct), lambda b, l, j: (b, l, cb0 + j)),
            pl.BlockSpec((1, HALO, ct), lambda b, l, j: (b, jnp.maximum(l * hb - 1, 0), cb0 + j)),
            pl.BlockSpec((1, HALO, ct), lambda b, l, j: (b, jnp.minimum((l + 1) * hb, n_hb - 1), cb0 + j)),
            pl.BlockSpec((SSM_CONV, ct), lambda b, l, j: (0, j)),
            pl.BlockSpec((1, ct), lambda b, l, j: (0, j)),
        ],
        out_specs=pl.BlockSpec((1, lt, ct), lambda b, l, j: (b, l, j)),
        out_shape=jax.ShapeDtypeStruct((bsz, seq, c), BF16),
        scratch_shapes=[pltpu.VMEM((lt + 2 * HALO, ct), F32)],
        compiler_params=_cparams(("parallel", "parallel", "parallel")),
        name="conv_ssm",
    )(proj3, proj3, proj3, conv_w, conv_b)


def _conformer_kernel(a_ref, g_ref, ap_ref, gp_ref, an_ref, gn_ref, w_ref, b_ref, lnw_ref, lnb_ref,
                      o_ref, scr, sh_scr, v_scr, *, lt, rc, lc):
    def glu(a, g):
        return a.astype(F32) * _sigmoid(g.astype(F32))

    _fill_halo_scratch(scr, glu(a_ref[0], g_ref[0]), glu(ap_ref[0], gp_ref[0]), glu(an_ref[0], gn_ref[0]), lt)
    c = scr.shape[1]
    pad = CONF_KERNEL // 2
    n_sh = lt + 2 * HALO - SUBLANES
    for r0 in range(0, n_sh, rc):
        n = min(rc, n_sh - r0)
        for j in range(c // lc):
            win = scr[r0:r0 + n + SUBLANES, j * lc:(j + 1) * lc]
            for m in range(1, SUBLANES):
                sh_scr[m - 1, r0:r0 + n, j * lc:(j + 1) * lc] = win[m:m + n, :]

    def conv_body(i, carry):
        r0 = pl.multiple_of(i * rc, rc)
        for j in range(c // lc):
            lanes = slice(j * lc, (j + 1) * lc)
            acc = jnp.zeros((rc, lc), F32)
            for k in range(CONF_KERNEL):
                o = HALO - pad + k
                m = o % SUBLANES
                base = o - m
                if m == 0:
                    u = scr[pl.ds(r0 + base, rc), lanes]
                else:
                    u = sh_scr[m - 1, pl.ds(r0 + base, rc), lanes]
                acc = acc + w_ref[k:k + 1, lanes] * u
            v_scr[pl.ds(r0, rc), lanes] = acc + b_ref[:, lanes]
        return carry

    lax.fori_loop(0, lt // rc, conv_body, 0)
    lnw = lnw_ref[...]
    lnb = lnb_ref[...]
    rn = 2 * SUBLANES

    def norm_body(i, carry):
        r0 = pl.multiple_of(i * rn, rn)
        v = v_scr[pl.ds(r0, rn), :]
        mu = jnp.mean(v, axis=-1, keepdims=True)
        vc = v - mu
        var = jnp.mean(vc * vc, axis=-1, keepdims=True)
        y = vc * lax.rsqrt(var + EPS) * lnw + lnb
        o_ref[0, pl.ds(r0, rn), :] = _silu(y).astype(o_ref.dtype)
        return carry

    lax.fori_loop(0, lt // rn, norm_body, 0, unroll=4)


def _conformer(proj3, dw_w, dw_b, ln_w, ln_b, col_a, col_g, lt, rc=64, lc=512):
    bsz, seq, _ = proj3.shape
    c = dw_w.shape[1]
    ja = col_a // c
    jg = col_g // c
    hb = lt // HALO
    n_hb = seq // HALO
    kern = functools.partial(_conformer_kernel, lt=lt, rc=rc, lc=lc)

    def cur(j):
        return pl.BlockSpec((1, lt, c), lambda b, l: (b, l, j))

    def prev(j):
        return pl.BlockSpec((1, HALO, c), lambda b, l: (b, jnp.maximum(l * hb - 1, 0), j))

    def nxt(j):
        return pl.BlockSpec((1, HALO, c), lambda b, l: (b, jnp.minimum((l + 1) * hb, n_hb - 1), j))

    vec = pl.BlockSpec((1, c), lambda b, l: (0, 0))
    return pl.pallas_call(
        kern,
        grid=(bsz, seq // lt),
        in_specs=[cur(ja), cur(jg), prev(ja), prev(jg), nxt(ja), nxt(jg),
                  pl.BlockSpec((CONF_KERNEL, c), lambda b, l: (0, 0)), vec, vec, vec],
        out_specs=pl.BlockSpec((1, lt, c), lambda b, l: (b, l, 0)),
        out_shape=jax.ShapeDtypeStruct((bsz, seq, c), BF16),
        scratch_shapes=[pltpu.VMEM((lt + 2 * HALO, c), F32),
                        pltpu.VMEM((SUBLANES - 1, lt + 2 * HALO, c), F32),
                        pltpu.VMEM((lt, c), F32)],
        compiler_params=_cparams(("parallel", "parallel")),
        name="conformer",
    )(proj3, proj3, proj3, proj3, proj3, proj3, dw_w, dw_b, ln_w, ln_b)


def _split3(v):
    hi = v.astype(BF16).astype(F32)
    r = v - hi
    mid = r.astype(BF16).astype(F32)
    lo = (r - mid).astype(BF16).astype(F32)
    return hi, mid, lo


def _ssd_kernel(*refs, reverse, final, n_heads):
    if final:
        (xs_ref, bm_ref, cm_ref, dt_ref, dtt_ref, dtb_ref, dtbt_ref, alog_ref, alogt_ref, tri_ref, trit_ref,
         e3_ref, yf_ref, z_ref, dsk_ref, nw_ref, o_ref, state_scr) = refs
    else:
        (xs_ref, bm_ref, cm_ref, dt_ref, dtt_ref, dtb_ref, dtbt_ref, alog_ref, alogt_ref, tri_ref, trit_ref,
         e3_ref, o_ref, state_scr) = refs
    q = CHUNK
    hg = n_heads
    d_ssm = n_heads * HEAD_DIM
    gw = d_ssm // SSM_GROUPS

    @pl.when(pl.program_id(1) == 0)
    def _():
        state_scr[...] = jnp.zeros_like(state_scr)

    lane = lax.broadcasted_iota(jnp.int32, (q, LANES), 1)
    sub = lax.broadcasted_iota(jnp.int32, (LANES, q), 0)

    def sel3_lanes(v):
        hi, mid, lo = _split3(v)
        return jnp.where(lane < hg, hi, jnp.where(lane < 2 * hg, mid, jnp.where(lane < 3 * hg, lo, 0.0))).astype(BF16)

    def sel3_rows(v):
        hi, mid, lo = _split3(v)
        return jnp.where(sub < hg, hi, jnp.where(sub < 2 * hg, mid, jnp.where(sub < 3 * hg, lo, 0.0))).astype(BF16)

    a_row = -jnp.exp(alog_ref[...])
    dt = _softplus(dt_ref[...] + dtb_ref[...])
    da = dt * a_row
    a_col = -jnp.exp(alogt_ref[...])
    dat = _softplus(dtt_ref[...] + dtbt_ref[...]) * a_col

    tri = tri_ref[...]
    cum3 = jnp.dot(tri, sel3_lanes(da), preferred_element_type=F32)
    cum = cum3
    for r in (1, 2, 3):
        cum = cum + pltpu.roll(cum3, r * hg, axis=1)
    cumt3 = jnp.dot(sel3_rows(dat), trit_ref[...], preferred_element_type=F32)
    cumt = cumt3[0:hg] + cumt3[hg:2 * hg] + cumt3[2 * hg:3 * hg]

    tot_row = 0 if reverse else q - 1
    total = cum[tot_row:tot_row + 1, :]

    e3 = e3_ref[...]
    lhs = jnp.concatenate(
        [sel3_lanes(dt), sel3_lanes(jnp.exp(cum)), sel3_lanes(jnp.exp(total - cum)),
         sel3_lanes(jnp.broadcast_to(jnp.exp(total), (q, LANES)))], axis=0)
    ex = jnp.dot(lhs, e3, preferred_element_type=F32)
    dt_x = ex[0:q]
    ecum_x = ex[q:2 * q]
    edte_x = ex[2 * q:3 * q]
    cdec_x = ex[3 * q:3 * q + 1]

    xs = xs_ref[0].astype(F32)
    xdt = xs * dt_x
    xdt_b = xdt.astype(BF16)
    xdte_b = (xdt * edte_x).astype(BF16)
    bm = bm_ref[0]
    cm = cm_ref[0]

    li = lax.broadcasted_iota(jnp.int32, (q, q), 0)
    si = lax.broadcasted_iota(jnp.int32, (q, q), 1)
    mask = (li <= si) if reverse else (li >= si)
    lane_lo = lax.broadcasted_iota(jnp.int32, (q, LANES), 1) < HEAD_DIM

    hpg = n_heads // SSM_GROUPS
    y_parts = []
    for g in range(SSM_GROUPS):
        bg = bm[:, g * D_STATE:(g + 1) * D_STATE]
        cg = cm[:, g * D_STATE:(g + 1) * D_STATE]
        cb = lax.dot_general(cg, bg, (((1,), (1,)), ((), ())), preferred_element_type=F32)
        st = state_scr[:, g * gw:(g + 1) * gw]
        y_off = jnp.dot(cg, st.astype(BF16), preferred_element_type=F32) * ecum_x[:, g * gw:(g + 1) * gw]
        diag = []
        for pair in range(hpg // 2):
            h0 = g * hpg + 2 * pair
            ls = []
            for h in (h0, h0 + 1):
                seg = cum[:, h:h + 1] - cumt[h:h + 1, :]
                ls.append((cb * jnp.exp(jnp.where(mask, seg, -jnp.inf))).astype(BF16))
            l2 = jnp.concatenate(ls, axis=1)
            xp = xdt_b[:, h0 * HEAD_DIM:(h0 + 2) * HEAD_DIM]
            zero = jnp.zeros_like(xp)
            r2 = jnp.concatenate([jnp.where(lane_lo, xp, zero), jnp.where(lane_lo, zero, xp)], axis=0)
            diag.append(jnp.dot(l2, r2, preferred_element_type=F32))
        y_parts.append(jnp.concatenate(diag, axis=1) + y_off)
        contrib = lax.dot_general(bg, xdte_b[:, g * gw:(g + 1) * gw], (((0,), (0,)), ((), ())),
                                  preferred_element_type=F32)
        state_scr[:, g * gw:(g + 1) * gw] = st * cdec_x[:, g * gw:(g + 1) * gw] + contrib
    y = jnp.concatenate(y_parts, axis=1)

    if not final:
        o_ref[0] = y
    else:
        y = y + yf_ref[0] + dsk_ref[...] * xs
        y = y * _silu(z_ref[0].astype(F32))
        outs = []
        for g in range(SSM_GROUPS):
            yg = y[:, g * gw:(g + 1) * gw]
            ms = jnp.mean(yg * yg, axis=-1, keepdims=True)
            outs.append(yg * lax.rsqrt(ms + EPS))
        o_ref[0] = (jnp.concatenate(outs, axis=1) * nw_ref[...]).astype(o_ref.dtype)


def _ssd(xbc3, dt2, dtt2, dtb, dtbt, alog, alogt, e3, *, reverse, n_heads, final_args=None):
    bsz, seq, _ = xbc3.shape
    d_ssm = n_heads * HEAD_DIM
    gn = SSM_GROUPS * D_STATE
    nc = seq // CHUNK
    q = CHUNK
    d = 1 if reverse else 0
    final = final_args is not None

    def cidx(c):
        return (nc - 1 - c) if reverse else c

    li = jnp.arange(q)[:, None]
    ji = jnp.arange(q)[None, :]
    tri = ((ji >= li) if reverse else (ji <= li)).astype(BF16)
    trit = tri.T

    const = lambda shape: pl.BlockSpec(shape, lambda b, c: (0,) * len(shape))
    in_specs = [
        pl.BlockSpec((1, q, d_ssm), lambda b, c: (b, cidx(c), 0)),
        pl.BlockSpec((1, q, gn), lambda b, c: (b, cidx(c), d_ssm // gn)),
        pl.BlockSpec((1, q, gn), lambda b, c: (b, cidx(c), d_ssm // gn + 1)),
        pl.BlockSpec((q, LANES), lambda b, c: (b * nc + cidx(c), d)),
        pl.BlockSpec((LANES, q), lambda b, c: (d, b * nc + cidx(c))),
        const((1, LANES)), const((LANES, q)), const((1, LANES)), const((LANES, q)),
        const((q, q)), const((q, q)), const((LANES, d_ssm)),
    ]
    args = [xbc3, xbc3, xbc3, dt2, dtt2, dtb, dtbt, alog, alogt, tri, trit, e3]
    if final:
        yf3, proj3, dskip_x, norm_w = final_args
        in_specs += [
            pl.BlockSpec((1, q, d_ssm), lambda b, c: (b, cidx(c), 0)),
            pl.BlockSpec((1, q, d_ssm), lambda b, c: (b, cidx(c), 0)),
            const((1, d_ssm)), const((1, d_ssm)),
        ]
        args += [yf3, proj3, dskip_x, norm_w]
    kern = functools.partial(_ssd_kernel, reverse=reverse, final=final, n_heads=n_heads)
    return pl.pallas_call(
        kern,
        grid=(bsz, nc),
        in_specs=in_specs,
        out_specs=pl.BlockSpec((1, q, d_ssm), lambda b, c: (b, cidx(c), 0)),
        out_shape=jax.ShapeDtypeStruct((bsz, seq, d_ssm), BF16 if final else F32),
        scratch_shapes=[pltpu.VMEM((D_STATE, d_ssm), F32)],
        compiler_params=_cparams(("parallel", "arbitrary")),
        name="ssd_bwd" if reverse else "ssd_fwd",
    )(*args)


def _outproj_kernel(ys_ref, yc_ref, wa_ref, wb_ref, x_ref, nw_ref, wr_hi_ref, wr_lo_ref, br_ref, x1_ref, lg_ref):
    acc = jnp.dot(ys_ref[...], wa_ref[...], preferred_element_type=F32)
    acc = acc + jnp.dot(yc_ref[...], wb_ref[...], preferred_element_type=F32)
    x1 = x_ref[...] + acc
    x1_ref[...] = x1
    ms = jnp.mean(x1 * x1, axis=-1, keepdims=True)
    h_hi, h_lo = _split2(x1 * lax.rsqrt(ms + EPS) * nw_ref[...])
    lg_ref[...] = _dot_split(h_hi, h_lo, wr_hi_ref, wr_lo_ref) + br_ref[...]


def _out_proj(y_ssm, y_conf, w_out, x2d, norm_w, wr_hi, wr_lo, b_router, tm):
    t, d = x2d.shape
    ka = y_ssm.shape[1]
    ne = wr_hi.shape[1]
    resident = lambda shape, idx: pl.BlockSpec(shape, lambda m: idx, pipeline_mode=pl.Buffered(1))
    return pl.pallas_call(
        _outproj_kernel,
        grid=(t // tm,),
        in_specs=[
            pl.BlockSpec((tm, ka), lambda m: (m, 0)),
            pl.BlockSpec((tm, ka), lambda m: (m, 0)),
            resident((ka, d), (0, 0)),
            resident((ka, d), (1, 0)),
            pl.BlockSpec((tm, d), lambda m: (m, 0)),
            pl.BlockSpec((1, d), lambda m: (0, 0)),
            resident((d, ne), (0, 0)),
            resident((d, ne), (0, 0)),
            pl.BlockSpec((1, ne), lambda m: (0, 0)),
        ],
        out_specs=[
            pl.BlockSpec((tm, d), lambda m: (m, 0)),
            pl.BlockSpec((tm, ne), lambda m: (m, 0)),
        ],
        out_shape=[
            jax.ShapeDtypeStruct((t, d), F32),
            jax.ShapeDtypeStruct((t, ne), F32),
        ],
        compiler_params=_cparams(("parallel",)),
        name="out_proj",
    )(y_ssm, y_conf, w_out, w_out, x2d, norm_w, wr_hi, wr_lo, b_router)


def _router_kernel(lg_ref, tri_ref, ir_ref, gate_ref, cnt_ref, run_scr, *, n_experts):
    @pl.when(pl.program_id(0) == 0)
    def _():
        run_scr[...] = jnp.zeros_like(run_scr)

    tr = lg_ref.shape[0]
    lane = lax.broadcasted_iota(jnp.int32, (tr, LANES), 1)
    lane_f = lane.astype(F32)
    lg = jnp.where(lane < n_experts, lg_ref[...], -jnp.inf)
    vals, ids, hots = [], [], []
    for _ in range(TOP_K):
        m = jnp.max(lg, axis=1, keepdims=True)
        idx = jnp.min(jnp.where(lg == m, lane_f, float(LANES)), axis=1, keepdims=True)
        hot = lane_f == idx
        vals.append(m)
        ids.append(idx)
        hots.append(hot)
        lg = jnp.where(hot, -jnp.inf, lg)
    es = [jnp.exp(v - vals[0]) for v in vals]
    den = es[0]
    for e in es[1:]:
        den = den + e
    hot_any = hots[0].astype(F32)
    for hot in hots[1:]:
        hot_any = hot_any + hot.astype(F32)
    before = jnp.dot(tri_ref[...], hot_any.astype(BF16), preferred_element_type=F32) + run_scr[...]
    run_scr[...] = run_scr[...] + jnp.sum(hot_any, axis=0, keepdims=True)
    cnt_ref[...] = run_scr[...].astype(jnp.int32)
    out_i = jnp.zeros((tr, LANES), F32)
    out_g = jnp.zeros((tr, LANES), F32)
    for k in range(TOP_K):
        rank = jnp.sum(jnp.where(hots[k], before, 0.0), axis=1, keepdims=True)
        out_i = jnp.where(lane == k, ids[k], out_i)
        out_i = jnp.where(lane == TOP_K + k, rank, out_i)
        out_g = jnp.where(lane == k, es[k] / den, out_g)
    ir_ref[...] = out_i.astype(jnp.int32)
    gate_ref[...] = out_g


def _router(logits, n_experts, tr):
    t = logits.shape[0]
    tri = (jnp.arange(tr)[None, :] < jnp.arange(tr)[:, None]).astype(BF16)
    kern = functools.partial(_router_kernel, n_experts=n_experts)
    return pl.pallas_call(
        kern,
        grid=(t // tr,),
        in_specs=[pl.BlockSpec((tr, LANES), lambda i: (i, 0)), pl.BlockSpec((tr, tr), lambda i: (0, 0))],
        out_specs=[pl.BlockSpec((tr, LANES), lambda i: (i, 0)), pl.BlockSpec((tr, LANES), lambda i: (i, 0)),
                   pl.BlockSpec((1, LANES), lambda i: (0, 0))],
        out_shape=[jax.ShapeDtypeStruct((t, LANES), jnp.int32), jax.ShapeDtypeStruct((t, LANES), F32),
                   jax.ShapeDtypeStruct((1, LANES), jnp.int32)],
        scratch_shapes=[pltpu.VMEM((1, LANES), F32)],
        compiler_params=_cparams(("arbitrary",)),
        name="moe_router",
    )(logits, tri)


_HI16 = 0xFFFF0000


def _pack_bf16_pair(a, b):
    au = lax.bitcast_convert_type(a.astype(BF16).astype(F32), jnp.uint32)
    bu = lax.bitcast_convert_type(b.astype(BF16).astype(F32), jnp.uint32)
    return (au & jnp.uint32(_HI16)) | lax.shift_right_logical(bu, jnp.uint32(16))


def _unpack_bf16_pair(w):
    a = lax.bitcast_convert_type(w & jnp.uint32(_HI16), F32).astype(BF16)
    b = lax.bitcast_convert_type(lax.shift_left(w, jnp.uint32(16)), F32).astype(BF16)
    return a, b


def _dispatch_kernel(zs_ref, idx_ref, x1_ref, nw_ref, xs_hbm, zero_scr, pk, sem, zsem, *, tm, bm, n_experts):
    i = pl.program_id(0)
    last = pl.num_programs(0) - 1
    slot = lax.rem(i, 2)
    half = x1_ref.shape[1] // 2

    def zero_copy(e):
        return pltpu.make_async_copy(zero_scr, xs_hbm.at[pl.ds(pl.multiple_of(zs_ref[e], bm), bm), :], zsem)

    @pl.when(i == 0)
    def _():
        zero_scr[...] = jnp.zeros_like(zero_scr)
        for e in range(n_experts):
            @pl.when(zs_ref[e] >= 0)
            def _():
                zero_copy(e).start()
        for e in range(n_experts):
            @pl.when(zs_ref[e] >= 0)
            def _():
                zero_copy(e).wait()

    def wait_rows(s):
        rows = xs_hbm.at[pl.ds(0, TOP_K * tm), :]
        pltpu.make_async_copy(rows, rows, sem.at[s]).wait()

    @pl.when(i >= 2)
    def _():
        wait_rows(slot)

    x = x1_ref[...]
    ms = jnp.mean(x * x, axis=-1, keepdims=True)
    h = x * lax.rsqrt(ms + EPS) * nw_ref[...]
    pk[slot] = _pack_bf16_pair(h[:, :half], h[:, half:])

    def body(t, carry):
        for k in range(TOP_K):
            r = idx_ref[0, k, t]
            pltpu.make_async_copy(pk.at[slot, pl.ds(t, 1), :], xs_hbm.at[pl.ds(r, 1), :], sem.at[slot]).start()
        return carry

    lax.fori_loop(0, tm, body, 0, unroll=2)

    @pl.when(i == last)
    def _():
        wait_rows(slot)

    @pl.when(jnp.logical_and(i == last, i >= 1))
    def _():
        wait_rows(1 - slot)


def _dispatch(dest_kt, zero_start, x1, norm_w, n_rows, tm, bm):
    t, d = x1.shape
    nt = t // tm
    n_experts = zero_start.shape[0]
    idx3 = dest_kt.reshape(TOP_K, nt, tm).transpose(1, 0, 2)
    kern = functools.partial(_dispatch_kernel, tm=tm, bm=bm, n_experts=n_experts)
    return pl.pallas_call(
        kern,
        grid_spec=pltpu.PrefetchScalarGridSpec(
            num_scalar_prefetch=1,
            grid=(nt,),
            in_specs=[
                pl.BlockSpec((1, TOP_K, tm), lambda i, zs: (i, 0, 0), memory_space=pltpu.SMEM),
                pl.BlockSpec((tm, d), lambda i, zs: (i, 0)),
                pl.BlockSpec((1, d), lambda i, zs: (0, 0)),
            ],
            out_specs=pl.BlockSpec(memory_space=pl.ANY),
            scratch_shapes=[pltpu.VMEM((bm, d // 2), jnp.uint32), pltpu.VMEM((2, tm, d // 2), jnp.uint32),
                            pltpu.SemaphoreType.DMA((2,)), pltpu.SemaphoreType.DMA(())],
        ),
        out_shape=jax.ShapeDtypeStruct((n_rows, d // 2), jnp.uint32),
        compiler_params=_cparams(("arbitrary",)),
        name="moe_dispatch",
    )(zero_start, idx3, x1, norm_w)


def _moe_up_kernel(be_ref, nu_ref, x_ref, wg_ref, bg_ref, wu_ref, bu_ref, o_ref):
    @pl.when(pl.program_id(1) < nu_ref[0])
    def _():
        xa, xb = _unpack_bf16_pair(x_ref[...])
        half = xa.shape[1]
        wg = wg_ref[0].astype(BF16)
        wu = wu_ref[0].astype(BF16)
        gate = (jnp.dot(xa, wg[:half], preferred_element_type=F32)
                + jnp.dot(xb, wg[half:], preferred_element_type=F32) + bg_ref[0])
        up = (jnp.dot(xa, wu[:half], preferred_element_type=F32)
              + jnp.dot(xb, wu[half:], preferred_element_type=F32) + bu_ref[0])
        gate = jnp.minimum(gate, SWIGLU_LIMIT)
        up = jnp.clip(up, -SWIGLU_LIMIT, SWIGLU_LIMIT)
        act = (up + 1.0) * gate * _sigmoid(SWIGLU_ALPHA * gate)
        o_ref[...] = act.astype(o_ref.dtype)


def _moe_down_kernel(be_ref, nu_ref, a_ref, wd_ref, bd_ref, o_ref):
    @pl.when(pl.program_id(1) < nu_ref[0])
    def _():
        y = jnp.dot(a_ref[...], wd_ref[0].astype(BF16), preferred_element_type=F32) + bd_ref[0]
        half = y.shape[1] // 2
        o_ref[...] = _pack_bf16_pair(y[:, :half], y[:, half:])


def _blk(b, nu_ref):
    return jnp.minimum(b, nu_ref[0] - 1)


def _moe_up(block_expert, n_used, xs, w_gate, b_gate3, w_up, b_up3, bm, tn):
    n_rows, dp = xs.shape
    d, dff = w_gate.shape[1], w_gate.shape[2]
    nb = n_rows // bm
    wspec = pl.BlockSpec((1, d, tn), lambda n, b, be, nu: (be[_blk(b, nu)], 0, n))
    bspec = pl.BlockSpec((1, 1, tn), lambda n, b, be, nu: (be[_blk(b, nu)], 0, n))
    return pl.pallas_call(
        _moe_up_kernel,
        grid_spec=pltpu.PrefetchScalarGridSpec(
            num_scalar_prefetch=2,
            grid=(dff // tn, nb),
            in_specs=[pl.BlockSpec((bm, dp), lambda n, b, be, nu: (_blk(b, nu), 0)), wspec, bspec, wspec, bspec],
            out_specs=pl.BlockSpec((bm, tn), lambda n, b, be, nu: (_blk(b, nu), n)),
        ),
        out_shape=jax.ShapeDtypeStruct((n_rows, dff), BF16),
        compiler_params=_cparams(("parallel", "arbitrary")),
        name="moe_up",
    )(block_expert, n_used, xs, w_gate, b_gate3, w_up, b_up3)


def _moe_down(block_expert, n_used, act, w_down, b_down3, bm, tn):
    n_rows, dff = act.shape
    d = w_down.shape[2]
    nb = n_rows // bm
    return pl.pallas_call(
        _moe_down_kernel,
        grid_spec=pltpu.PrefetchScalarGridSpec(
            num_scalar_prefetch=2,
            grid=(d // tn, nb),
            in_specs=[
                pl.BlockSpec((bm, dff), lambda n, b, be, nu: (_blk(b, nu), 0)),
                pl.BlockSpec((1, dff, tn), lambda n, b, be, nu: (be[_blk(b, nu)], 0, n)),
                pl.BlockSpec((1, 1, tn), lambda n, b, be, nu: (be[_blk(b, nu)], 0, n)),
            ],
            out_specs=pl.BlockSpec((bm, tn // 2), lambda n, b, be, nu: (_blk(b, nu), n)),
        ),
        out_shape=jax.ShapeDtypeStruct((n_rows, d // 2), jnp.uint32),
        compiler_params=_cparams(("parallel", "arbitrary")),
        name="moe_down",
    )(block_expert, n_used, act, w_down, b_down3)


def _combine_kernel(idx_ref, idxn_ref, y_hbm, x1_ref, g_ref, nw_ref, o_ref, buf, sem, *, tm, pair_tile):
    i = pl.program_id(0)
    slot = lax.rem(i, 2)

    def start_tile(ids_ref, s):
        def body(t, carry):
            for k in range(TOP_K):
                r = ids_ref[0, k, t]
                pltpu.make_async_copy(y_hbm.at[pl.ds(r, 1), :], buf.at[s, pl.ds(k * tm + t, 1), :],
                                      sem.at[s]).start()
            return carry

        lax.fori_loop(0, tm, body, 0, unroll=2)

    @pl.when(i == 0)
    def _():
        start_tile(idx_ref, 0)

    @pl.when(i + 1 < pl.num_programs(0))
    def _():
        start_tile(idxn_ref, 1 - slot)

    pltpu.make_async_copy(y_hbm.at[pl.ds(0, TOP_K * tm), :], buf.at[slot], sem.at[slot]).wait()
    acc = x1_ref[...]
    g = g_ref[...]
    hp = pair_tile // 2
    for k in range(TOP_K):
        w = buf[slot, k * tm:(k + 1) * tm, :]
        parts = []
        for j in range(w.shape[1] // hp):
            wj = w[:, j * hp:(j + 1) * hp]
            parts.append(lax.bitcast_convert_type(wj & jnp.uint32(_HI16), F32))
            parts.append(lax.bitcast_convert_type(lax.shift_left(wj, jnp.uint32(16)), F32))
        acc = acc + g[:, k:k + 1] * jnp.concatenate(parts, axis=1)
    ms = jnp.mean(acc * acc, axis=-1, keepdims=True)
    o_ref[...] = acc * lax.rsqrt(ms + EPS) * nw_ref[...]


def _combine(dest_kt, y_rows, x1, gates, norm_w, tm, pair_tile):
    t, d = x1.shape
    nt = t // tm
    idx3 = dest_kt.reshape(TOP_K, nt, tm).transpose(1, 0, 2)
    kern = functools.partial(_combine_kernel, tm=tm, pair_tile=pair_tile)
    return pl.pallas_call(
        kern,
        grid=(nt,),
        in_specs=[
            pl.BlockSpec((1, TOP_K, tm), lambda i: (i, 0, 0), memory_space=pltpu.SMEM),
            pl.BlockSpec((1, TOP_K, tm), lambda i: (jnp.minimum(i + 1, nt - 1), 0, 0), memory_space=pltpu.SMEM),
            pl.BlockSpec(memory_space=pl.ANY),
            pl.BlockSpec((tm, d), lambda i: (i, 0)),
            pl.BlockSpec((tm, TOP_K), lambda i: (i, 0)),
            pl.BlockSpec((1, d), lambda i: (0, 0)),
        ],
        out_specs=pl.BlockSpec((tm, d), lambda i: (i, 0)),
        out_shape=jax.ShapeDtypeStruct((t, d), F32),
        scratch_shapes=[pltpu.VMEM((2, TOP_K * tm, d // 2), jnp.uint32), pltpu.SemaphoreType.DMA((2,))],
        compiler_params=_cparams(("arbitrary",)),
        name="moe_combine",
    )(idx3, idx3, y_rows, x1, gates, norm_w)


def _route_tables(ids_ranks, counts, n_experts, bm, n_blocks):
    top_idx = ids_ranks[:, :TOP_K]
    rank = ids_ranks[:, TOP_K:2 * TOP_K]
    cnt = counts[0, :n_experts]
    padded = ((cnt + bm - 1) // bm) * bm
    padded_end = jnp.cumsum(padded)
    padded_start = padded_end - padded
    hot = top_idx[:, :, None] == jnp.arange(n_experts, dtype=jnp.int32)[None, None, :]
    dest = rank + jnp.sum(jnp.where(hot, padded_start[None, None, :], 0), axis=-1)
    n_used = (padded_end[-1] // bm).astype(jnp.int32)
    block_start = jnp.arange(n_blocks, dtype=jnp.int32) * bm
    block_expert = jnp.minimum(jnp.searchsorted(padded_end, block_start, side="right"), n_experts - 1)
    zero_start = jnp.where(cnt > 0, padded_end - bm, -1).astype(jnp.int32)
    return dest.astype(jnp.int32).T, block_expert.astype(jnp.int32), n_used.reshape(1), zero_start


def _pick(n, candidates):
    for c in candidates:
        if n % c == 0:
            return c
    return n


def _layer(x, norm_mix_w, w_in, conv_ssm_w, conv_ssm_b, dt_bias_fwd, dt_bias_bwd, a_log_fwd, a_log_bwd,
           d_skip, ssm_norm_w, conf_dw_w, conf_dw_b, conf_ln_w, conf_ln_b, w_out, norm_ffn_w, w_router,
           b_router, w_gate, b_gate, w_up, b_up, w_down, b_down):
    bsz, seq, d = x.shape
    t = bsz * seq
    n_heads = dt_bias_fwd.shape[0]
    d_ssm = n_heads * HEAD_DIM
    d_xbc = conv_ssm_w.shape[1]
    d_conf = conf_dw_w.shape[1]
    n_experts = w_gate.shape[0]
    row = lambda v: v.reshape(1, -1).astype(F32)

    c_dt = d_ssm + d_xbc
    col_conf = d_ssm
    col_xbc = d_ssm + 2 * d_conf
    w_main = jnp.concatenate(
        [w_in[:, :d_ssm], w_in[:, c_dt + 2 * n_heads:], w_in[:, d_ssm:c_dt]], axis=1).astype(BF16)
    wf = w_in[:, c_dt:c_dt + n_heads]
    wb = w_in[:, c_dt + n_heads:c_dt + 2 * n_heads]
    zpad = jnp.zeros((d, LANES - 3 * n_heads), F32)
    w_dt = jnp.concatenate([wf, wf, wf, zpad, wb, wb, wb, zpad], axis=1)
    rep3 = lambda v: jnp.concatenate([v, v, v, jnp.zeros((LANES - 3 * n_heads,), F32)])
    x2d = x.reshape(t, d)

    def split2(w):
        hi = w.astype(BF16)
        return hi, (w - hi.astype(F32)).astype(BF16)

    tn = _pick(w_main.shape[1], (1024, 512, 256, 128))
    w_tiles = w_main.reshape(d, w_main.shape[1] // tn, tn).transpose(1, 0, 2)
    proj, dt2, dtt2 = _in_proj(x2d, row(norm_mix_w), w_tiles, *split2(w_dt), _pick(t, (1024, 512, 256, 128)))
    proj3 = proj.reshape(bsz, seq, -1)

    xbc = _conv_ssm(proj3, conv_ssm_w, row(conv_ssm_b), col_xbc, _pick(seq, (512, 256, 128)), 512)
    y_conf = _conformer(proj3, conf_dw_w, row(conf_dw_b), row(conf_ln_w), row(conf_ln_b),
                        col_conf, col_conf + d_conf, _pick(seq, (256, 128)))

    e3 = ((jnp.arange(LANES)[:, None] % n_heads == jnp.arange(d_ssm)[None, :] // HEAD_DIM)
          & (jnp.arange(LANES)[:, None] < 3 * n_heads)).astype(BF16)
    prm = {}
    for name, bias, alog in (("f", dt_bias_fwd, a_log_fwd), ("b", dt_bias_bwd, a_log_bwd)):
        b3 = rep3(bias.astype(F32))
        a3 = rep3(alog.astype(F32))
        prm[name] = (b3.reshape(1, LANES), jnp.broadcast_to(b3[:, None], (LANES, CHUNK)),
                     a3.reshape(1, LANES), jnp.broadcast_to(a3[:, None], (LANES, CHUNK)))
    y_f = _ssd(xbc, dt2, dtt2, *prm["f"], e3, reverse=False, n_heads=n_heads)
    dskip_x = jnp.repeat(d_skip.astype(F32), HEAD_DIM).reshape(1, d_ssm)
    y_ssm = _ssd(xbc, dt2, dtt2, *prm["b"], e3, reverse=True, n_heads=n_heads,
                 final_args=(y_f, proj3, dskip_x, row(ssm_norm_w)))

    wr = jnp.zeros((d, LANES), F32).at[:, :n_experts].set(w_router.astype(F32))
    br = jnp.zeros((1, LANES), F32).at[0, :n_experts].set(b_router.astype(F32))
    x1, logits = _out_proj(y_ssm.reshape(t, d_ssm), y_conf.reshape(t, d_conf), w_out.astype(BF16), x2d,
                           row(norm_ffn_w), *split2(wr), br, _pick(t, (512, 256, 128)))

    bm = MOE_ROWS
    n_blocks = -(-(t * TOP_K + n_experts * (bm - 1)) // bm)
    ids_ranks, gates_x, counts = _router(logits, n_experts, _pick(t, (1024, 512, 256, 128)))
    gates = gates_x[:, :TOP_K]
    dest_kt, block_expert, n_used, zero_start = _route_tables(ids_ranks, counts, n_experts, bm, n_blocks)
    xs = _dispatch(dest_kt, zero_start, x1, row(norm_ffn_w), n_blocks * bm, _pick(t, (512, 256, 128)), bm)
    act = _moe_up(block_expert, n_used, xs, w_gate, b_gate[:, None, :], w_up, b_up[:, None, :], bm, 1024)
    y_rows = _moe_down(block_expert, n_used, act, w_down, b_down[:, None, :], bm, MOE_DOWN_TILE)
    return x1, gates, dest_kt, y_rows


def kernel(x, norm_mix_w, w_in, conv_ssm_w, conv_ssm_b, dt_bias_fwd, dt_bias_bwd, a_log_fwd, a_log_bwd, d_skip,
           ssm_norm_w, conf_dw_w, conf_dw_b, conf_ln_w, conf_ln_b, w_out, norm_ffn_w, w_router, b_router, w_gate,
           b_gate, w_up, b_up, w_down, b_down, norm_final_w):
    assert w_in.shape[0] == 1, "a single layer is supported"
    bsz, seq, d = x.shape
    t = bsz * seq
    x1, gates, dest_kt, y_rows = _layer(
        x, norm_mix_w[0], w_in[0], conv_ssm_w[0], conv_ssm_b[0], dt_bias_fwd[0], dt_bias_bwd[0], a_log_fwd[0],
        a_log_bwd[0], d_skip[0], ssm_norm_w[0], conf_dw_w[0], conf_dw_b[0], conf_ln_w[0], conf_ln_b[0], w_out[0],
        norm_ffn_w[0], w_router[0], b_router[0], w_gate[0], b_gate[0], w_up[0], b_up[0], w_down[0], b_down[0])
    out = _combine(dest_kt, y_rows, x1, gates, norm_final_w.reshape(1, d).astype(F32), _pick(t, (256, 128)),
                   MOE_DOWN_TILE)
    return out.reshape(bsz, seq, d)
```

```python
import functools
import math

import jax
import jax.numpy as jnp
from jax import lax
from jax.experimental import pallas as pl
from jax.experimental.pallas import tpu as pltpu

F32 = jnp.float32
BF16 = jnp.bfloat16

EPS = 1e-5
HEAD_DIM = 64
SSM_GROUPS = 4
D_STATE = 128
CHUNK = 128
SSM_CONV = 7
CONF_KERNEL = 31
TOP_K = 4
SWIGLU_LIMIT = 7.0
SWIGLU_ALPHA = 1.702

LANES = 128
SUBLANES = 8
VMEM_LIMIT_BYTES = 56 * 1024 * 1024
HALO = 16
MOE_ROWS = 512
ROW_TILE = SUBLANES
PROJ_TILE = 1024

HIGHEST = lax.Precision.HIGHEST


def _cparams(sem):
    return pltpu.CompilerParams(dimension_semantics=sem, vmem_limit_bytes=VMEM_LIMIT_BYTES)


def _sigmoid(v):
    return 1.0 / (1.0 + jnp.exp(-v))


def _silu(v):
    return v * _sigmoid(v)


def _softplus(v):
    return jnp.maximum(v, 0.0) + jnp.log(1.0 + jnp.exp(-jnp.abs(v)))


def _split2(v):
    hi = v.astype(BF16)
    lo = (v - hi.astype(F32)).astype(BF16)
    return hi, lo


def _dot_split(a_hi, a_lo, b_hi_ref, b_lo_ref):
    b_hi = b_hi_ref[...]
    acc = jnp.dot(a_hi, b_hi, preferred_element_type=F32)
    acc = acc + jnp.dot(a_lo, b_hi, preferred_element_type=F32)
    return acc + jnp.dot(a_hi, b_lo_ref[...], preferred_element_type=F32)


def _inproj_kernel(x_ref, nw_ref, w_ref, wdt_hi_ref, wdt_lo_ref, o_ref, dt_ref, dtt_ref, h_scr):
    @pl.when(pl.program_id(1) == 0)
    def _():
        x = x_ref[...]
        ms = jnp.mean(x * x, axis=-1, keepdims=True)
        h = x * lax.rsqrt(ms + EPS) * nw_ref[...]
        h_hi, h_lo = _split2(h)
        h_scr[...] = h_hi
        dt = _dot_split(h_hi, h_lo, wdt_hi_ref, wdt_lo_ref)
        dt_ref[...] = dt
        dtt_ref[...] = dt.T

    o_ref[0] = jnp.dot(h_scr[...], w_ref[0], preferred_element_type=F32).astype(o_ref.dtype)


def _in_proj(x2d, norm_w, w_tiles, w_dt_hi, w_dt_lo, tm):
    t, d = x2d.shape
    n_tiles, _, tn = w_tiles.shape
    n_dt = w_dt_hi.shape[1]
    return pl.pallas_call(
        _inproj_kernel,
        grid=(t // tm, n_tiles),
        in_specs=[
            pl.BlockSpec((tm, d), lambda m, n: (m, 0)),
            pl.BlockSpec((1, d), lambda m, n: (0, 0)),
            pl.BlockSpec((1, d, tn), lambda m, n: (n, 0, 0)),
            pl.BlockSpec((d, n_dt), lambda m, n: (0, 0)),
            pl.BlockSpec((d, n_dt), lambda m, n: (0, 0)),
        ],
        out_specs=[
            pl.BlockSpec((1, tm, tn), lambda m, n: (n, m, 0)),
            pl.BlockSpec((tm, n_dt), lambda m, n: (m, 0)),
            pl.BlockSpec((n_dt, tm), lambda m, n: (0, m)),
        ],
        out_shape=[
            jax.ShapeDtypeStruct((n_tiles, t, tn), BF16),
            jax.ShapeDtypeStruct((t, n_dt), F32),
            jax.ShapeDtypeStruct((n_dt, t), F32),
        ],
        scratch_shapes=[pltpu.VMEM((tm, d), BF16)],
        compiler_params=_cparams(("parallel", "arbitrary")),
        name="in_proj",
    )(x2d, norm_w, w_tiles, w_dt_hi, w_dt_lo)


def _fill_halo_scratch(scr, cur, prev, nxt, lt):
    l = pl.program_id(1)
    last = pl.num_programs(1) - 1
    scr[HALO:HALO + lt, :] = cur
    scr[0:HALO, :] = jnp.where(l > 0, prev, 0.0)
    scr[HALO + lt:HALO + lt + HALO, :] = jnp.where(l < last, nxt, 0.0)


def _conv_ssm_kernel(cur_ref, prev_ref, next_ref, w_ref, b_ref, o_ref, scr, *, lt, rc):
    _fill_halo_scratch(scr, cur_ref[0, 0].astype(F32), prev_ref[0, 0].astype(F32), next_ref[0, 0].astype(F32), lt)
    pad = SSM_CONV // 2
    w = w_ref[...]
    bias = b_ref[...]

    def body(i, carry):
        r0 = pl.multiple_of(i * rc, rc)
        win = scr[pl.ds(r0 + (HALO - SUBLANES), rc + 2 * SUBLANES), :]
        acc = jnp.zeros((rc, scr.shape[1]), F32)
        for k in range(SSM_CONV):
            o = SUBLANES - pad + k
            acc = acc + w[k:k + 1, :] * win[o:o + rc, :]
        o_ref[0, pl.ds(r0, rc), :] = _silu(acc + bias).astype(o_ref.dtype)
        return carry

    lax.fori_loop(0, lt // rc, body, 0)


def _conv_ssm(proj4, conv_w, conv_b, tile0, lt, ct, rc=32):
    _, bsz, seq, tn = proj4.shape
    c = conv_w.shape[1]
    per = tn // ct
    hb = lt // HALO
    n_hb = seq // HALO
    kern = functools.partial(_conv_ssm_kernel, lt=lt, rc=rc)
    return pl.pallas_call(
        kern,
        grid=(bsz, seq // lt, c // ct),
        in_specs=[
            pl.BlockSpec((1, 1, lt, ct), lambda b, l, j: (tile0 + j // per, b, l, j % per)),
            pl.BlockSpec((1, 1, HALO, ct),
                         lambda b, l, j: (tile0 + j // per, b, jnp.maximum(l * hb - 1, 0), j % per)),
            pl.BlockSpec((1, 1, HALO, ct),
                         lambda b, l, j: (tile0 + j // per, b, jnp.minimum((l + 1) * hb, n_hb - 1), j % per)),
            pl.BlockSpec((SSM_CONV, ct), lambda b, l, j: (0, j)),
            pl.BlockSpec((1, ct), lambda b, l, j: (0, j)),
        ],
        out_specs=pl.BlockSpec((1, lt, ct), lambda b, l, j: (b, l, j)),
        out_shape=jax.ShapeDtypeStruct((bsz, seq, c), BF16),
        scratch_shapes=[pltpu.VMEM((lt + 2 * HALO, ct), F32)],
        compiler_params=_cparams(("parallel", "parallel", "parallel")),
        name="conv_ssm",
    )(proj4, proj4, proj4, conv_w, conv_b)


def _conformer_kernel(*refs, lt, rc, lc, n_t):
    tiles = refs[:6 * n_t]
    w_ref, b_ref, lnw_ref, lnb_ref, o_ref, scr, sh_scr, v_scr = refs[6 * n_t:]

    def glu(group):
        a = jnp.concatenate([r[0, 0].astype(F32) for r in tiles[2 * n_t * group:2 * n_t * group + n_t]], axis=1)
        g = jnp.concatenate([r[0, 0].astype(F32) for r in tiles[2 * n_t * group + n_t:2 * n_t * (group + 1)]], axis=1)
        return a * _sigmoid(g)

    _fill_halo_scratch(scr, glu(0), glu(1), glu(2), lt)
    c = scr.shape[1]
    pad = CONF_KERNEL // 2
    n_sh = lt + 2 * HALO - SUBLANES
    for r0 in range(0, n_sh, rc):
        n = min(rc, n_sh - r0)
        for j in range(c // lc):
            win = scr[r0:r0 + n + SUBLANES, j * lc:(j + 1) * lc]
            for m in range(1, SUBLANES):
                sh_scr[m - 1, r0:r0 + n, j * lc:(j + 1) * lc] = win[m:m + n, :]

    def conv_body(i, carry):
        r0 = pl.multiple_of(i * rc, rc)
        for j in range(c // lc):
            lanes = slice(j * lc, (j + 1) * lc)
            acc = jnp.zeros((rc, lc), F32)
            for k in range(CONF_KERNEL):
                o = HALO - pad + k
                m = o % SUBLANES
                base = o - m
                if m == 0:
                    u = scr[pl.ds(r0 + base, rc), lanes]
                else:
                    u = sh_scr[m - 1, pl.ds(r0 + base, rc), lanes]
                acc = acc + w_ref[k:k + 1, lanes] * u
            v_scr[pl.ds(r0, rc), lanes] = acc + b_ref[:, lanes]
        return carry

    lax.fori_loop(0, lt // rc, conv_body, 0)
    lnw = lnw_ref[...]
    lnb = lnb_ref[...]
    rn = 2 * SUBLANES

    def norm_body(i, carry):
        r0 = pl.multiple_of(i * rn, rn)
        v = v_scr[pl.ds(r0, rn), :]
        mu = jnp.mean(v, axis=-1, keepdims=True)
        vc = v - mu
        var = jnp.mean(vc * vc, axis=-1, keepdims=True)
        y = vc * lax.rsqrt(var + EPS) * lnw + lnb
        o_ref[0, pl.ds(r0, rn), :] = _silu(y).astype(o_ref.dtype)
        return carry

    lax.fori_loop(0, lt // rn, norm_body, 0, unroll=4)


def _conformer(proj4, dw_w, dw_b, ln_w, ln_b, tile_a, tile_g, lt, rc=64, lc=512):
    _, bsz, seq, tn = proj4.shape
    c = dw_w.shape[1]
    n_t = c // tn
    hb = lt // HALO
    n_hb = seq // HALO
    kern = functools.partial(_conformer_kernel, lt=lt, rc=rc, lc=lc, n_t=n_t)

    def cur(j):
        return pl.BlockSpec((1, 1, lt, tn), lambda b, l: (j, b, l, 0))

    def prev(j):
        return pl.BlockSpec((1, 1, HALO, tn), lambda b, l: (j, b, jnp.maximum(l * hb - 1, 0), 0))

    def nxt(j):
        return pl.BlockSpec((1, 1, HALO, tn), lambda b, l: (j, b, jnp.minimum((l + 1) * hb, n_hb - 1), 0))

    tile_specs = [mk(t0 + j) for mk in (cur, prev, nxt) for t0 in (tile_a, tile_g) for j in range(n_t)]
    vec = pl.BlockSpec((1, c), lambda b, l: (0, 0))
    return pl.pallas_call(
        kern,
        grid=(bsz, seq // lt),
        in_specs=tile_specs + [pl.BlockSpec((CONF_KERNEL, c), lambda b, l: (0, 0)), vec, vec, vec],
        out_specs=pl.BlockSpec((1, lt, c), lambda b, l: (b, l, 0)),
        out_shape=jax.ShapeDtypeStruct((bsz, seq, c), BF16),
        scratch_shapes=[pltpu.VMEM((lt + 2 * HALO, c), F32),
                        pltpu.VMEM((SUBLANES - 1, lt + 2 * HALO, c), F32),
                        pltpu.VMEM((lt, c), F32)],
        compiler_params=_cparams(("parallel", "parallel")),
        name="conformer",
    )(*([proj4] * len(tile_specs)), dw_w, dw_b, ln_w, ln_b)


def _split3(v):
    hi = v.astype(BF16).astype(F32)
    r = v - hi
    mid = r.astype(BF16).astype(F32)
    lo = (r - mid).astype(BF16).astype(F32)
    return hi, mid, lo


def _ssd_kernel(*refs, reverse, final, n_heads):
    if final:
        (xs_ref, bm_ref, cm_ref, dt_ref, dtt_ref, dtb_ref, dtbt_ref, alog_ref, alogt_ref, tri_ref, trit_ref,
         e3_ref, yf_ref) = refs[:13]
        z_refs = refs[13:-4]
        dsk_ref, nw_ref, o_ref, state_scr = refs[-4:]
    else:
        (xs_ref, bm_ref, cm_ref, dt_ref, dtt_ref, dtb_ref, dtbt_ref, alog_ref, alogt_ref, tri_ref, trit_ref,
         e3_ref, o_ref, state_scr) = refs
    q = CHUNK
    hg = n_heads
    d_ssm = n_heads * HEAD_DIM
    gw = d_ssm // SSM_GROUPS

    @pl.when(pl.program_id(1) == 0)
    def _():
        state_scr[...] = jnp.zeros_like(state_scr)

    lane = lax.broadcasted_iota(jnp.int32, (q, LANES), 1)
    sub = lax.broadcasted_iota(jnp.int32, (LANES, q), 0)

    def sel3_lanes(v):
        hi, mid, lo = _split3(v)
        return jnp.where(lane < hg, hi, jnp.where(lane < 2 * hg, mid, jnp.where(lane < 3 * hg, lo, 0.0))).astype(BF16)

    def sel3_rows(v):
        hi, mid, lo = _split3(v)
        return jnp.where(sub < hg, hi, jnp.where(sub < 2 * hg, mid, jnp.where(sub < 3 * hg, lo, 0.0))).astype(BF16)

    a_row = -jnp.exp(alog_ref[...])
    dt = _softplus(dt_ref[...] + dtb_ref[...])
    da = dt * a_row
    a_col = -jnp.exp(alogt_ref[...])
    dat = _softplus(dtt_ref[...] + dtbt_ref[...]) * a_col

    tri = tri_ref[...]
    cum3 = jnp.dot(tri, sel3_lanes(da), preferred_element_type=F32)
    cum = cum3
    for r in (1, 2, 3):
        cum = cum + pltpu.roll(cum3, r * hg, axis=1)
    cumt3 = jnp.dot(sel3_rows(dat), trit_ref[...], preferred_element_type=F32)
    cumt = cumt3[0:hg] + cumt3[hg:2 * hg] + cumt3[2 * hg:3 * hg]

    tot_row = 0 if reverse else q - 1
    total = cum[tot_row:tot_row + 1, :]

    e3 = e3_ref[...]
    lhs = jnp.concatenate(
        [sel3_lanes(dt), sel3_lanes(jnp.exp(cum)), sel3_lanes(jnp.exp(total - cum)),
         sel3_lanes(jnp.broadcast_to(jnp.exp(total), (q, LANES)))], axis=0)
    ex = jnp.dot(lhs, e3, preferred_element_type=F32)
    dt_x = ex[0:q]
    ecum_x = ex[q:2 * q]
    edte_x = ex[2 * q:3 * q]
    cdec_x = ex[3 * q:3 * q + 1]

    xs = xs_ref[0].astype(F32)
    xdt = xs * dt_x
    xdt_b = xdt.astype(BF16)
    xdte_b = (xdt * edte_x).astype(BF16)
    bm = bm_ref[0]
    cm = cm_ref[0]

    li = lax.broadcasted_iota(jnp.int32, (q, q), 0)
    si = lax.broadcasted_iota(jnp.int32, (q, q), 1)
    mask = (li <= si) if reverse else (li >= si)
    lane_lo = lax.broadcasted_iota(jnp.int32, (q, LANES), 1) < HEAD_DIM

    hpg = n_heads // SSM_GROUPS
    y_parts = []
    for g in range(SSM_GROUPS):
        bg = bm[:, g * D_STATE:(g + 1) * D_STATE]
        cg = cm[:, g * D_STATE:(g + 1) * D_STATE]
        cb = lax.dot_general(cg, bg, (((1,), (1,)), ((), ())), preferred_element_type=F32)
        st = state_scr[:, g * gw:(g + 1) * gw]
        y_off = jnp.dot(cg, st.astype(BF16), preferred_element_type=F32) * ecum_x[:, g * gw:(g + 1) * gw]
        diag = []
        for pair in range(hpg // 2):
            h0 = g * hpg + 2 * pair
            ls = []
            for h in (h0, h0 + 1):
                seg = cum[:, h:h + 1] - cumt[h:h + 1, :]
                ls.append((cb * jnp.exp(jnp.where(mask, seg, -jnp.inf))).astype(BF16))
            l2 = jnp.concatenate(ls, axis=1)
            xp = xdt_b[:, h0 * HEAD_DIM:(h0 + 2) * HEAD_DIM]
            zero = jnp.zeros_like(xp)
            r2 = jnp.concatenate([jnp.where(lane_lo, xp, zero), jnp.where(lane_lo, zero, xp)], axis=0)
            diag.append(jnp.dot(l2, r2, preferred_element_type=F32))
        y_parts.append(jnp.concatenate(diag, axis=1) + y_off)
        contrib = lax.dot_general(bg, xdte_b[:, g * gw:(g + 1) * gw], (((0,), (0,)), ((), ())),
                                  preferred_element_type=F32)
        state_scr[:, g * gw:(g + 1) * gw] = st * cdec_x[:, g * gw:(g + 1) * gw] + contrib
    y = jnp.concatenate(y_parts, axis=1)

    if not final:
        o_ref[0] = y
    else:
        y = y + yf_ref[0] + dsk_ref[...] * xs
        y = y * _silu(jnp.concatenate([r[0, 0].astype(F32) for r in z_refs], axis=1))
        outs = []
        for g in range(SSM_GROUPS):
            yg = y[:, g * gw:(g + 1) * gw]
            ms = jnp.mean(yg * yg, axis=-1, keepdims=True)
            outs.append(yg * lax.rsqrt(ms + EPS))
        o_ref[0] = (jnp.concatenate(outs, axis=1) * nw_ref[...]).astype(o_ref.dtype)


def _ssd(xbc3, dt2, dtt2, dtb, dtbt, alog, alogt, e3, *, reverse, n_heads, final_args=None):
    bsz, seq, _ = xbc3.shape
    d_ssm = n_heads * HEAD_DIM
    gn = SSM_GROUPS * D_STATE
    nc = seq // CHUNK
    q = CHUNK
    d = 1 if reverse else 0
    final = final_args is not None

    def cidx(c):
        return (nc - 1 - c) if reverse else c

    li = jnp.arange(q)[:, None]
    ji = jnp.arange(q)[None, :]
    tri = ((ji >= li) if reverse else (ji <= li)).astype(BF16)
    trit = tri.T

    const = lambda shape: pl.BlockSpec(shape, lambda b, c: (0,) * len(shape))
    in_specs = [
        pl.BlockSpec((1, q, d_ssm), lambda b, c: (b, cidx(c), 0)),
        pl.BlockSpec((1, q, gn), lambda b, c: (b, cidx(c), d_ssm // gn)),
        pl.BlockSpec((1, q, gn), lambda b, c: (b, cidx(c), d_ssm // gn + 1)),
        pl.BlockSpec((q, LANES), lambda b, c: (b * nc + cidx(c), d)),
        pl.BlockSpec((LANES, q), lambda b, c: (d, b * nc + cidx(c))),
        const((1, LANES)), const((LANES, q)), const((1, LANES)), const((LANES, q)),
        const((q, q)), const((q, q)), const((LANES, d_ssm)),
    ]
    args = [xbc3, xbc3, xbc3, dt2, dtt2, dtb, dtbt, alog, alogt, tri, trit, e3]
    if final:
        yf3, proj4, dskip_x, norm_w = final_args
        tn = proj4.shape[3]
        z_specs = [pl.BlockSpec((1, 1, q, tn), lambda b, c, j=j: (j, b, cidx(c), 0)) for j in range(d_ssm // tn)]
        in_specs += [pl.BlockSpec((1, q, d_ssm), lambda b, c: (b, cidx(c), 0))] + z_specs
        in_specs += [const((1, d_ssm)), const((1, d_ssm))]
        args += [yf3] + [proj4] * len(z_specs) + [dskip_x, norm_w]
    kern = functools.partial(_ssd_kernel, reverse=reverse, final=final, n_heads=n_heads)
    return pl.pallas_call(
        kern,
        grid=(bsz, nc),
        in_specs=in_specs,
        out_specs=pl.BlockSpec((1, q, d_ssm), lambda b, c: (b, cidx(c), 0)),
        out_shape=jax.ShapeDtypeStruct((bsz, seq, d_ssm), BF16 if final else F32),
        scratch_shapes=[pltpu.VMEM((D_STATE, d_ssm), F32)],
        compiler_params=_cparams(("parallel", "arbitrary")),
        name="ssd_bwd" if reverse else "ssd_fwd",
    )(*args)


def _outproj_kernel(ys_ref, yc_ref, wa_ref, wb_ref, x_ref, nw_ref, wr_hi_ref, wr_lo_ref, br_ref, x1_ref, lg_ref):
    acc = jnp.dot(ys_ref[...], wa_ref[...], preferred_element_type=F32)
    acc = acc + jnp.dot(yc_ref[...], wb_ref[...], preferred_element_type=F32)
    x1 = x_ref[...] + acc
    x1_ref[...] = x1
    ms = jnp.mean(x1 * x1, axis=-1, keepdims=True)
    h_hi, h_lo = _split2(x1 * lax.rsqrt(ms + EPS) * nw_ref[...])
    lg_ref[...] = _dot_split(h_hi, h_lo, wr_hi_ref, wr_lo_ref) + br_ref[...]


def _out_proj(y_ssm, y_conf, w_out, x2d, norm_w, wr_hi, wr_lo, b_router, tm):
    t, d = x2d.shape
    ka = y_ssm.shape[1]
    ne = wr_hi.shape[1]
    resident = lambda shape, idx: pl.BlockSpec(shape, lambda m: idx, pipeline_mode=pl.Buffered(1))
    return pl.pallas_call(
        _outproj_kernel,
        grid=(t // tm,),
        in_specs=[
            pl.BlockSpec((tm, ka), lambda m: (m, 0)),
            pl.BlockSpec((tm, ka), lambda m: (m, 0)),
            resident((ka, d), (0, 0)),
            resident((ka, d), (1, 0)),
            pl.BlockSpec((tm, d), lambda m: (m, 0)),
            pl.BlockSpec((1, d), lambda m: (0, 0)),
            resident((d, ne), (0, 0)),
            resident((d, ne), (0, 0)),
            pl.BlockSpec((1, ne), lambda m: (0, 0)),
        ],
        out_specs=[
            pl.BlockSpec((tm, d), lambda m: (m, 0)),
            pl.BlockSpec((tm, ne), lambda m: (m, 0)),
        ],
        out_shape=[
            jax.ShapeDtypeStruct((t, d), F32),
            jax.ShapeDtypeStruct((t, ne), F32),
        ],
        compiler_params=_cparams(("parallel",)),
        name="out_proj",
    )(y_ssm, y_conf, w_out, w_out, x2d, norm_w, wr_hi, wr_lo, b_router)


def _router_kernel(lg_ref, tri_ref, ir_ref, gate_ref, cnt_ref, run_scr, *, n_experts):
    @pl.when(pl.program_id(0) == 0)
    def _():
        run_scr[...] = jnp.zeros_like(run_scr)

    tr = lg_ref.shape[0]
    lane = lax.broadcasted_iota(jnp.int32, (tr, LANES), 1)
    lane_f = lane.astype(F32)
    lg = jnp.where(lane < n_experts, lg_ref[...], -jnp.inf)
    vals, ids, hots = [], [], []
    for _ in range(TOP_K):
        m = jnp.max(lg, axis=1, keepdims=True)
        idx = jnp.min(jnp.where(lg == m, lane_f, float(LANES)), axis=1, keepdims=True)
        hot = lane_f == idx
        vals.append(m)
        ids.append(idx)
        hots.append(hot)
        lg = jnp.where(hot, -jnp.inf, lg)
    es = [jnp.exp(v - vals[0]) for v in vals]
    den = es[0]
    for e in es[1:]:
        den = den + e
    hot_any = hots[0].astype(F32)
    for hot in hots[1:]:
        hot_any = hot_any + hot.astype(F32)
    before = jnp.dot(tri_ref[...], hot_any.astype(BF16), preferred_element_type=F32) + run_scr[...]
    run_scr[...] = run_scr[...] + jnp.sum(hot_any, axis=0, keepdims=True)
    cnt_ref[...] = run_scr[...].astype(jnp.int32)
    out_i = jnp.zeros((tr, LANES), F32)
    out_g = jnp.zeros((tr, LANES), F32)
    for k in range(TOP_K):
        rank = jnp.sum(jnp.where(hots[k], before, 0.0), axis=1, keepdims=True)
        out_i = jnp.where(lane == k, ids[k], out_i)
        out_i = jnp.where(lane == TOP_K + k, rank, out_i)
        out_g = jnp.where(lane == k, es[k] / den, out_g)
    ir_ref[...] = out_i.astype(jnp.int32)
    gate_ref[...] = out_g


def _router(logits, n_experts, tr):
    t = logits.shape[0]
    tri = (jnp.arange(tr)[None, :] < jnp.arange(tr)[:, None]).astype(BF16)
    kern = functools.partial(_router_kernel, n_experts=n_experts)
    return pl.pallas_call(
        kern,
        grid=(t // tr,),
        in_specs=[pl.BlockSpec((tr, LANES), lambda i: (i, 0)), pl.BlockSpec((tr, tr), lambda i: (0, 0))],
        out_specs=[pl.BlockSpec((tr, LANES), lambda i: (i, 0)), pl.BlockSpec((tr, LANES), lambda i: (i, 0)),
                   pl.BlockSpec((1, LANES), lambda i: (0, 0))],
        out_shape=[jax.ShapeDtypeStruct((t, LANES), jnp.int32), jax.ShapeDtypeStruct((t, LANES), F32),
                   jax.ShapeDtypeStruct((1, LANES), jnp.int32)],
        scratch_shapes=[pltpu.VMEM((1, LANES), F32)],
        compiler_params=_cparams(("arbitrary",)),
        name="moe_router",
    )(logits, tri)


_HI16 = 0xFFFF0000


def _pack_bf16_pair(a, b):
    au = lax.bitcast_convert_type(a.astype(BF16).astype(F32), jnp.uint32)
    bu = lax.bitcast_convert_type(b.astype(BF16).astype(F32), jnp.uint32)
    return (au & jnp.uint32(_HI16)) | lax.shift_right_logical(bu, jnp.uint32(16))


def _unpack_bf16_pair(w):
    a = lax.bitcast_convert_type(w & jnp.uint32(_HI16), F32).astype(BF16)
    b = lax.bitcast_convert_type(lax.shift_left(w, jnp.uint32(16)), F32).astype(BF16)
    return a, b


def _dispatch_kernel(zs_ref, idx_ref, x1_ref, nw_ref, xs_hbm, zero_scr, pk, sem, zsem, *, tm, bm, n_experts):
    i = pl.program_id(0)
    last = pl.num_programs(0) - 1
    slot = lax.rem(i, 2)
    half = x1_ref.shape[1] // 2
    rt = ROW_TILE

    def zero_copy(e):
        dst = xs_hbm.at[pl.ds(pl.multiple_of(zs_ref[e], rt * bm), rt * bm), :]
        return pltpu.make_async_copy(zero_scr, dst, zsem)

    @pl.when(i == 0)
    def _():
        zero_scr[...] = jnp.zeros_like(zero_scr)
        for e in range(n_experts):
            @pl.when(zs_ref[e] >= 0)
            def _():
                zero_copy(e).start()
        for e in range(n_experts):
            @pl.when(zs_ref[e] >= 0)
            def _():
                zero_copy(e).wait()

    def wait_rows(s):
        rows = xs_hbm.at[pl.ds(0, rt * TOP_K * tm), :]
        pltpu.make_async_copy(rows, rows, sem.at[s]).wait()

    @pl.when(i >= 2)
    def _():
        wait_rows(slot)

    x = x1_ref[...]
    ms = jnp.mean(x * x, axis=-1, keepdims=True)
    h = x * lax.rsqrt(ms + EPS) * nw_ref[...]
    packed = _pack_bf16_pair(h[:, :half], h[:, half:])
    for c in range(rt):
        pk[slot, pl.ds(c, tm, stride=rt), :] = packed[:, c * LANES:(c + 1) * LANES]

    def body(t, carry):
        src = pk.at[slot, pl.ds(pl.multiple_of(t * rt, rt), rt), :]
        for k in range(TOP_K):
            r = pl.multiple_of(idx_ref[0, k, t], rt)
            pltpu.make_async_copy(src, xs_hbm.at[pl.ds(r, rt), :], sem.at[slot]).start()
        return carry

    lax.fori_loop(0, tm, body, 0, unroll=2)

    @pl.when(i == last)
    def _():
        wait_rows(slot)

    @pl.when(jnp.logical_and(i == last, i >= 1))
    def _():
        wait_rows(1 - slot)


def _dispatch(dest_kt, zero_start, x1, norm_w, n_rows, tm, bm):
    t, d = x1.shape
    assert d // 2 == ROW_TILE * LANES
    nt = t // tm
    n_experts = zero_start.shape[0]
    idx3 = (dest_kt * ROW_TILE).reshape(TOP_K, nt, tm).transpose(1, 0, 2)
    zero_start = zero_start * ROW_TILE
    kern = functools.partial(_dispatch_kernel, tm=tm, bm=bm, n_experts=n_experts)
    return pl.pallas_call(
        kern,
        grid_spec=pltpu.PrefetchScalarGridSpec(
            num_scalar_prefetch=1,
            grid=(nt,),
            in_specs=[
                pl.BlockSpec((1, TOP_K, tm), lambda i, zs: (i, 0, 0), memory_space=pltpu.SMEM),
                pl.BlockSpec((tm, d), lambda i, zs: (i, 0)),
                pl.BlockSpec((1, d), lambda i, zs: (0, 0)),
            ],
            out_specs=pl.BlockSpec(memory_space=pl.ANY),
            scratch_shapes=[pltpu.VMEM((ROW_TILE * bm, LANES), jnp.uint32),
                            pltpu.VMEM((2, ROW_TILE * tm, LANES), jnp.uint32),
                            pltpu.SemaphoreType.DMA((2,)), pltpu.SemaphoreType.DMA(())],
        ),
        out_shape=jax.ShapeDtypeStruct((ROW_TILE * n_rows, LANES), jnp.uint32),
        compiler_params=_cparams(("arbitrary",)),
        name="moe_dispatch",
    )(zero_start, idx3, x1, norm_w)


def _moe_up_kernel(be_ref, nu_ref, x_ref, wg_ref, bg_ref, wu_ref, bu_ref, o_ref):
    @pl.when(pl.program_id(1) < nu_ref[0])
    def _():
        bm = x_ref.shape[0] // ROW_TILE
        words = jnp.concatenate([x_ref[pl.ds(c, bm, stride=ROW_TILE), :] for c in range(ROW_TILE)], axis=1)
        xa, xb = _unpack_bf16_pair(words)
        half = xa.shape[1]
        wg = wg_ref[0].astype(BF16)
        wu = wu_ref[0].astype(BF16)
        gate = (jnp.dot(xa, wg[:half], preferred_element_type=F32)
                + jnp.dot(xb, wg[half:], preferred_element_type=F32) + bg_ref[0])
        up = (jnp.dot(xa, wu[:half], preferred_element_type=F32)
              + jnp.dot(xb, wu[half:], preferred_element_type=F32) + bu_ref[0])
        gate = jnp.minimum(gate, SWIGLU_LIMIT)
        up = jnp.clip(up, -SWIGLU_LIMIT, SWIGLU_LIMIT)
        act = (up + 1.0) * gate * _sigmoid(SWIGLU_ALPHA * gate)
        o_ref[...] = act.astype(o_ref.dtype)


def _moe_down_kernel(be_ref, nu_ref, a_ref, wd_ref, bd_ref, o_ref):
    @pl.when(pl.program_id(0) < nu_ref[0])
    def _():
        y = jnp.dot(a_ref[...], wd_ref[0].astype(BF16), preferred_element_type=F32) + bd_ref[0]
        bm, d = y.shape
        packed = _pack_bf16_pair(y[:, :d // 2], y[:, d // 2:])
        for c in range(ROW_TILE):
            o_ref[pl.ds(c, bm, stride=ROW_TILE), :] = packed[:, c * LANES:(c + 1) * LANES]


def _blk(b, nu_ref):
    return jnp.minimum(b, nu_ref[0] - 1)


def _moe_up(block_expert, n_used, xs, w_gate, b_gate3, w_up, b_up3, bm, tn):
    n_rows = xs.shape[0] // ROW_TILE
    d, dff = w_gate.shape[1], w_gate.shape[2]
    nb = n_rows // bm
    wspec = pl.BlockSpec((1, d, tn), lambda n, b, be, nu: (be[_blk(b, nu)], 0, n))
    bspec = pl.BlockSpec((1, 1, tn), lambda n, b, be, nu: (be[_blk(b, nu)], 0, n))
    return pl.pallas_call(
        _moe_up_kernel,
        grid_spec=pltpu.PrefetchScalarGridSpec(
            num_scalar_prefetch=2,
            grid=(dff // tn, nb),
            in_specs=[pl.BlockSpec((ROW_TILE * bm, LANES), lambda n, b, be, nu: (_blk(b, nu), 0)),
                      wspec, bspec, wspec, bspec],
            out_specs=pl.BlockSpec((bm, tn), lambda n, b, be, nu: (_blk(b, nu), n)),
        ),
        out_shape=jax.ShapeDtypeStruct((n_rows, dff), BF16),
        compiler_params=_cparams(("parallel", "arbitrary")),
        name="moe_up",
    )(block_expert, n_used, xs, w_gate, b_gate3, w_up, b_up3)


def _moe_down(block_expert, n_used, act, w_down, b_down3, bm):
    n_rows, dff = act.shape
    d = w_down.shape[2]
    assert d // 2 == ROW_TILE * LANES
    nb = n_rows // bm
    return pl.pallas_call(
        _moe_down_kernel,
        grid_spec=pltpu.PrefetchScalarGridSpec(
            num_scalar_prefetch=2,
            grid=(nb,),
            in_specs=[
                pl.BlockSpec((bm, dff), lambda b, be, nu: (_blk(b, nu), 0)),
                pl.BlockSpec((1, dff, d), lambda b, be, nu: (be[_blk(b, nu)], 0, 0)),
                pl.BlockSpec((1, 1, d), lambda b, be, nu: (be[_blk(b, nu)], 0, 0)),
            ],
            out_specs=pl.BlockSpec((ROW_TILE * bm, LANES), lambda b, be, nu: (_blk(b, nu), 0)),
        ),
        out_shape=jax.ShapeDtypeStruct((ROW_TILE * n_rows, LANES), jnp.uint32),
        compiler_params=_cparams(("arbitrary",)),
        name="moe_down",
    )(block_expert, n_used, act, w_down, b_down3)


def _combine_kernel(idx_ref, idxn_ref, y_hbm, x1_ref, g_ref, nw_ref, o_ref, buf, sem, *, tm):
    i = pl.program_id(0)
    slot = lax.rem(i, 2)
    rt = ROW_TILE

    def start_tile(ids_ref, s):
        def body(t, carry):
            for k in range(TOP_K):
                r = pl.multiple_of(ids_ref[0, k, t], rt)
                dst = buf.at[s, pl.ds(pl.multiple_of((k * tm + t) * rt, rt), rt), :]
                pltpu.make_async_copy(y_hbm.at[pl.ds(r, rt), :], dst, sem.at[s]).start()
            return carry

        lax.fori_loop(0, tm, body, 0, unroll=2)

    @pl.when(i == 0)
    def _():
        start_tile(idx_ref, 0)

    @pl.when(i + 1 < pl.num_programs(0))
    def _():
        start_tile(idxn_ref, 1 - slot)

    pltpu.make_async_copy(y_hbm.at[pl.ds(0, rt * TOP_K * tm), :], buf.at[slot], sem.at[slot]).wait()
    acc = x1_ref[...]
    g = g_ref[...]
    for k in range(TOP_K):
        words = jnp.concatenate(
            [buf[slot, pl.ds(rt * k * tm + c, tm, stride=rt), :] for c in range(rt)], axis=1)
        y = jnp.concatenate([lax.bitcast_convert_type(words & jnp.uint32(_HI16), F32),
                             lax.bitcast_convert_type(lax.shift_left(words, jnp.uint32(16)), F32)], axis=1)
        acc = acc + g[:, k:k + 1] * y
    ms = jnp.mean(acc * acc, axis=-1, keepdims=True)
    o_ref[...] = acc * lax.rsqrt(ms + EPS) * nw_ref[...]


def _combine(dest_kt, y_rows, x1, gates, norm_w, tm):
    t, d = x1.shape
    nt = t // tm
    idx3 = (dest_kt * ROW_TILE).reshape(TOP_K, nt, tm).transpose(1, 0, 2)
    kern = functools.partial(_combine_kernel, tm=tm)
    return pl.pallas_call(
        kern,
        grid=(nt,),
        in_specs=[
            pl.BlockSpec((1, TOP_K, tm), lambda i: (i, 0, 0), memory_space=pltpu.SMEM),
            pl.BlockSpec((1, TOP_K, tm), lambda i: (jnp.minimum(i + 1, nt - 1), 0, 0), memory_space=pltpu.SMEM),
            pl.BlockSpec(memory_space=pl.ANY),
            pl.BlockSpec((tm, d), lambda i: (i, 0)),
            pl.BlockSpec((tm, TOP_K), lambda i: (i, 0)),
            pl.BlockSpec((1, d), lambda i: (0, 0)),
        ],
        out_specs=pl.BlockSpec((tm, d), lambda i: (i, 0)),
        out_shape=jax.ShapeDtypeStruct((t, d), F32),
        scratch_shapes=[pltpu.VMEM((2, ROW_TILE * TOP_K * tm, LANES), jnp.uint32), pltpu.SemaphoreType.DMA((2,))],
        compiler_params=_cparams(("arbitrary",)),
        name="moe_combine",
    )(idx3, idx3, y_rows, x1, gates, norm_w)


def _route_tables(ids_ranks, counts, n_experts, bm, n_blocks):
    top_idx = ids_ranks[:, :TOP_K]
    rank = ids_ranks[:, TOP_K:2 * TOP_K]
    cnt = counts[0, :n_experts]
    padded = ((cnt + bm - 1) // bm) * bm
    padded_end = jnp.cumsum(padded)
    padded_start = padded_end - padded
    hot = top_idx[:, :, None] == jnp.arange(n_experts, dtype=jnp.int32)[None, None, :]
    dest = rank + jnp.sum(jnp.where(hot, padded_start[None, None, :], 0), axis=-1)
    n_used = (padded_end[-1] // bm).astype(jnp.int32)
    block_start = jnp.arange(n_blocks, dtype=jnp.int32) * bm
    block_expert = jnp.minimum(jnp.sum(padded_end[None, :] <= block_start[:, None], axis=1), n_experts - 1)
    zero_start = jnp.where(cnt > 0, padded_end - bm, -1).astype(jnp.int32)
    return dest.astype(jnp.int32).T, block_expert.astype(jnp.int32), n_used.reshape(1), zero_start


def _pick(n, candidates):
    for c in candidates:
        if n % c == 0:
            return c
    return n


def _layer(x, norm_mix_w, w_in, conv_ssm_w, conv_ssm_b, dt_bias_fwd, dt_bias_bwd, a_log_fwd, a_log_bwd,
           d_skip, ssm_norm_w, conf_dw_w, conf_dw_b, conf_ln_w, conf_ln_b, w_out, norm_ffn_w, w_router,
           b_router, w_gate, b_gate, w_up, b_up, w_down, b_down):
    bsz, seq, d = x.shape
    t = bsz * seq
    n_heads = dt_bias_fwd.shape[0]
    d_ssm = n_heads * HEAD_DIM
    d_xbc = conv_ssm_w.shape[1]
    d_conf = conf_dw_w.shape[1]
    n_experts = w_gate.shape[0]
    row = lambda v: v.reshape(1, -1).astype(F32)

    c_dt = d_ssm + d_xbc
    w_main = jnp.concatenate(
        [w_in[:, :d_ssm], w_in[:, c_dt + 2 * n_heads:], w_in[:, d_ssm:c_dt]], axis=1).astype(BF16)
    wf = w_in[:, c_dt:c_dt + n_heads]
    wb = w_in[:, c_dt + n_heads:c_dt + 2 * n_heads]
    zpad = jnp.zeros((d, LANES - 3 * n_heads), F32)
    w_dt = jnp.concatenate([wf, wf, wf, zpad, wb, wb, wb, zpad], axis=1)
    rep3 = lambda v: jnp.concatenate([v, v, v, jnp.zeros((LANES - 3 * n_heads,), F32)])
    x2d = x.reshape(t, d)

    def split2(w):
        hi = w.astype(BF16)
        return hi, (w - hi.astype(F32)).astype(BF16)

    tn = PROJ_TILE
    w_tiles = w_main.reshape(d, w_main.shape[1] // tn, tn).transpose(1, 0, 2)
    proj, dt2, dtt2 = _in_proj(x2d, row(norm_mix_w), w_tiles, *split2(w_dt), _pick(t, (1024, 512, 256, 128)))
    proj4 = proj.reshape(-1, bsz, seq, tn)
    tile_conf = d_ssm // tn
    tile_xbc = (d_ssm + 2 * d_conf) // tn

    xbc = _conv_ssm(proj4, conv_ssm_w, row(conv_ssm_b), tile_xbc, _pick(seq, (512, 256, 128)), 512)
    y_conf = _conformer(proj4, conf_dw_w, row(conf_dw_b), row(conf_ln_w), row(conf_ln_b),
                        tile_conf, tile_conf + d_conf // tn, _pick(seq, (256, 128)))

    e3 = ((jnp.arange(LANES)[:, None] % n_heads == jnp.arange(d_ssm)[None, :] // HEAD_DIM)
          & (jnp.arange(LANES)[:, None] < 3 * n_heads)).astype(BF16)
    prm = {}
    for name, bias, alog in (("f", dt_bias_fwd, a_log_fwd), ("b", dt_bias_bwd, a_log_bwd)):
        b3 = rep3(bias.astype(F32))
        a3 = rep3(alog.astype(F32))
        prm[name] = (b3.reshape(1, LANES), jnp.broadcast_to(b3[:, None], (LANES, CHUNK)),
                     a3.reshape(1, LANES), jnp.broadcast_to(a3[:, None], (LANES, CHUNK)))
    y_f = _ssd(xbc, dt2, dtt2, *prm["f"], e3, reverse=False, n_heads=n_heads)
    dskip_x = jnp.repeat(d_skip.astype(F32), HEAD_DIM).reshape(1, d_ssm)
    y_ssm = _ssd(xbc, dt2, dtt2, *prm["b"], e3, reverse=True, n_heads=n_heads,
                 final_args=(y_f, proj4, dskip_x, row(ssm_norm_w)))

    wr = jnp.zeros((d, LANES), F32).at[:, :n_experts].set(w_router.astype(F32))
    br = jnp.zeros((1, LANES), F32).at[0, :n_experts].set(b_router.astype(F32))
    x1, logits = _out_proj(y_ssm.reshape(t, d_ssm), y_conf.reshape(t, d_conf), w_out.astype(BF16), x2d,
                           row(norm_ffn_w), *split2(wr), br, _pick(t, (512, 256, 128)))

    bm = MOE_ROWS
    n_blocks = -(-(t * TOP_K + n_experts * (bm - 1)) // bm)
    ids_ranks, gates_x, counts = _router(logits, n_experts, _pick(t, (1024, 512, 256, 128)))
    gates = gates_x[:, :TOP_K]
    dest_kt, block_expert, n_used, zero_start = _route_tables(ids_ranks, counts, n_experts, bm, n_blocks)
    xs = _dispatch(dest_kt, zero_start, x1, row(norm_ffn_w), n_blocks * bm, _pick(t, (512, 256, 128)), bm)
    act = _moe_up(block_expert, n_used, xs, w_gate, b_gate[:, None, :], w_up, b_up[:, None, :], bm, 1024)
    y_rows = _moe_down(block_expert, n_used, act, w_down, b_down[:, None, :], bm)
    return x1, gates, dest_kt, y_rows


def kernel(x, norm_mix_w, w_in, conv_ssm_w, conv_ssm_b, dt_bias_fwd, dt_bias_bwd, a_log_fwd, a_log_bwd, d_skip,
           ssm_norm_w, conf_dw_w, conf_dw_b, conf_ln_w, conf_ln_b, w_out, norm_ffn_w, w_router, b_router, w_gate,
           b_gate, w_up, b_up, w_down, b_down, norm_final_w):
    assert w_in.shape[0] == 1, "a single layer is supported"
    bsz, seq, d = x.shape
    t = bsz * seq
    x1, gates, dest_kt, y_rows = _layer(
        x, norm_mix_w[0], w_in[0], conv_ssm_w[0], conv_ssm_b[0], dt_bias_fwd[0], dt_bias_bwd[0], a_log_fwd[0],
        a_log_bwd[0], d_skip[0], ssm_norm_w[0], conf_dw_w[0], conf_dw_b[0], conf_ln_w[0], conf_ln_b[0], w_out[0],
        norm_ffn_w[0], w_router[0], b_router[0], w_gate[0], b_gate[0], w_up[0], b_up[0], w_down[0], b_down[0])
    out = _combine(dest_kt, y_rows, x1, gates, norm_final_w.reshape(1, d).astype(F32), _pick(t, (256, 128)))
    return out.reshape(bsz, seq, d)
```

```python
import functools
import math

import jax
import jax.numpy as jnp
from jax import lax
from jax.experimental import pallas as pl
from jax.experimental.pallas import tpu as pltpu

F32 = jnp.float32
BF16 = jnp.bfloat16

EPS = 1e-5
HEAD_DIM = 64
SSM_GROUPS = 4
D_STATE = 128
CHUNK = 128
SSM_CONV = 7
CONF_KERNEL = 31
TOP_K = 4
SWIGLU_LIMIT = 7.0
SWIGLU_ALPHA = 1.702

LANES = 128
SUBLANES = 8
VMEM_LIMIT_BYTES = 56 * 1024 * 1024
HALO = 16
MOE_ROWS = 512
ROW_TILE = SUBLANES
PROJ_TILE = 1024

HIGHEST = lax.Precision.HIGHEST


def _cparams(sem):
    return pltpu.CompilerParams(dimension_semantics=sem, vmem_limit_bytes=VMEM_LIMIT_BYTES)


def _sigmoid(v):
    return 1.0 / (1.0 + jnp.exp(-v))


def _silu(v):
    return v * _sigmoid(v)


def _softplus(v):
    return jnp.maximum(v, 0.0) + jnp.log(1.0 + jnp.exp(-jnp.abs(v)))


def _split2(v):
    hi = v.astype(BF16)
    lo = (v - hi.astype(F32)).astype(BF16)
    return hi, lo


def _dot_split(a_hi, a_lo, b_hi_ref, b_lo_ref):
    b_hi = b_hi_ref[...]
    acc = jnp.dot(a_hi, b_hi, preferred_element_type=F32)
    acc = acc + jnp.dot(a_lo, b_hi, preferred_element_type=F32)
    return acc + jnp.dot(a_hi, b_lo_ref[...], preferred_element_type=F32)


def _norm_dt_kernel(x_ref, nw_ref, wdt_hi_ref, wdt_lo_ref, h_ref, dt_ref, dtt_ref):
    x = x_ref[...]
    ms = jnp.mean(x * x, axis=-1, keepdims=True)
    h_hi, h_lo = _split2(x * lax.rsqrt(ms + EPS) * nw_ref[...])
    h_ref[...] = h_hi
    dt = _dot_split(h_hi, h_lo, wdt_hi_ref, wdt_lo_ref)
    dt_ref[...] = dt
    dtt_ref[...] = dt.T


def _norm_dt(x2d, norm_w, w_dt_hi, w_dt_lo, tm):
    t, d = x2d.shape
    n_dt = w_dt_hi.shape[1]
    return pl.pallas_call(
        _norm_dt_kernel,
        grid=(t // tm,),
        in_specs=[
            pl.BlockSpec((tm, d), lambda m: (m, 0)),
            pl.BlockSpec((1, d), lambda m: (0, 0)),
            pl.BlockSpec((d, n_dt), lambda m: (0, 0)),
            pl.BlockSpec((d, n_dt), lambda m: (0, 0)),
        ],
        out_specs=[
            pl.BlockSpec((tm, d), lambda m: (m, 0)),
            pl.BlockSpec((tm, n_dt), lambda m: (m, 0)),
            pl.BlockSpec((n_dt, tm), lambda m: (0, m)),
        ],
        out_shape=[
            jax.ShapeDtypeStruct((t, d), BF16),
            jax.ShapeDtypeStruct((t, n_dt), F32),
            jax.ShapeDtypeStruct((n_dt, t), F32),
        ],
        compiler_params=_cparams(("parallel",)),
        name="norm_dt",
    )(x2d, norm_w, w_dt_hi, w_dt_lo)


def _inproj_kernel(h_ref, w_ref, o_ref):
    o_ref[0] = jnp.dot(h_ref[...], w_ref[0], preferred_element_type=F32).astype(o_ref.dtype)


def _in_proj(h, w_tiles, tm):
    t, d = h.shape
    n_tiles, _, tn = w_tiles.shape
    return pl.pallas_call(
        _inproj_kernel,
        grid=(t // tm, n_tiles),
        in_specs=[
            pl.BlockSpec((tm, d), lambda m, n: (m, 0)),
            pl.BlockSpec((1, d, tn), lambda m, n: (n, 0, 0)),
        ],
        out_specs=pl.BlockSpec((1, tm, tn), lambda m, n: (n, m, 0)),
        out_shape=jax.ShapeDtypeStruct((n_tiles, t, tn), BF16),
        compiler_params=_cparams(("parallel", "arbitrary")),
        name="in_proj",
    )(h, w_tiles)


def _fill_halo_scratch(scr, cur, prev, nxt, lt):
    l = pl.program_id(1)
    last = pl.num_programs(1) - 1
    scr[HALO:HALO + lt, :] = cur
    scr[0:HALO, :] = jnp.where(l > 0, prev, 0.0)
    scr[HALO + lt:HALO + lt + HALO, :] = jnp.where(l < last, nxt, 0.0)


def _conv_ssm_kernel(cur_ref, prev_ref, next_ref, w_ref, b_ref, o_ref, scr, *, lt, rc):
    _fill_halo_scratch(scr, cur_ref[0, 0].astype(F32), prev_ref[0, 0].astype(F32), next_ref[0, 0].astype(F32), lt)
    pad = SSM_CONV // 2
    w = w_ref[...]
    bias = b_ref[...]

    def body(i, carry):
        r0 = pl.multiple_of(i * rc, rc)
        win = scr[pl.ds(r0 + (HALO - SUBLANES), rc + 2 * SUBLANES), :]
        acc = jnp.zeros((rc, scr.shape[1]), F32)
        for k in range(SSM_CONV):
            o = SUBLANES - pad + k
            acc = acc + w[k:k + 1, :] * win[o:o + rc, :]
        o_ref[0, pl.ds(r0, rc), :] = _silu(acc + bias).astype(o_ref.dtype)
        return carry

    lax.fori_loop(0, lt // rc, body, 0)


def _conv_ssm(proj4, conv_w, conv_b, tile0, lt, ct, rc=32):
    _, bsz, seq, tn = proj4.shape
    c = conv_w.shape[1]
    per = tn // ct
    hb = lt // HALO
    n_hb = seq // HALO
    kern = functools.partial(_conv_ssm_kernel, lt=lt, rc=rc)
    return pl.pallas_call(
        kern,
        grid=(bsz, seq // lt, c // ct),
        in_specs=[
            pl.BlockSpec((1, 1, lt, ct), lambda b, l, j: (tile0 + j // per, b, l, j % per)),
            pl.BlockSpec((1, 1, HALO, ct),
                         lambda b, l, j: (tile0 + j // per, b, jnp.maximum(l * hb - 1, 0), j % per)),
            pl.BlockSpec((1, 1, HALO, ct),
                         lambda b, l, j: (tile0 + j // per, b, jnp.minimum((l + 1) * hb, n_hb - 1), j % per)),
            pl.BlockSpec((SSM_CONV, ct), lambda b, l, j: (0, j)),
            pl.BlockSpec((1, ct), lambda b, l, j: (0, j)),
        ],
        out_specs=pl.BlockSpec((1, lt, ct), lambda b, l, j: (b, l, j)),
        out_shape=jax.ShapeDtypeStruct((bsz, seq, c), BF16),
        scratch_shapes=[pltpu.VMEM((lt + 2 * HALO, ct), F32)],
        compiler_params=_cparams(("parallel", "parallel", "parallel")),
        name="conv_ssm",
    )(proj4, proj4, proj4, conv_w, conv_b)


def _conformer_kernel(*refs, lt, rc, lc, n_t):
    tiles = refs[:6 * n_t]
    w_ref, b_ref, lnw_ref, lnb_ref, o_ref, scr, sh_scr, v_scr = refs[6 * n_t:]

    def glu(group):
        a = jnp.concatenate([r[0, 0].astype(F32) for r in tiles[2 * n_t * group:2 * n_t * group + n_t]], axis=1)
        g = jnp.concatenate([r[0, 0].astype(F32) for r in tiles[2 * n_t * group + n_t:2 * n_t * (group + 1)]], axis=1)
        return a * _sigmoid(g)

    _fill_halo_scratch(scr, glu(0), glu(1), glu(2), lt)
    c = scr.shape[1]
    pad = CONF_KERNEL // 2
    n_sh = lt + 2 * HALO - SUBLANES
    for r0 in range(0, n_sh, rc):
        n = min(rc, n_sh - r0)
        for j in range(c // lc):
            win = scr[r0:r0 + n + SUBLANES, j * lc:(j + 1) * lc]
            for m in range(1, SUBLANES):
                sh_scr[m - 1, r0:r0 + n, j * lc:(j + 1) * lc] = win[m:m + n, :]

    def conv_body(i, carry):
        r0 = pl.multiple_of(i * rc, rc)
        for j in range(c // lc):
            lanes = slice(j * lc, (j + 1) * lc)
            acc = jnp.zeros((rc, lc), F32)
            for m in range(SUBLANES):
                offs = [(k, HALO - pad + k - m) for k in range(CONF_KERNEL) if (HALO - pad + k) % SUBLANES == m]
                if not offs:
                    continue
                span = rc + offs[-1][1]
                if m == 0:
                    win = scr[pl.ds(r0, span), lanes]
                else:
                    win = sh_scr[m - 1, pl.ds(r0, span), lanes]
                for k, base in offs:
                    acc = acc + w_ref[k:k + 1, lanes] * win[base:base + rc, :]
            v_scr[pl.ds(r0, rc), lanes] = acc + b_ref[:, lanes]
        return carry

    lax.fori_loop(0, lt // rc, conv_body, 0)
    lnw = lnw_ref[...]
    lnb = lnb_ref[...]
    rn = 2 * SUBLANES

    def norm_body(i, carry):
        r0 = pl.multiple_of(i * rn, rn)
        v = v_scr[pl.ds(r0, rn), :]
        mu = jnp.mean(v, axis=-1, keepdims=True)
        vc = v - mu
        var = jnp.mean(vc * vc, axis=-1, keepdims=True)
        y = vc * lax.rsqrt(var + EPS) * lnw + lnb
        o_ref[0, pl.ds(r0, rn), :] = _silu(y).astype(o_ref.dtype)
        return carry

    lax.fori_loop(0, lt // rn, norm_body, 0, unroll=4)


def _conformer(proj4, dw_w, dw_b, ln_w, ln_b, tile_a, tile_g, lt, rc=64, lc=128):
    _, bsz, seq, tn = proj4.shape
    c = dw_w.shape[1]
    n_t = c // tn
    hb = lt // HALO
    n_hb = seq // HALO
    kern = functools.partial(_conformer_kernel, lt=lt, rc=rc, lc=lc, n_t=n_t)

    def cur(j):
        return pl.BlockSpec((1, 1, lt, tn), lambda b, l: (j, b, l, 0))

    def prev(j):
        return pl.BlockSpec((1, 1, HALO, tn), lambda b, l: (j, b, jnp.maximum(l * hb - 1, 0), 0))

    def nxt(j):
        return pl.BlockSpec((1, 1, HALO, tn), lambda b, l: (j, b, jnp.minimum((l + 1) * hb, n_hb - 1), 0))

    tile_specs = [mk(t0 + j) for mk in (cur, prev, nxt) for t0 in (tile_a, tile_g) for j in range(n_t)]
    vec = pl.BlockSpec((1, c), lambda b, l: (0, 0))
    return pl.pallas_call(
        kern,
        grid=(bsz, seq // lt),
        in_specs=tile_specs + [pl.BlockSpec((CONF_KERNEL, c), lambda b, l: (0, 0)), vec, vec, vec],
        out_specs=pl.BlockSpec((1, lt, c), lambda b, l: (b, l, 0)),
        out_shape=jax.ShapeDtypeStruct((bsz, seq, c), BF16),
        scratch_shapes=[pltpu.VMEM((lt + 2 * HALO, c), F32),
                        pltpu.VMEM((SUBLANES - 1, lt + 2 * HALO, c), F32),
                        pltpu.VMEM((lt, c), F32)],
        compiler_params=_cparams(("parallel", "parallel")),
        name="conformer",
    )(*([proj4] * len(tile_specs)), dw_w, dw_b, ln_w, ln_b)


def _split3(v):
    hi = v.astype(BF16).astype(F32)
    r = v - hi
    mid = r.astype(BF16).astype(F32)
    lo = (r - mid).astype(BF16).astype(F32)
    return hi, mid, lo


def _ssd_kernel(*refs, reverse, final, n_heads, nbs):
    xs_ref, bm_ref, cm_ref, dt_ref = refs[:4]
    dtt_refs = refs[4:4 + nbs]
    consts = refs[4 + nbs:11 + nbs]
    if final:
        yf_ref = refs[11 + nbs]
        z_refs = refs[12 + nbs:-4]
        dsk_ref, nw_ref, o_ref, state_scr = refs[-4:]
    else:
        yf_ref, z_refs, dsk_ref, nw_ref = None, (), None, None
        o_ref, state_scr = refs[-2:]

    @pl.when(pl.program_id(1) == 0)
    def _():
        state_scr[...] = jnp.zeros_like(state_scr)

    for s in range(nbs):
        _ssd_chunk(s, xs_ref, bm_ref, cm_ref, dt_ref, dtt_refs[s], consts, yf_ref, z_refs, dsk_ref, nw_ref, o_ref,
                   state_scr, reverse=reverse, final=final, n_heads=n_heads)


def _ssd_chunk(s, xs_ref, bm_ref, cm_ref, dt_ref, dtt_ref, consts, yf_ref, z_refs, dsk_ref, nw_ref, o_ref,
               state_scr, *, reverse, final, n_heads):
    dtb_ref, dtbt_ref, alog_ref, alogt_ref, tri_ref, trit_ref, e3_ref = consts
    q = CHUNK
    hg = n_heads
    d_ssm = n_heads * HEAD_DIM
    gw = d_ssm // SSM_GROUPS

    lane = lax.broadcasted_iota(jnp.int32, (q, LANES), 1)
    sub = lax.broadcasted_iota(jnp.int32, (LANES, q), 0)

    def sel3_lanes(v):
        hi, mid, lo = _split3(v)
        return jnp.where(lane < hg, hi, jnp.where(lane < 2 * hg, mid, jnp.where(lane < 3 * hg, lo, 0.0))).astype(BF16)

    def sel3_rows(v):
        hi, mid, lo = _split3(v)
        return jnp.where(sub < hg, hi, jnp.where(sub < 2 * hg, mid, jnp.where(sub < 3 * hg, lo, 0.0))).astype(BF16)

    a_row = -jnp.exp(alog_ref[...])
    dt = _softplus(dt_ref[s] + dtb_ref[...])
    da = dt * a_row
    a_col = -jnp.exp(alogt_ref[...])
    dat = _softplus(dtt_ref[...] + dtbt_ref[...]) * a_col

    tri = tri_ref[...]
    cum3 = jnp.dot(tri, sel3_lanes(da), preferred_element_type=F32)
    cum = cum3
    for r in (1, 2, 3):
        cum = cum + pltpu.roll(cum3, r * hg, axis=1)
    cumt3 = jnp.dot(sel3_rows(dat), trit_ref[...], preferred_element_type=F32)
    cumt = cumt3[0:hg] + cumt3[hg:2 * hg] + cumt3[2 * hg:3 * hg]

    tot_row = 0 if reverse else q - 1
    total = cum[tot_row:tot_row + 1, :]

    e3 = e3_ref[...]
    lhs = jnp.concatenate(
        [sel3_lanes(dt), sel3_lanes(jnp.exp(cum)), sel3_lanes(jnp.exp(total - cum)),
         sel3_lanes(jnp.broadcast_to(jnp.exp(total), (q, LANES)))], axis=0)
    ex = jnp.dot(lhs, e3, preferred_element_type=F32)
    dt_x = ex[0:q]
    ecum_x = ex[q:2 * q]
    edte_x = ex[2 * q:3 * q]
    cdec_x = ex[3 * q:3 * q + 1]

    xs = xs_ref[s].astype(F32)
    xdt = xs * dt_x
    xdt_b = xdt.astype(BF16)
    xdte_b = (xdt * edte_x).astype(BF16)
    bm = bm_ref[s]
    cm = cm_ref[s]

    li = lax.broadcasted_iota(jnp.int32, (q, q), 0)
    si = lax.broadcasted_iota(jnp.int32, (q, q), 1)
    mask = (li <= si) if reverse else (li >= si)
    lane_lo = lax.broadcasted_iota(jnp.int32, (q, LANES), 1) < HEAD_DIM

    hpg = n_heads // SSM_GROUPS
    y_parts = []
    for g in range(SSM_GROUPS):
        bg = bm[:, g * D_STATE:(g + 1) * D_STATE]
        cg = cm[:, g * D_STATE:(g + 1) * D_STATE]
        cb = lax.dot_general(cg, bg, (((1,), (1,)), ((), ())), preferred_element_type=F32)
        st = state_scr[s, :, g * gw:(g + 1) * gw]
        y_off = jnp.dot(cg, st.astype(BF16), preferred_element_type=F32) * ecum_x[:, g * gw:(g + 1) * gw]
        diag = []
        for pair in range(hpg // 2):
            h0 = g * hpg + 2 * pair
            ls = []
            for h in (h0, h0 + 1):
                seg = cum[:, h:h + 1] - cumt[h:h + 1, :]
                ls.append((cb * jnp.exp(jnp.where(mask, seg, -jnp.inf))).astype(BF16))
            l2 = jnp.concatenate(ls, axis=1)
            xp = xdt_b[:, h0 * HEAD_DIM:(h0 + 2) * HEAD_DIM]
            zero = jnp.zeros_like(xp)
            r2 = jnp.concatenate([jnp.where(lane_lo, xp, zero), jnp.where(lane_lo, zero, xp)], axis=0)
            diag.append(jnp.dot(l2, r2, preferred_element_type=F32))
        y_parts.append(jnp.concatenate(diag, axis=1) + y_off)
        contrib = lax.dot_general(bg, xdte_b[:, g * gw:(g + 1) * gw], (((0,), (0,)), ((), ())),
                                  preferred_element_type=F32)
        state_scr[s, :, g * gw:(g + 1) * gw] = st * cdec_x[:, g * gw:(g + 1) * gw] + contrib
    y = jnp.concatenate(y_parts, axis=1)

    if not final:
        o_ref[s] = y
    else:
        y = y + yf_ref[s] + dsk_ref[...] * xs
        y = y * _silu(jnp.concatenate([r[0, s].astype(F32) for r in z_refs], axis=1))
        outs = []
        for g in range(SSM_GROUPS):
            yg = y[:, g * gw:(g + 1) * gw]
            ms = jnp.mean(yg * yg, axis=-1, keepdims=True)
            outs.append(yg * lax.rsqrt(ms + EPS))
        o_ref[s] = (jnp.concatenate(outs, axis=1) * nw_ref[...]).astype(o_ref.dtype)


def _ssd(xbc3, dt2, dtt2, dtb, dtbt, alog, alogt, e3, *, reverse, n_heads, final_args=None):
    bsz, seq, _ = xbc3.shape
    d_ssm = n_heads * HEAD_DIM
    gn = SSM_GROUPS * D_STATE
    nc = seq // CHUNK
    q = CHUNK
    d = 1 if reverse else 0
    final = final_args is not None

    def cidx(c):
        return (nc - 1 - c) if reverse else c

    li = jnp.arange(q)[:, None]
    ji = jnp.arange(q)[None, :]
    tri = ((ji >= li) if reverse else (ji <= li)).astype(BF16)
    trit = tri.T

    nbs = 2 if bsz % 2 == 0 else 1
    dt3 = dt2.reshape(bsz, seq, dt2.shape[1])
    const = lambda shape: pl.BlockSpec(shape, lambda b, c: (0,) * len(shape))
    in_specs = [
        pl.BlockSpec((nbs, q, d_ssm), lambda b, c: (b, cidx(c), 0)),
        pl.BlockSpec((nbs, q, gn), lambda b, c: (b, cidx(c), d_ssm // gn)),
        pl.BlockSpec((nbs, q, gn), lambda b, c: (b, cidx(c), d_ssm // gn + 1)),
        pl.BlockSpec((nbs, q, LANES), lambda b, c: (b, cidx(c), d)),
    ]
    in_specs += [pl.BlockSpec((LANES, q), lambda b, c, s=s: (d, (b * nbs + s) * nc + cidx(c))) for s in range(nbs)]
    in_specs += [const((1, LANES)), const((LANES, q)), const((1, LANES)), const((LANES, q)),
                 const((q, q)), const((q, q)), const((LANES, d_ssm))]
    args = [xbc3, xbc3, xbc3, dt3] + [dtt2] * nbs + [dtb, dtbt, alog, alogt, tri, trit, e3]
    if final:
        yf3, proj4, dskip_x, norm_w = final_args
        tn = proj4.shape[3]
        z_specs = [pl.BlockSpec((1, nbs, q, tn), lambda b, c, j=j: (j, b, cidx(c), 0)) for j in range(d_ssm // tn)]
        in_specs += [pl.BlockSpec((nbs, q, d_ssm), lambda b, c: (b, cidx(c), 0))] + z_specs
        in_specs += [const((1, d_ssm)), const((1, d_ssm))]
        args += [yf3] + [proj4] * len(z_specs) + [dskip_x, norm_w]
    kern = functools.partial(_ssd_kernel, reverse=reverse, final=final, n_heads=n_heads, nbs=nbs)
    return pl.pallas_call(
        kern,
        grid=(bsz // nbs, nc),
        in_specs=in_specs,
        out_specs=pl.BlockSpec((nbs, q, d_ssm), lambda b, c: (b, cidx(c), 0)),
        out_shape=jax.ShapeDtypeStruct((bsz, seq, d_ssm), BF16 if final else F32),
        scratch_shapes=[pltpu.VMEM((nbs, D_STATE, d_ssm), F32)],
        compiler_params=_cparams(("parallel", "arbitrary")),
        name="ssd_bwd" if reverse else "ssd_fwd",
    )(*args)


def _outproj_kernel(ys_ref, yc_ref, wa_ref, wb_ref, x_ref, nw_ref, wr_hi_ref, wr_lo_ref, br_ref, x1_ref, lg_ref):
    acc = jnp.dot(ys_ref[...], wa_ref[...], preferred_element_type=F32)
    acc = acc + jnp.dot(yc_ref[...], wb_ref[...], preferred_element_type=F32)
    x1 = x_ref[...] + acc
    x1_ref[...] = x1
    ms = jnp.mean(x1 * x1, axis=-1, keepdims=True)
    h_hi, h_lo = _split2(x1 * lax.rsqrt(ms + EPS) * nw_ref[...])
    lg_ref[...] = _dot_split(h_hi, h_lo, wr_hi_ref, wr_lo_ref) + br_ref[...]


def _out_proj(y_ssm, y_conf, w_out, x2d, norm_w, wr_hi, wr_lo, b_router, tm):
    t, d = x2d.shape
    ka = y_ssm.shape[1]
    ne = wr_hi.shape[1]
    resident = lambda shape, idx: pl.BlockSpec(shape, lambda m: idx, pipeline_mode=pl.Buffered(1))
    return pl.pallas_call(
        _outproj_kernel,
        grid=(t // tm,),
        in_specs=[
            pl.BlockSpec((tm, ka), lambda m: (m, 0)),
            pl.BlockSpec((tm, ka), lambda m: (m, 0)),
            resident((ka, d), (0, 0)),
            resident((ka, d), (1, 0)),
            pl.BlockSpec((tm, d), lambda m: (m, 0)),
            pl.BlockSpec((1, d), lambda m: (0, 0)),
            resident((d, ne), (0, 0)),
            resident((d, ne), (0, 0)),
            pl.BlockSpec((1, ne), lambda m: (0, 0)),
        ],
        out_specs=[
            pl.BlockSpec((tm, d), lambda m: (m, 0)),
            pl.BlockSpec((tm, ne), lambda m: (m, 0)),
        ],
        out_shape=[
            jax.ShapeDtypeStruct((t, d), F32),
            jax.ShapeDtypeStruct((t, ne), F32),
        ],
        compiler_params=_cparams(("parallel",)),
        name="out_proj",
    )(y_ssm, y_conf, w_out, w_out, x2d, norm_w, wr_hi, wr_lo, b_router)


def _router_kernel(lg_ref, tri_ref, ir_ref, gate_ref, cnt_ref, run_scr, *, n_experts):
    @pl.when(pl.program_id(0) == 0)
    def _():
        run_scr[...] = jnp.zeros_like(run_scr)

    tr = lg_ref.shape[0]
    lane = lax.broadcasted_iota(jnp.int32, (tr, LANES), 1)
    lane_f = lane.astype(F32)
    lg = jnp.where(lane < n_experts, lg_ref[...], -jnp.inf)
    vals, ids, hots = [], [], []
    for _ in range(TOP_K):
        m = jnp.max(lg, axis=1, keepdims=True)
        idx = jnp.min(jnp.where(lg == m, lane_f, float(LANES)), axis=1, keepdims=True)
        hot = lane_f == idx
        vals.append(m)
        ids.append(idx)
        hots.append(hot)
        lg = jnp.where(hot, -jnp.inf, lg)
    es = [jnp.exp(v - vals[0]) for v in vals]
    den = es[0]
    for e in es[1:]:
        den = den + e
    hot_any = hots[0].astype(F32)
    for hot in hots[1:]:
        hot_any = hot_any + hot.astype(F32)
    before = jnp.dot(tri_ref[...], hot_any.astype(BF16), preferred_element_type=F32) + run_scr[...]
    run_scr[...] = run_scr[...] + jnp.sum(hot_any, axis=0, keepdims=True)
    cnt_ref[...] = run_scr[...].astype(jnp.int32)
    out_i = jnp.zeros((tr, LANES), F32)
    out_g = jnp.zeros((tr, LANES), F32)
    for k in range(TOP_K):
        rank = jnp.sum(jnp.where(hots[k], before, 0.0), axis=1, keepdims=True)
        out_i = jnp.where(lane == k, ids[k], out_i)
        out_i = jnp.where(lane == TOP_K + k, rank, out_i)
        out_g = jnp.where(lane == k, es[k] / den, out_g)
    ir_ref[...] = out_i.astype(jnp.int32)
    gate_ref[...] = out_g


def _router(logits, n_experts, tr):
    t = logits.shape[0]
    tri = (jnp.arange(tr)[None, :] < jnp.arange(tr)[:, None]).astype(BF16)
    kern = functools.partial(_router_kernel, n_experts=n_experts)
    return pl.pallas_call(
        kern,
        grid=(t // tr,),
        in_specs=[pl.BlockSpec((tr, LANES), lambda i: (i, 0)), pl.BlockSpec((tr, tr), lambda i: (0, 0))],
        out_specs=[pl.BlockSpec((tr, LANES), lambda i: (i, 0)), pl.BlockSpec((tr, LANES), lambda i: (i, 0)),
                   pl.BlockSpec((1, LANES), lambda i: (0, 0))],
        out_shape=[jax.ShapeDtypeStruct((t, LANES), jnp.int32), jax.ShapeDtypeStruct((t, LANES), F32),
                   jax.ShapeDtypeStruct((1, LANES), jnp.int32)],
        scratch_shapes=[pltpu.VMEM((1, LANES), F32)],
        compiler_params=_cparams(("arbitrary",)),
        name="moe_router",
    )(logits, tri)


_HI16 = 0xFFFF0000


def _pack_bf16_pair(a, b):
    au = lax.bitcast_convert_type(a.astype(BF16).astype(F32), jnp.uint32)
    bu = lax.bitcast_convert_type(b.astype(BF16).astype(F32), jnp.uint32)
    return (au & jnp.uint32(_HI16)) | lax.shift_right_logical(bu, jnp.uint32(16))


def _unpack_bf16_pair(w):
    a = lax.bitcast_convert_type(w & jnp.uint32(_HI16), F32).astype(BF16)
    b = lax.bitcast_convert_type(lax.shift_left(w, jnp.uint32(16)), F32).astype(BF16)
    return a, b


def _dispatch_kernel(zs_ref, idx_ref, x1_ref, nw_ref, xs_hbm, zero_scr, pk, sem, zsem, *, tm, bm, n_experts):
    i = pl.program_id(0)
    last = pl.num_programs(0) - 1
    slot = lax.rem(i, 2)
    half = x1_ref.shape[1] // 2
    rt = ROW_TILE

    def zero_copy(e):
        dst = xs_hbm.at[pl.ds(pl.multiple_of(zs_ref[e], rt * bm), rt * bm), :]
        return pltpu.make_async_copy(zero_scr, dst, zsem)

    @pl.when(i == 0)
    def _():
        zero_scr[...] = jnp.zeros_like(zero_scr)
        for e in range(n_experts):
            @pl.when(zs_ref[e] >= 0)
            def _():
                zero_copy(e).start()
        for e in range(n_experts):
            @pl.when(zs_ref[e] >= 0)
            def _():
                zero_copy(e).wait()

    def wait_rows(s):
        rows = xs_hbm.at[pl.ds(0, rt * TOP_K * tm), :]
        pltpu.make_async_copy(rows, rows, sem.at[s]).wait()

    @pl.when(i >= 2)
    def _():
        wait_rows(slot)

    x = x1_ref[...]
    ms = jnp.mean(x * x, axis=-1, keepdims=True)
    h = x * lax.rsqrt(ms + EPS) * nw_ref[...]
    packed = _pack_bf16_pair(h[:, :half], h[:, half:])
    for c in range(rt):
        pk[slot, pl.ds(c, tm, stride=rt), :] = packed[:, c * LANES:(c + 1) * LANES]

    def body(t, carry):
        src = pk.at[slot, pl.ds(pl.multiple_of(t * rt, rt), rt), :]
        for k in range(TOP_K):
            r = pl.multiple_of(idx_ref[0, k, t], rt)
            pltpu.make_async_copy(src, xs_hbm.at[pl.ds(r, rt), :], sem.at[slot]).start()
        return carry

    lax.fori_loop(0, tm, body, 0, unroll=2)

    @pl.when(i == last)
    def _():
        wait_rows(slot)

    @pl.when(jnp.logical_and(i == last, i >= 1))
    def _():
        wait_rows(1 - slot)


def _dispatch(dest_kt, zero_start, x1, norm_w, n_rows, tm, bm):
    t, d = x1.shape
    assert d // 2 == ROW_TILE * LANES
    nt = t // tm
    n_experts = zero_start.shape[0]
    idx3 = (dest_kt * ROW_TILE).reshape(TOP_K, nt, tm).transpose(1, 0, 2)
    zero_start = zero_start * ROW_TILE
    kern = functools.partial(_dispatch_kernel, tm=tm, bm=bm, n_experts=n_experts)
    return pl.pallas_call(
        kern,
        grid_spec=pltpu.PrefetchScalarGridSpec(
            num_scalar_prefetch=1,
            grid=(nt,),
            in_specs=[
                pl.BlockSpec((1, TOP_K, tm), lambda i, zs: (i, 0, 0), memory_space=pltpu.SMEM),
                pl.BlockSpec((tm, d), lambda i, zs: (i, 0)),
                pl.BlockSpec((1, d), lambda i, zs: (0, 0)),
            ],
            out_specs=pl.BlockSpec(memory_space=pl.ANY),
            scratch_shapes=[pltpu.VMEM((ROW_TILE * bm, LANES), jnp.uint32),
                            pltpu.VMEM((2, ROW_TILE * tm, LANES), jnp.uint32),
                            pltpu.SemaphoreType.DMA((2,)), pltpu.SemaphoreType.DMA(())],
        ),
        out_shape=jax.ShapeDtypeStruct((ROW_TILE * n_rows, LANES), jnp.uint32),
        compiler_params=_cparams(("arbitrary",)),
        name="moe_dispatch",
    )(zero_start, idx3, x1, norm_w)


def _moe_up_kernel(be_ref, nu_ref, x_ref, wg_ref, bg_ref, wu_ref, bu_ref, o_ref):
    @pl.when(pl.program_id(1) < nu_ref[0])
    def _():
        bm = x_ref.shape[0] // ROW_TILE
        words = jnp.concatenate([x_ref[pl.ds(c, bm, stride=ROW_TILE), :] for c in range(ROW_TILE)], axis=1)
        xa, xb = _unpack_bf16_pair(words)
        half = xa.shape[1]
        wg = wg_ref[0].astype(BF16)
        wu = wu_ref[0].astype(BF16)
        gate = (jnp.dot(xa, wg[:half], preferred_element_type=F32)
                + jnp.dot(xb, wg[half:], preferred_element_type=F32) + bg_ref[0])
        up = (jnp.dot(xa, wu[:half], preferred_element_type=F32)
              + jnp.dot(xb, wu[half:], preferred_element_type=F32) + bu_ref[0])
        gate = jnp.minimum(gate, SWIGLU_LIMIT)
        up = jnp.clip(up, -SWIGLU_LIMIT, SWIGLU_LIMIT)
        act = (up + 1.0) * gate * _sigmoid(SWIGLU_ALPHA * gate)
        o_ref[...] = act.astype(o_ref.dtype)


def _moe_down_kernel(be_ref, nu_ref, a_ref, wd_ref, bd_ref, o_ref):
    @pl.when(pl.program_id(0) < nu_ref[0])
    def _():
        y = jnp.dot(a_ref[...], wd_ref[0].astype(BF16), preferred_element_type=F32) + bd_ref[0]
        bm, d = y.shape
        packed = _pack_bf16_pair(y[:, :d // 2], y[:, d // 2:])
        for c in range(ROW_TILE):
            o_ref[pl.ds(c, bm, stride=ROW_TILE), :] = packed[:, c * LANES:(c + 1) * LANES]


def _blk(b, nu_ref):
    return jnp.minimum(b, nu_ref[0] - 1)


def _moe_up(block_expert, n_used, xs, w_gate, b_gate3, w_up, b_up3, bm, tn):
    n_rows = xs.shape[0] // ROW_TILE
    d, dff = w_gate.shape[1], w_gate.shape[2]
    nb = n_rows // bm
    wspec = pl.BlockSpec((1, d, tn), lambda n, b, be, nu: (be[_blk(b, nu)], 0, n))
    bspec = pl.BlockSpec((1, 1, tn), lambda n, b, be, nu: (be[_blk(b, nu)], 0, n))
    return pl.pallas_call(
        _moe_up_kernel,
        grid_spec=pltpu.PrefetchScalarGridSpec(
            num_scalar_prefetch=2,
            grid=(dff // tn, nb),
            in_specs=[pl.BlockSpec((ROW_TILE * bm, LANES), lambda n, b, be, nu: (_blk(b, nu), 0)),
                      wspec, bspec, wspec, bspec],
            out_specs=pl.BlockSpec((bm, tn), lambda n, b, be, nu: (_blk(b, nu), n)),
        ),
        out_shape=jax.ShapeDtypeStruct((n_rows, dff), BF16),
        compiler_params=_cparams(("parallel", "arbitrary")),
        name="moe_up",
    )(block_expert, n_used, xs, w_gate, b_gate3, w_up, b_up3)


def _moe_down(block_expert, n_used, act, w_down, b_down3, bm):
    n_rows, dff = act.shape
    d = w_down.shape[2]
    assert d // 2 == ROW_TILE * LANES
    nb = n_rows // bm
    return pl.pallas_call(
        _moe_down_kernel,
        grid_spec=pltpu.PrefetchScalarGridSpec(
            num_scalar_prefetch=2,
            grid=(nb,),
            in_specs=[
                pl.BlockSpec((bm, dff), lambda b, be, nu: (_blk(b, nu), 0)),
                pl.BlockSpec((1, dff, d), lambda b, be, nu: (be[_blk(b, nu)], 0, 0)),
                pl.BlockSpec((1, 1, d), lambda b, be, nu: (be[_blk(b, nu)], 0, 0)),
            ],
            out_specs=pl.BlockSpec((ROW_TILE * bm, LANES), lambda b, be, nu: (_blk(b, nu), 0)),
        ),
        out_shape=jax.ShapeDtypeStruct((ROW_TILE * n_rows, LANES), jnp.uint32),
        compiler_params=_cparams(("arbitrary",)),
        name="moe_down",
    )(block_expert, n_used, act, w_down, b_down3)


def _combine_kernel(idx_ref, idxn_ref, y_hbm, x1_ref, g_ref, nw_ref, o_ref, buf, sem, *, tm):
    i = pl.program_id(0)
    slot = lax.rem(i, 2)
    rt = ROW_TILE

    def start_tile(ids_ref, s):
        def body(t, carry):
            for k in range(TOP_K):
                r = pl.multiple_of(ids_ref[0, k, t], rt)
                dst = buf.at[s, pl.ds(pl.multiple_of((k * tm + t) * rt, rt), rt), :]
                pltpu.make_async_copy(y_hbm.at[pl.ds(r, rt), :], dst, sem.at[s]).start()
            return carry

        lax.fori_loop(0, tm, body, 0, unroll=2)

    @pl.when(i == 0)
    def _():
        start_tile(idx_ref, 0)

    @pl.when(i + 1 < pl.num_programs(0))
    def _():
        start_tile(idxn_ref, 1 - slot)

    pltpu.make_async_copy(y_hbm.at[pl.ds(0, rt * TOP_K * tm), :], buf.at[slot], sem.at[slot]).wait()
    acc = x1_ref[...]
    g = g_ref[...]
    for k in range(TOP_K):
        words = jnp.concatenate(
            [buf[slot, pl.ds(rt * k * tm + c, tm, stride=rt), :] for c in range(rt)], axis=1)
        y = jnp.concatenate([lax.bitcast_convert_type(words & jnp.uint32(_HI16), F32),
                             lax.bitcast_convert_type(lax.shift_left(words, jnp.uint32(16)), F32)], axis=1)
        acc = acc + g[:, k:k + 1] * y
    ms = jnp.mean(acc * acc, axis=-1, keepdims=True)
    o_ref[...] = acc * lax.rsqrt(ms + EPS) * nw_ref[...]


def _combine(dest_kt, y_rows, x1, gates, norm_w, tm):
    t, d = x1.shape
    nt = t // tm
    idx3 = (dest_kt * ROW_TILE).reshape(TOP_K, nt, tm).transpose(1, 0, 2)
    kern = functools.partial(_combine_kernel, tm=tm)
    return pl.pallas_call(
        kern,
        grid=(nt,),
        in_specs=[
            pl.BlockSpec((1, TOP_K, tm), lambda i: (i, 0, 0), memory_space=pltpu.SMEM),
            pl.BlockSpec((1, TOP_K, tm), lambda i: (jnp.minimum(i + 1, nt - 1), 0, 0), memory_space=pltpu.SMEM),
            pl.BlockSpec(memory_space=pl.ANY),
            pl.BlockSpec((tm, d), lambda i: (i, 0)),
            pl.BlockSpec((tm, TOP_K), lambda i: (i, 0)),
            pl.BlockSpec((1, d), lambda i: (0, 0)),
        ],
        out_specs=pl.BlockSpec((tm, d), lambda i: (i, 0)),
        out_shape=jax.ShapeDtypeStruct((t, d), F32),
        scratch_shapes=[pltpu.VMEM((2, ROW_TILE * TOP_K * tm, LANES), jnp.uint32), pltpu.SemaphoreType.DMA((2,))],
        compiler_params=_cparams(("arbitrary",)),
        name="moe_combine",
    )(idx3, idx3, y_rows, x1, gates, norm_w)


def _route_tables(ids_ranks, counts, n_experts, bm, n_blocks):
    top_idx = ids_ranks[:, :TOP_K]
    rank = ids_ranks[:, TOP_K:2 * TOP_K]
    cnt = counts[0, :n_experts]
    padded = ((cnt + bm - 1) // bm) * bm
    padded_end = jnp.cumsum(padded)
    padded_start = padded_end - padded
    hot = top_idx[:, :, None] == jnp.arange(n_experts, dtype=jnp.int32)[None, None, :]
    dest = rank + jnp.sum(jnp.where(hot, padded_start[None, None, :], 0), axis=-1)
    n_used = (padded_end[-1] // bm).astype(jnp.int32)
    block_start = jnp.arange(n_blocks, dtype=jnp.int32) * bm
    block_expert = jnp.minimum(jnp.sum(padded_end[None, :] <= block_start[:, None], axis=1), n_experts - 1)
    zero_start = jnp.where(cnt > 0, padded_end - bm, -1).astype(jnp.int32)
    return dest.astype(jnp.int32).T, block_expert.astype(jnp.int32), n_used.reshape(1), zero_start


def _pick(n, candidates):
    for c in candidates:
        if n % c == 0:
            return c
    return n


def _layer(x, norm_mix_w, w_in, conv_ssm_w, conv_ssm_b, dt_bias_fwd, dt_bias_bwd, a_log_fwd, a_log_bwd,
           d_skip, ssm_norm_w, conf_dw_w, conf_dw_b, conf_ln_w, conf_ln_b, w_out, norm_ffn_w, w_router,
           b_router, w_gate, b_gate, w_up, b_up, w_down, b_down):
    bsz, seq, d = x.shape
    t = bsz * seq
    n_heads = dt_bias_fwd.shape[0]
    d_ssm = n_heads * HEAD_DIM
    d_xbc = conv_ssm_w.shape[1]
    d_conf = conf_dw_w.shape[1]
    n_experts = w_gate.shape[0]
    row = lambda v: v.reshape(1, -1).astype(F32)

    tn = PROJ_TILE
    c_dt = d_ssm + d_xbc
    c_conf = c_dt + 2 * n_heads
    starts = ([s for s in range(0, d_ssm, tn)] + [c_conf + s for s in range(0, 2 * d_conf, tn)]
              + [d_ssm + s for s in range(0, d_xbc, tn)])
    w_tiles = jnp.stack([w_in[:, s:s + tn] for s in starts]).astype(BF16)
    wf = w_in[:, c_dt:c_dt + n_heads]
    wb = w_in[:, c_dt + n_heads:c_dt + 2 * n_heads]
    zpad = jnp.zeros((d, LANES - 3 * n_heads), F32)
    w_dt = jnp.concatenate([wf, wf, wf, zpad, wb, wb, wb, zpad], axis=1)
    rep3 = lambda v: jnp.concatenate([v, v, v, jnp.zeros((LANES - 3 * n_heads,), F32)])
    x2d = x.reshape(t, d)

    def split2(w):
        hi = w.astype(BF16)
        return hi, (w - hi.astype(F32)).astype(BF16)

    h, dt2, dtt2 = _norm_dt(x2d, row(norm_mix_w), *split2(w_dt), _pick(t, (512, 256, 128)))
    proj4 = _in_proj(h, w_tiles, _pick(t, (2048, 1024, 512, 256, 128))).reshape(-1, bsz, seq, tn)
    tile_conf = d_ssm // tn
    tile_xbc = (d_ssm + 2 * d_conf) // tn

    xbc = _conv_ssm(proj4, conv_ssm_w, row(conv_ssm_b), tile_xbc, _pick(seq, (512, 256, 128)), 512)
    y_conf = _conformer(proj4, conf_dw_w, row(conf_dw_b), row(conf_ln_w), row(conf_ln_b),
                        tile_conf, tile_conf + d_conf // tn, _pick(seq, (256, 128)))

    e3 = ((jnp.arange(LANES)[:, None] % n_heads == jnp.arange(d_ssm)[None, :] // HEAD_DIM)
          & (jnp.arange(LANES)[:, None] < 3 * n_heads)).astype(BF16)
    prm = {}
    for name, bias, alog in (("f", dt_bias_fwd, a_log_fwd), ("b", dt_bias_bwd, a_log_bwd)):
        b3 = rep3(bias.astype(F32))
        a3 = rep3(alog.astype(F32))
        prm[name] = (b3.reshape(1, LANES), jnp.broadcast_to(b3[:, None], (LANES, CHUNK)),
                     a3.reshape(1, LANES), jnp.broadcast_to(a3[:, None], (LANES, CHUNK)))
    y_f = _ssd(xbc, dt2, dtt2, *prm["f"], e3, reverse=False, n_heads=n_heads)
    dskip_x = jnp.repeat(d_skip.astype(F32), HEAD_DIM).reshape(1, d_ssm)
    y_ssm = _ssd(xbc, dt2, dtt2, *prm["b"], e3, reverse=True, n_heads=n_heads,
                 final_args=(y_f, proj4, dskip_x, row(ssm_norm_w)))

    wr = jnp.zeros((d, LANES), F32).at[:, :n_experts].set(w_router.astype(F32))
    br = jnp.zeros((1, LANES), F32).at[0, :n_experts].set(b_router.astype(F32))
    x1, logits = _out_proj(y_ssm.reshape(t, d_ssm), y_conf.reshape(t, d_conf), w_out.astype(BF16), x2d,
                           row(norm_ffn_w), *split2(wr), br, _pick(t, (512, 256, 128)))

    bm = MOE_ROWS
    n_blocks = -(-(t * TOP_K + n_experts * (bm - 1)) // bm)
    ids_ranks, gates_x, counts = _router(logits, n_experts, _pick(t, (1024, 512, 256, 128)))
    gates = gates_x[:, :TOP_K]
    dest_kt, block_expert, n_used, zero_start = _route_tables(ids_ranks, counts, n_experts, bm, n_blocks)
    xs = _dispatch(dest_kt, zero_start, x1, row(norm_ffn_w), n_blocks * bm, _pick(t, (512, 256, 128)), bm)
    act = _moe_up(block_expert, n_used, xs, w_gate, b_gate[:, None, :], w_up, b_up[:, None, :], bm, 1024)
    y_rows = _moe_down(block_expert, n_used, act, w_down, b_down[:, None, :], bm)
    return x1, gates, dest_kt, y_rows


def kernel(x, norm_mix_w, w_in, conv_ssm_w, conv_ssm_b, dt_bias_fwd, dt_bias_bwd, a_log_fwd, a_log_bwd, d_skip,
           ssm_norm_w, conf_dw_w, conf_dw_b, conf_ln_w, conf_ln_b, w_out, norm_ffn_w, w_router, b_router, w_gate,
           b_gate, w_up, b_up, w_down, b_down, norm_final_w):
    assert w_in.shape[0] == 1, "a single layer is supported"
    bsz, seq, d = x.shape
    t = bsz * seq
    x1, gates, dest_kt, y_rows = _layer(
        x, norm_mix_w[0], w_in[0], conv_ssm_w[0], conv_ssm_b[0], dt_bias_fwd[0], dt_bias_bwd[0], a_log_fwd[0],
        a_log_bwd[0], d_skip[0], ssm_norm_w[0], conf_dw_w[0], conf_dw_b[0], conf_ln_w[0], conf_ln_b[0], w_out[0],
        norm_ffn_w[0], w_router[0], b_router[0], w_gate[0], b_gate[0], w_up[0], b_up[0], w_down[0], b_down[0])
    out = _combine(dest_kt, y_rows, x1, gates, norm_final_w.reshape(1, d).astype(F32), _pick(t, (256, 128)))
    return out.reshape(bsz, seq, d)
```

```python
import functools
import math

import jax
import jax.numpy as jnp
from jax import lax
from jax.experimental import pallas as pl
from jax.experimental.pallas import tpu as pltpu

F32 = jnp.float32
BF16 = jnp.bfloat16

EPS = 1e-5
HEAD_DIM = 64
SSM_GROUPS = 4
D_STATE = 128
CHUNK = 128
SSM_CONV = 7
CONF_KERNEL = 31
TOP_K = 4
SWIGLU_LIMIT = 7.0
SWIGLU_ALPHA = 1.702

LANES = 128
SUBLANES = 8
VMEM_LIMIT_BYTES = 56 * 1024 * 1024
HALO = 16
MOE_ROWS = 512
MOE_BUCKETS = 4
ROW_TILE = SUBLANES
PROJ_TILE = 1024

HIGHEST = lax.Precision.HIGHEST


def _cparams(sem):
    return pltpu.CompilerParams(dimension_semantics=sem, vmem_limit_bytes=VMEM_LIMIT_BYTES)


def _sigmoid(v):
    return 1.0 / (1.0 + jnp.exp(-v))


def _silu(v):
    return v * _sigmoid(v)


def _softplus(v):
    return jnp.maximum(v, 0.0) + jnp.log(1.0 + jnp.exp(-jnp.abs(v)))


def _split2(v):
    hi = v.astype(BF16)
    lo = (v - hi.astype(F32)).astype(BF16)
    return hi, lo


def _dot_split(a_hi, a_lo, b_hi_ref, b_lo_ref):
    b_hi = b_hi_ref[...]
    acc = jnp.dot(a_hi, b_hi, preferred_element_type=F32)
    acc = acc + jnp.dot(a_lo, b_hi, preferred_element_type=F32)
    return acc + jnp.dot(a_hi, b_lo_ref[...], preferred_element_type=F32)


def _norm_dt_kernel(x_ref, nw_ref, wdt_hi_ref, wdt_lo_ref, h_ref, dt_ref, dtt_ref):
    x = x_ref[...]
    ms = jnp.mean(x * x, axis=-1, keepdims=True)
    h_hi, h_lo = _split2(x * lax.rsqrt(ms + EPS) * nw_ref[...])
    h_ref[...] = h_hi
    dt = _dot_split(h_hi, h_lo, wdt_hi_ref, wdt_lo_ref)
    dt_ref[...] = dt
    dtt_ref[...] = dt.T


def _norm_dt(x2d, norm_w, w_dt_hi, w_dt_lo, tm):
    t, d = x2d.shape
    n_dt = w_dt_hi.shape[1]
    return pl.pallas_call(
        _norm_dt_kernel,
        grid=(t // tm,),
        in_specs=[
            pl.BlockSpec((tm, d), lambda m: (m, 0)),
            pl.BlockSpec((1, d), lambda m: (0, 0)),
            pl.BlockSpec((d, n_dt), lambda m: (0, 0)),
            pl.BlockSpec((d, n_dt), lambda m: (0, 0)),
        ],
        out_specs=[
            pl.BlockSpec((tm, d), lambda m: (m, 0)),
            pl.BlockSpec((tm, n_dt), lambda m: (m, 0)),
            pl.BlockSpec((n_dt, tm), lambda m: (0, m)),
        ],
        out_shape=[
            jax.ShapeDtypeStruct((t, d), BF16),
            jax.ShapeDtypeStruct((t, n_dt), F32),
            jax.ShapeDtypeStruct((n_dt, t), F32),
        ],
        compiler_params=_cparams(("parallel",)),
        name="norm_dt",
    )(x2d, norm_w, w_dt_hi, w_dt_lo)


def _inproj_kernel(h_ref, w_ref, o_ref):
    o_ref[0] = jnp.dot(h_ref[...], w_ref[0], preferred_element_type=F32).astype(o_ref.dtype)


def _in_proj(h, w_tiles, tm):
    t, d = h.shape
    n_tiles, _, tn = w_tiles.shape
    return pl.pallas_call(
        _inproj_kernel,
        grid=(t // tm, n_tiles),
        in_specs=[
            pl.BlockSpec((tm, d), lambda m, n: (m, 0)),
            pl.BlockSpec((1, d, tn), lambda m, n: (n, 0, 0)),
        ],
        out_specs=pl.BlockSpec((1, tm, tn), lambda m, n: (n, m, 0)),
        out_shape=jax.ShapeDtypeStruct((n_tiles, t, tn), BF16),
        compiler_params=_cparams(("parallel", "arbitrary")),
        name="in_proj",
    )(h, w_tiles)


def _fill_halo_scratch(scr, cur, prev, nxt, lt):
    l = pl.program_id(1)
    last = pl.num_programs(1) - 1
    scr[HALO:HALO + lt, :] = cur
    scr[0:HALO, :] = jnp.where(l > 0, prev, 0.0)
    scr[HALO + lt:HALO + lt + HALO, :] = jnp.where(l < last, nxt, 0.0)


def _conv_ssm_kernel(cur_ref, prev_ref, next_ref, w_ref, b_ref, o_ref, scr, *, lt, rc):
    _fill_halo_scratch(scr, cur_ref[0, 0].astype(F32), prev_ref[0, 0].astype(F32), next_ref[0, 0].astype(F32), lt)
    pad = SSM_CONV // 2
    w = w_ref[...]
    bias = b_ref[...]

    def body(i, carry):
        r0 = pl.multiple_of(i * rc, rc)
        win = scr[pl.ds(r0 + (HALO - SUBLANES), rc + 2 * SUBLANES), :]
        acc = jnp.zeros((rc, scr.shape[1]), F32)
        for k in range(SSM_CONV):
            o = SUBLANES - pad + k
            acc = acc + w[k:k + 1, :] * win[o:o + rc, :]
        o_ref[0, pl.ds(r0, rc), :] = _silu(acc + bias).astype(o_ref.dtype)
        return carry

    lax.fori_loop(0, lt // rc, body, 0)


def _conv_ssm(proj4, conv_w, conv_b, tile0, lt, ct, rc=32):
    _, bsz, seq, tn = proj4.shape
    c = conv_w.shape[1]
    per = tn // ct
    hb = lt // HALO
    n_hb = seq // HALO
    kern = functools.partial(_conv_ssm_kernel, lt=lt, rc=rc)
    return pl.pallas_call(
        kern,
        grid=(bsz, seq // lt, c // ct),
        in_specs=[
            pl.BlockSpec((1, 1, lt, ct), lambda b, l, j: (tile0 + j // per, b, l, j % per)),
            pl.BlockSpec((1, 1, HALO, ct),
                         lambda b, l, j: (tile0 + j // per, b, jnp.maximum(l * hb - 1, 0), j % per)),
            pl.BlockSpec((1, 1, HALO, ct),
                         lambda b, l, j: (tile0 + j // per, b, jnp.minimum((l + 1) * hb, n_hb - 1), j % per)),
            pl.BlockSpec((SSM_CONV, ct), lambda b, l, j: (0, j)),
            pl.BlockSpec((1, ct), lambda b, l, j: (0, j)),
        ],
        out_specs=pl.BlockSpec((1, lt, ct), lambda b, l, j: (b, l, j)),
        out_shape=jax.ShapeDtypeStruct((bsz, seq, c), BF16),
        scratch_shapes=[pltpu.VMEM((lt + 2 * HALO, ct), F32)],
        compiler_params=_cparams(("parallel", "parallel", "parallel")),
        name="conv_ssm",
    )(proj4, proj4, proj4, conv_w, conv_b)


def _conformer_kernel(*refs, lt, rc, lc, n_t):
    tiles = refs[:6 * n_t]
    w_ref, b_ref, lnw_ref, lnb_ref, o_ref, scr, sh_scr, v_scr = refs[6 * n_t:]

    def glu(group):
        a = jnp.concatenate([r[0, 0].astype(F32) for r in tiles[2 * n_t * group:2 * n_t * group + n_t]], axis=1)
        g = jnp.concatenate([r[0, 0].astype(F32) for r in tiles[2 * n_t * group + n_t:2 * n_t * (group + 1)]], axis=1)
        return a * _sigmoid(g)

    _fill_halo_scratch(scr, glu(0), glu(1), glu(2), lt)
    c = scr.shape[1]
    pad = CONF_KERNEL // 2
    n_sh = lt + 2 * HALO - SUBLANES
    for r0 in range(0, n_sh, rc):
        n = min(rc, n_sh - r0)
        for j in range(c // lc):
            win = scr[r0:r0 + n + SUBLANES, j * lc:(j + 1) * lc]
            for m in range(1, SUBLANES):
                sh_scr[m - 1, r0:r0 + n, j * lc:(j + 1) * lc] = win[m:m + n, :]

    def conv_body(i, carry):
        r0 = pl.multiple_of(i * rc, rc)
        for j in range(c // lc):
            lanes = slice(j * lc, (j + 1) * lc)
            acc = jnp.zeros((rc, lc), F32)
            for m in range(SUBLANES):
                offs = [(k, HALO - pad + k - m) for k in range(CONF_KERNEL) if (HALO - pad + k) % SUBLANES == m]
                if not offs:
                    continue
                span = rc + offs[-1][1]
                if m == 0:
                    win = scr[pl.ds(r0, span), lanes]
                else:
                    win = sh_scr[m - 1, pl.ds(r0, span), lanes]
                for k, base in offs:
                    acc = acc + w_ref[k:k + 1, lanes] * win[base:base + rc, :]
            v_scr[pl.ds(r0, rc), lanes] = acc + b_ref[:, lanes]
        return carry

    lax.fori_loop(0, lt // rc, conv_body, 0)
    lnw = lnw_ref[...]
    lnb = lnb_ref[...]
    rn = 2 * SUBLANES

    def norm_body(i, carry):
        r0 = pl.multiple_of(i * rn, rn)
        v = v_scr[pl.ds(r0, rn), :]
        mu = jnp.mean(v, axis=-1, keepdims=True)
        vc = v - mu
        var = jnp.mean(vc * vc, axis=-1, keepdims=True)
        y = vc * lax.rsqrt(var + EPS) * lnw + lnb
        o_ref[0, pl.ds(r0, rn), :] = _silu(y).astype(o_ref.dtype)
        return carry

    lax.fori_loop(0, lt // rn, norm_body, 0, unroll=4)


def _conformer(proj4, dw_w, dw_b, ln_w, ln_b, tile_a, tile_g, lt, rc=64, lc=128):
    _, bsz, seq, tn = proj4.shape
    c = dw_w.shape[1]
    n_t = c // tn
    hb = lt // HALO
    n_hb = seq // HALO
    kern = functools.partial(_conformer_kernel, lt=lt, rc=rc, lc=lc, n_t=n_t)

    def cur(j):
        return pl.BlockSpec((1, 1, lt, tn), lambda b, l: (j, b, l, 0))

    def prev(j):
        return pl.BlockSpec((1, 1, HALO, tn), lambda b, l: (j, b, jnp.maximum(l * hb - 1, 0), 0))

    def nxt(j):
        return pl.BlockSpec((1, 1, HALO, tn), lambda b, l: (j, b, jnp.minimum((l + 1) * hb, n_hb - 1), 0))

    tile_specs = [mk(t0 + j) for mk in (cur, prev, nxt) for t0 in (tile_a, tile_g) for j in range(n_t)]
    vec = pl.BlockSpec((1, c), lambda b, l: (0, 0))
    return pl.pallas_call(
        kern,
        grid=(bsz, seq // lt),
        in_specs=tile_specs + [pl.BlockSpec((CONF_KERNEL, c), lambda b, l: (0, 0)), vec, vec, vec],
        out_specs=pl.BlockSpec((1, lt, c), lambda b, l: (b, l, 0)),
        out_shape=jax.ShapeDtypeStruct((bsz, seq, c), BF16),
        scratch_shapes=[pltpu.VMEM((lt + 2 * HALO, c), F32),
                        pltpu.VMEM((SUBLANES - 1, lt + 2 * HALO, c), F32),
                        pltpu.VMEM((lt, c), F32)],
        compiler_params=_cparams(("parallel", "parallel")),
        name="conformer",
    )(*([proj4] * len(tile_specs)), dw_w, dw_b, ln_w, ln_b)


def _split3(v):
    hi = v.astype(BF16).astype(F32)
    r = v - hi
    mid = r.astype(BF16).astype(F32)
    lo = (r - mid).astype(BF16).astype(F32)
    return hi, mid, lo


def _ssd_kernel(*refs, reverse, final, n_heads, nbs):
    xs_ref, bm_ref, cm_ref, dt_ref = refs[:4]
    dtt_refs = refs[4:4 + nbs]
    consts = refs[4 + nbs:11 + nbs]
    if final:
        yf_ref = refs[11 + nbs]
        z_refs = refs[12 + nbs:-4]
        dsk_ref, nw_ref, o_ref, state_scr = refs[-4:]
    else:
        yf_ref, z_refs, dsk_ref, nw_ref = None, (), None, None
        o_ref, state_scr = refs[-2:]

    @pl.when(pl.program_id(1) == 0)
    def _():
        state_scr[...] = jnp.zeros_like(state_scr)

    for s in range(nbs):
        _ssd_chunk(s, xs_ref, bm_ref, cm_ref, dt_ref, dtt_refs[s], consts, yf_ref, z_refs, dsk_ref, nw_ref, o_ref,
                   state_scr, reverse=reverse, final=final, n_heads=n_heads)


def _ssd_chunk(s, xs_ref, bm_ref, cm_ref, dt_ref, dtt_ref, consts, yf_ref, z_refs, dsk_ref, nw_ref, o_ref,
               state_scr, *, reverse, final, n_heads):
    dtb_ref, dtbt_ref, alog_ref, alogt_ref, tri_ref, trit_ref, e3_ref = consts
    q = CHUNK
    hg = n_heads
    d_ssm = n_heads * HEAD_DIM
    gw = d_ssm // SSM_GROUPS

    lane = lax.broadcasted_iota(jnp.int32, (q, LANES), 1)
    sub = lax.broadcasted_iota(jnp.int32, (LANES, q), 0)

    def sel3_lanes(v):
        hi, mid, lo = _split3(v)
        return jnp.where(lane < hg, hi, jnp.where(lane < 2 * hg, mid, jnp.where(lane < 3 * hg, lo, 0.0))).astype(BF16)

    def sel3_rows(v):
        hi, mid, lo = _split3(v)
        return jnp.where(sub < hg, hi, jnp.where(sub < 2 * hg, mid, jnp.where(sub < 3 * hg, lo, 0.0))).astype(BF16)

    a_row = -jnp.exp(alog_ref[...])
    dt = _softplus(dt_ref[s] + dtb_ref[...])
    da = dt * a_row
    a_col = -jnp.exp(alogt_ref[...])
    dat = _softplus(dtt_ref[...] + dtbt_ref[...]) * a_col

    tri = tri_ref[...]
    cum3 = jnp.dot(tri, sel3_lanes(da), preferred_element_type=F32)
    cum = cum3
    for r in (1, 2, 3):
        cum = cum + pltpu.roll(cum3, r * hg, axis=1)
    cumt3 = jnp.dot(sel3_rows(dat), trit_ref[...], preferred_element_type=F32)
    cumt = cumt3[0:hg] + cumt3[hg:2 * hg] + cumt3[2 * hg:3 * hg]

    tot_row = 0 if reverse else q - 1
    total = cum[tot_row:tot_row + 1, :]

    e3 = e3_ref[...]
    lhs = jnp.concatenate(
        [sel3_lanes(dt), sel3_lanes(jnp.exp(cum)), sel3_lanes(jnp.exp(total - cum)),
         sel3_lanes(jnp.broadcast_to(jnp.exp(total), (q, LANES)))], axis=0)
    ex = jnp.dot(lhs, e3, preferred_element_type=F32)
    dt_x = ex[0:q]
    ecum_x = ex[q:2 * q]
    edte_x = ex[2 * q:3 * q]
    cdec_x = ex[3 * q:3 * q + 1]

    xs = xs_ref[s].astype(F32)
    xdt = xs * dt_x
    xdt_b = xdt.astype(BF16)
    xdte_b = (xdt * edte_x).astype(BF16)
    bm = bm_ref[s]
    cm = cm_ref[s]

    li = lax.broadcasted_iota(jnp.int32, (q, q), 0)
    si = lax.broadcasted_iota(jnp.int32, (q, q), 1)
    mask = (li <= si) if reverse else (li >= si)
    lane_lo = lax.broadcasted_iota(jnp.int32, (q, LANES), 1) < HEAD_DIM

    hpg = n_heads // SSM_GROUPS
    y_parts = []
    for g in range(SSM_GROUPS):
        bg = bm[:, g * D_STATE:(g + 1) * D_STATE]
        cg = cm[:, g * D_STATE:(g + 1) * D_STATE]
        cb = lax.dot_general(cg, bg, (((1,), (1,)), ((), ())), preferred_element_type=F32)
        st = state_scr[s, :, g * gw:(g + 1) * gw]
        y_off = jnp.dot(cg, st.astype(BF16), preferred_element_type=F32) * ecum_x[:, g * gw:(g + 1) * gw]
        diag = []
        for pair in range(hpg // 2):
            h0 = g * hpg + 2 * pair
            ls = []
            for h in (h0, h0 + 1):
                seg = cum[:, h:h + 1] - cumt[h:h + 1, :]
                ls.append((cb * jnp.exp(jnp.where(mask, seg, -jnp.inf))).astype(BF16))
            l2 = jnp.concatenate(ls, axis=1)
            xp = xdt_b[:, h0 * HEAD_DIM:(h0 + 2) * HEAD_DIM]
            zero = jnp.zeros_like(xp)
            r2 = jnp.concatenate([jnp.where(lane_lo, xp, zero), jnp.where(lane_lo, zero, xp)], axis=0)
            diag.append(jnp.dot(l2, r2, preferred_element_type=F32))
        y_parts.append(jnp.concatenate(diag, axis=1) + y_off)
        contrib = lax.dot_general(bg, xdte_b[:, g * gw:(g + 1) * gw], (((0,), (0,)), ((), ())),
                                  preferred_element_type=F32)
        state_scr[s, :, g * gw:(g + 1) * gw] = st * cdec_x[:, g * gw:(g + 1) * gw] + contrib
    y = jnp.concatenate(y_parts, axis=1)

    if not final:
        o_ref[s] = y
    else:
        y = y + yf_ref[s] + dsk_ref[...] * xs
        y = y * _silu(jnp.concatenate([r[0, s].astype(F32) for r in z_refs], axis=1))
        outs = []
        for g in range(SSM_GROUPS):
            yg = y[:, g * gw:(g + 1) * gw]
            ms = jnp.mean(yg * yg, axis=-1, keepdims=True)
            outs.append(yg * lax.rsqrt(ms + EPS))
        o_ref[s] = (jnp.concatenate(outs, axis=1) * nw_ref[...]).astype(o_ref.dtype)


def _ssd(xbc3, dt2, dtt2, dtb, dtbt, alog, alogt, e3, *, reverse, n_heads, final_args=None):
    bsz, seq, _ = xbc3.shape
    d_ssm = n_heads * HEAD_DIM
    gn = SSM_GROUPS * D_STATE
    nc = seq // CHUNK
    q = CHUNK
    d = 1 if reverse else 0
    final = final_args is not None

    def cidx(c):
        return (nc - 1 - c) if reverse else c

    li = jnp.arange(q)[:, None]
    ji = jnp.arange(q)[None, :]
    tri = ((ji >= li) if reverse else (ji <= li)).astype(BF16)
    trit = tri.T

    nbs = 2 if bsz % 2 == 0 else 1
    dt3 = dt2.reshape(bsz, seq, dt2.shape[1])
    const = lambda shape: pl.BlockSpec(shape, lambda b, c: (0,) * len(shape))
    in_specs = [
        pl.BlockSpec((nbs, q, d_ssm), lambda b, c: (b, cidx(c), 0)),
        pl.BlockSpec((nbs, q, gn), lambda b, c: (b, cidx(c), d_ssm // gn)),
        pl.BlockSpec((nbs, q, gn), lambda b, c: (b, cidx(c), d_ssm // gn + 1)),
        pl.BlockSpec((nbs, q, LANES), lambda b, c: (b, cidx(c), d)),
    ]
    in_specs += [pl.BlockSpec((LANES, q), lambda b, c, s=s: (d, (b * nbs + s) * nc + cidx(c))) for s in range(nbs)]
    in_specs += [const((1, LANES)), const((LANES, q)), const((1, LANES)), const((LANES, q)),
                 const((q, q)), const((q, q)), const((LANES, d_ssm))]
    args = [xbc3, xbc3, xbc3, dt3] + [dtt2] * nbs + [dtb, dtbt, alog, alogt, tri, trit, e3]
    if final:
        yf3, proj4, dskip_x, norm_w = final_args
        tn = proj4.shape[3]
        z_specs = [pl.BlockSpec((1, nbs, q, tn), lambda b, c, j=j: (j, b, cidx(c), 0)) for j in range(d_ssm // tn)]
        in_specs += [pl.BlockSpec((nbs, q, d_ssm), lambda b, c: (b, cidx(c), 0))] + z_specs
        in_specs += [const((1, d_ssm)), const((1, d_ssm))]
        args += [yf3] + [proj4] * len(z_specs) + [dskip_x, norm_w]
    kern = functools.partial(_ssd_kernel, reverse=reverse, final=final, n_heads=n_heads, nbs=nbs)
    return pl.pallas_call(
        kern,
        grid=(bsz // nbs, nc),
        in_specs=in_specs,
        out_specs=pl.BlockSpec((nbs, q, d_ssm), lambda b, c: (b, cidx(c), 0)),
        out_shape=jax.ShapeDtypeStruct((bsz, seq, d_ssm), BF16 if final else F32),
        scratch_shapes=[pltpu.VMEM((nbs, D_STATE, d_ssm), F32)],
        compiler_params=_cparams(("parallel", "arbitrary")),
        name="ssd_bwd" if reverse else "ssd_fwd",
    )(*args)


def _outproj_kernel(ys_ref, yc_ref, wa_ref, wb_ref, x_ref, nw_ref, wr_hi_ref, wr_lo_ref, br_ref, x1_ref, lg_ref):
    acc = jnp.dot(ys_ref[...], wa_ref[...], preferred_element_type=F32)
    acc = acc + jnp.dot(yc_ref[...], wb_ref[...], preferred_element_type=F32)
    x1 = x_ref[...] + acc
    x1_ref[...] = x1
    ms = jnp.mean(x1 * x1, axis=-1, keepdims=True)
    h_hi, h_lo = _split2(x1 * lax.rsqrt(ms + EPS) * nw_ref[...])
    lg_ref[...] = _dot_split(h_hi, h_lo, wr_hi_ref, wr_lo_ref) + br_ref[...]


def _out_proj(y_ssm, y_conf, w_out, x2d, norm_w, wr_hi, wr_lo, b_router, tm):
    t, d = x2d.shape
    ka = y_ssm.shape[1]
    ne = wr_hi.shape[1]
    resident = lambda shape, idx: pl.BlockSpec(shape, lambda m: idx, pipeline_mode=pl.Buffered(1))
    return pl.pallas_call(
        _outproj_kernel,
        grid=(t // tm,),
        in_specs=[
            pl.BlockSpec((tm, ka), lambda m: (m, 0)),
            pl.BlockSpec((tm, ka), lambda m: (m, 0)),
            resident((ka, d), (0, 0)),
            resident((ka, d), (1, 0)),
            pl.BlockSpec((tm, d), lambda m: (m, 0)),
            pl.BlockSpec((1, d), lambda m: (0, 0)),
            resident((d, ne), (0, 0)),
            resident((d, ne), (0, 0)),
            pl.BlockSpec((1, ne), lambda m: (0, 0)),
        ],
        out_specs=[
            pl.BlockSpec((tm, d), lambda m: (m, 0)),
            pl.BlockSpec((tm, ne), lambda m: (m, 0)),
        ],
        out_shape=[
            jax.ShapeDtypeStruct((t, d), F32),
            jax.ShapeDtypeStruct((t, ne), F32),
        ],
        compiler_params=_cparams(("parallel",)),
        name="out_proj",
    )(y_ssm, y_conf, w_out, w_out, x2d, norm_w, wr_hi, wr_lo, b_router)


def _router_kernel(lg_ref, tri_ref, ir_ref, gate_ref, cnt_ref, run_scr, *, n_experts):
    @pl.when(pl.program_id(0) == 0)
    def _():
        run_scr[...] = jnp.zeros_like(run_scr)

    tr = lg_ref.shape[0]
    lane = lax.broadcasted_iota(jnp.int32, (tr, LANES), 1)
    lane_f = lane.astype(F32)
    lg = jnp.where(lane < n_experts, lg_ref[...], -jnp.inf)
    vals, ids, hots = [], [], []
    for _ in range(TOP_K):
        m = jnp.max(lg, axis=1, keepdims=True)
        idx = jnp.min(jnp.where(lg == m, lane_f, float(LANES)), axis=1, keepdims=True)
        hot = lane_f == idx
        vals.append(m)
        ids.append(idx)
        hots.append(hot)
        lg = jnp.where(hot, -jnp.inf, lg)
    es = [jnp.exp(v - vals[0]) for v in vals]
    den = es[0]
    for e in es[1:]:
        den = den + e
    hot_any = hots[0].astype(F32)
    for hot in hots[1:]:
        hot_any = hot_any + hot.astype(F32)
    before = jnp.dot(tri_ref[...], hot_any.astype(BF16), preferred_element_type=F32) + run_scr[...]
    run_scr[...] = run_scr[...] + jnp.sum(hot_any, axis=0, keepdims=True)
    cnt_ref[...] = run_scr[...].astype(jnp.int32)
    out_i = jnp.zeros((tr, LANES), F32)
    out_g = jnp.zeros((tr, LANES), F32)
    for k in range(TOP_K):
        rank = jnp.sum(jnp.where(hots[k], before, 0.0), axis=1, keepdims=True)
        out_i = jnp.where(lane == k, ids[k], out_i)
        out_i = jnp.where(lane == TOP_K + k, rank, out_i)
        out_g = jnp.where(lane == k, es[k] / den, out_g)
    ir_ref[...] = out_i.astype(jnp.int32)
    gate_ref[...] = out_g


def _router(logits, n_experts, tr):
    t = logits.shape[0]
    tri = (jnp.arange(tr)[None, :] < jnp.arange(tr)[:, None]).astype(BF16)
    kern = functools.partial(_router_kernel, n_experts=n_experts)
    return pl.pallas_call(
        kern,
        grid=(t // tr,),
        in_specs=[pl.BlockSpec((tr, LANES), lambda i: (i, 0)), pl.BlockSpec((tr, tr), lambda i: (0, 0))],
        out_specs=[pl.BlockSpec((tr, LANES), lambda i: (i, 0)), pl.BlockSpec((tr, LANES), lambda i: (i, 0)),
                   pl.BlockSpec((1, LANES), lambda i: (0, 0))],
        out_shape=[jax.ShapeDtypeStruct((t, LANES), jnp.int32), jax.ShapeDtypeStruct((t, LANES), F32),
                   jax.ShapeDtypeStruct((1, LANES), jnp.int32)],
        scratch_shapes=[pltpu.VMEM((1, LANES), F32)],
        compiler_params=_cparams(("arbitrary",)),
        name="moe_router",
    )(logits, tri)


_HI16 = 0xFFFF0000


def _pack_bf16_pair(a, b):
    au = lax.bitcast_convert_type(a.astype(BF16).astype(F32), jnp.uint32)
    bu = lax.bitcast_convert_type(b.astype(BF16).astype(F32), jnp.uint32)
    return (au & jnp.uint32(_HI16)) | lax.shift_right_logical(bu, jnp.uint32(16))


def _unpack_bf16_pair(w):
    a = lax.bitcast_convert_type(w & jnp.uint32(_HI16), F32).astype(BF16)
    b = lax.bitcast_convert_type(lax.shift_left(w, jnp.uint32(16)), F32).astype(BF16)
    return a, b


def _dispatch_kernel(zs_ref, idx_ref, x1_ref, nw_ref, xs_hbm, zero_scr, pk, sem, zsem, *, tm, bm, n_experts):
    i = pl.program_id(0)
    last = pl.num_programs(0) - 1
    slot = lax.rem(i, 2)
    half = x1_ref.shape[1] // 2
    rt = ROW_TILE

    def zero_copy(e):
        dst = xs_hbm.at[pl.ds(pl.multiple_of(zs_ref[e], rt * bm), rt * bm), :]
        return pltpu.make_async_copy(zero_scr, dst, zsem)

    @pl.when(i == 0)
    def _():
        zero_scr[...] = jnp.zeros_like(zero_scr)
        for e in range(n_experts):
            @pl.when(zs_ref[e] >= 0)
            def _():
                zero_copy(e).start()
        for e in range(n_experts):
            @pl.when(zs_ref[e] >= 0)
            def _():
                zero_copy(e).wait()

    def wait_rows(s):
        rows = xs_hbm.at[pl.ds(0, rt * TOP_K * tm), :]
        pltpu.make_async_copy(rows, rows, sem.at[s]).wait()

    @pl.when(i >= 2)
    def _():
        wait_rows(slot)

    x = x1_ref[...]
    ms = jnp.mean(x * x, axis=-1, keepdims=True)
    h = x * lax.rsqrt(ms + EPS) * nw_ref[...]
    packed = _pack_bf16_pair(h[:, :half], h[:, half:])
    for c in range(rt):
        pk[slot, pl.ds(c, tm, stride=rt), :] = packed[:, c * LANES:(c + 1) * LANES]

    def body(t, carry):
        src = pk.at[slot, pl.ds(pl.multiple_of(t * rt, rt), rt), :]
        for k in range(TOP_K):
            r = pl.multiple_of(idx_ref[0, k, t], rt)
            pltpu.make_async_copy(src, xs_hbm.at[pl.ds(r, rt), :], sem.at[slot]).start()
        return carry

    lax.fori_loop(0, tm, body, 0, unroll=2)

    @pl.when(i == last)
    def _():
        wait_rows(slot)

    @pl.when(jnp.logical_and(i == last, i >= 1))
    def _():
        wait_rows(1 - slot)


def _dispatch(dest_kt, zero_start, x1, norm_w, n_rows, tm, bm):
    t, d = x1.shape
    assert d // 2 == ROW_TILE * LANES
    nt = t // tm
    n_experts = zero_start.shape[0]
    idx3 = (dest_kt * ROW_TILE).reshape(TOP_K, nt, tm).transpose(1, 0, 2)
    zero_start = zero_start * ROW_TILE
    kern = functools.partial(_dispatch_kernel, tm=tm, bm=bm, n_experts=n_experts)
    return pl.pallas_call(
        kern,
        grid_spec=pltpu.PrefetchScalarGridSpec(
            num_scalar_prefetch=1,
            grid=(nt,),
            in_specs=[
                pl.BlockSpec((1, TOP_K, tm), lambda i, zs: (i, 0, 0), memory_space=pltpu.SMEM),
                pl.BlockSpec((tm, d), lambda i, zs: (i, 0)),
                pl.BlockSpec((1, d), lambda i, zs: (0, 0)),
            ],
            out_specs=pl.BlockSpec(memory_space=pl.ANY),
            scratch_shapes=[pltpu.VMEM((ROW_TILE * bm, LANES), jnp.uint32),
                            pltpu.VMEM((2, ROW_TILE * tm, LANES), jnp.uint32),
                            pltpu.SemaphoreType.DMA((2,)), pltpu.SemaphoreType.DMA(())],
        ),
        out_shape=jax.ShapeDtypeStruct((ROW_TILE * n_rows, LANES), jnp.uint32),
        compiler_params=_cparams(("arbitrary",)),
        name="moe_dispatch",
    )(zero_start, idx3, x1, norm_w)


def _for_row_bucket(b, nu_ref, bq_ref, bm, compute):
    for quarters in range(1, MOE_BUCKETS + 1):
        @pl.when(jnp.logical_and(b < nu_ref[0], bq_ref[b] == quarters))
        def _():
            compute(quarters * (bm // MOE_BUCKETS))


def _moe_up_kernel(be_ref, nu_ref, bq_ref, x_ref, wg_ref, bg_ref, wu_ref, bu_ref, o_ref):
    bm = x_ref.shape[0] // ROW_TILE

    def compute(nv):
        words = jnp.concatenate([x_ref[pl.ds(c, nv, stride=ROW_TILE), :] for c in range(ROW_TILE)], axis=1)
        xa, xb = _unpack_bf16_pair(words)
        half = xa.shape[1]
        wg = wg_ref[0].astype(BF16)
        wu = wu_ref[0].astype(BF16)
        gate = (jnp.dot(xa, wg[:half], preferred_element_type=F32)
                + jnp.dot(xb, wg[half:], preferred_element_type=F32) + bg_ref[0])
        up = (jnp.dot(xa, wu[:half], preferred_element_type=F32)
              + jnp.dot(xb, wu[half:], preferred_element_type=F32) + bu_ref[0])
        gate = jnp.minimum(gate, SWIGLU_LIMIT)
        up = jnp.clip(up, -SWIGLU_LIMIT, SWIGLU_LIMIT)
        act = (up + 1.0) * gate * _sigmoid(SWIGLU_ALPHA * gate)
        o_ref[:nv, :] = act.astype(o_ref.dtype)

    _for_row_bucket(pl.program_id(1), nu_ref, bq_ref, bm, compute)


def _moe_down_kernel(be_ref, nu_ref, bq_ref, a_ref, wd_ref, bd_ref, o_ref):
    def compute(nv):
        y = jnp.dot(a_ref[:nv, :], wd_ref[0].astype(BF16), preferred_element_type=F32) + bd_ref[0]
        d = y.shape[1]
        packed = _pack_bf16_pair(y[:, :d // 2], y[:, d // 2:])
        for c in range(ROW_TILE):
            o_ref[pl.ds(c, nv, stride=ROW_TILE), :] = packed[:, c * LANES:(c + 1) * LANES]

    _for_row_bucket(pl.program_id(0), nu_ref, bq_ref, a_ref.shape[0], compute)


def _blk(b, nu_ref):
    return jnp.minimum(b, nu_ref[0] - 1)


def _moe_up(block_expert, n_used, block_quarters, xs, w_gate, b_gate3, w_up, b_up3, bm, tn):
    n_rows = xs.shape[0] // ROW_TILE
    d, dff = w_gate.shape[1], w_gate.shape[2]
    nb = n_rows // bm
    wspec = pl.BlockSpec((1, d, tn), lambda n, b, be, nu, bq: (be[_blk(b, nu)], 0, n))
    bspec = pl.BlockSpec((1, 1, tn), lambda n, b, be, nu, bq: (be[_blk(b, nu)], 0, n))
    return pl.pallas_call(
        _moe_up_kernel,
        grid_spec=pltpu.PrefetchScalarGridSpec(
            num_scalar_prefetch=3,
            grid=(dff // tn, nb),
            in_specs=[pl.BlockSpec((ROW_TILE * bm, LANES), lambda n, b, be, nu, bq: (_blk(b, nu), 0)),
                      wspec, bspec, wspec, bspec],
            out_specs=pl.BlockSpec((bm, tn), lambda n, b, be, nu, bq: (_blk(b, nu), n)),
        ),
        out_shape=jax.ShapeDtypeStruct((n_rows, dff), BF16),
        compiler_params=_cparams(("parallel", "arbitrary")),
        name="moe_up",
    )(block_expert, n_used, block_quarters, xs, w_gate, b_gate3, w_up, b_up3)


def _moe_down(block_expert, n_used, block_quarters, act, w_down, b_down3, bm):
    n_rows, dff = act.shape
    d = w_down.shape[2]
    assert d // 2 == ROW_TILE * LANES
    nb = n_rows // bm
    return pl.pallas_call(
        _moe_down_kernel,
        grid_spec=pltpu.PrefetchScalarGridSpec(
            num_scalar_prefetch=3,
            grid=(nb,),
            in_specs=[
                pl.BlockSpec((bm, dff), lambda b, be, nu, bq: (_blk(b, nu), 0)),
                pl.BlockSpec((1, dff, d), lambda b, be, nu, bq: (be[_blk(b, nu)], 0, 0)),
                pl.BlockSpec((1, 1, d), lambda b, be, nu, bq: (be[_blk(b, nu)], 0, 0)),
            ],
            out_specs=pl.BlockSpec((ROW_TILE * bm, LANES), lambda b, be, nu, bq: (_blk(b, nu), 0)),
        ),
        out_shape=jax.ShapeDtypeStruct((ROW_TILE * n_rows, LANES), jnp.uint32),
        compiler_params=_cparams(("arbitrary",)),
        name="moe_down",
    )(block_expert, n_used, block_quarters, act, w_down, b_down3)


def _combine_kernel(idx_ref, idxn_ref, y_hbm, x1_ref, g_ref, nw_ref, o_ref, buf, sem, *, tm):
    i = pl.program_id(0)
    slot = lax.rem(i, 2)
    rt = ROW_TILE

    def start_tile(ids_ref, s):
        def body(t, carry):
            for k in range(TOP_K):
                r = pl.multiple_of(ids_ref[0, k, t], rt)
                dst = buf.at[s, pl.ds(pl.multiple_of((k * tm + t) * rt, rt), rt), :]
                pltpu.make_async_copy(y_hbm.at[pl.ds(r, rt), :], dst, sem.at[s]).start()
            return carry

        lax.fori_loop(0, tm, body, 0, unroll=2)

    @pl.when(i == 0)
    def _():
        start_tile(idx_ref, 0)

    @pl.when(i + 1 < pl.num_programs(0))
    def _():
        start_tile(idxn_ref, 1 - slot)

    pltpu.make_async_copy(y_hbm.at[pl.ds(0, rt * TOP_K * tm), :], buf.at[slot], sem.at[slot]).wait()
    acc = x1_ref[...]
    g = g_ref[...]
    for k in range(TOP_K):
        words = jnp.concatenate(
            [buf[slot, pl.ds(rt * k * tm + c, tm, stride=rt), :] for c in range(rt)], axis=1)
        y = jnp.concatenate([lax.bitcast_convert_type(words & jnp.uint32(_HI16), F32),
                             lax.bitcast_convert_type(lax.shift_left(words, jnp.uint32(16)), F32)], axis=1)
        acc = acc + g[:, k:k + 1] * y
    ms = jnp.mean(acc * acc, axis=-1, keepdims=True)
    o_ref[...] = acc * lax.rsqrt(ms + EPS) * nw_ref[...]


def _combine(dest_kt, y_rows, x1, gates, norm_w, tm):
    t, d = x1.shape
    nt = t // tm
    idx3 = (dest_kt * ROW_TILE).reshape(TOP_K, nt, tm).transpose(1, 0, 2)
    kern = functools.partial(_combine_kernel, tm=tm)
    return pl.pallas_call(
        kern,
        grid=(nt,),
        in_specs=[
            pl.BlockSpec((1, TOP_K, tm), lambda i: (i, 0, 0), memory_space=pltpu.SMEM),
            pl.BlockSpec((1, TOP_K, tm), lambda i: (jnp.minimum(i + 1, nt - 1), 0, 0), memory_space=pltpu.SMEM),
            pl.BlockSpec(memory_space=pl.ANY),
            pl.BlockSpec((tm, d), lambda i: (i, 0)),
            pl.BlockSpec((tm, TOP_K), lambda i: (i, 0)),
            pl.BlockSpec((1, d), lambda i: (0, 0)),
        ],
        out_specs=pl.BlockSpec((tm, d), lambda i: (i, 0)),
        out_shape=jax.ShapeDtypeStruct((t, d), F32),
        scratch_shapes=[pltpu.VMEM((2, ROW_TILE * TOP_K * tm, LANES), jnp.uint32), pltpu.SemaphoreType.DMA((2,))],
        compiler_params=_cparams(("arbitrary",)),
        name="moe_combine",
    )(idx3, idx3, y_rows, x1, gates, norm_w)


def _route_tables(ids_ranks, counts, n_experts, bm, n_blocks):
    top_idx = ids_ranks[:, :TOP_K]
    rank = ids_ranks[:, TOP_K:2 * TOP_K]
    cnt = counts[0, :n_experts]
    padded = ((cnt + bm - 1) // bm) * bm
    padded_end = jnp.cumsum(padded)
    padded_start = padded_end - padded
    hot = top_idx[:, :, None] == jnp.arange(n_experts, dtype=jnp.int32)[None, None, :]
    dest = rank + jnp.sum(jnp.where(hot, padded_start[None, None, :], 0), axis=-1)
    n_used = (padded_end[-1] // bm).astype(jnp.int32)
    block_start = jnp.arange(n_blocks, dtype=jnp.int32) * bm
    block_expert = jnp.minimum(jnp.sum(padded_end[None, :] <= block_start[:, None], axis=1), n_experts - 1)
    zero_start = jnp.where(cnt > 0, padded_end - bm, -1).astype(jnp.int32)
    hot_b = block_expert[:, None] == jnp.arange(n_experts, dtype=block_expert.dtype)[None, :]
    group_end = jnp.sum(jnp.where(hot_b, (padded_start + cnt)[None, :], 0), axis=1)
    rows = jnp.clip(group_end - block_start, 0, bm)
    q = bm // MOE_BUCKETS
    block_quarters = jnp.clip((rows + q - 1) // q, 1, MOE_BUCKETS).astype(jnp.int32)
    return (dest.astype(jnp.int32).T, block_expert.astype(jnp.int32), n_used.reshape(1), zero_start,
            block_quarters)


def _pick(n, candidates):
    for c in candidates:
        if n % c == 0:
            return c
    return n


def _layer(x, norm_mix_w, w_in, conv_ssm_w, conv_ssm_b, dt_bias_fwd, dt_bias_bwd, a_log_fwd, a_log_bwd,
           d_skip, ssm_norm_w, conf_dw_w, conf_dw_b, conf_ln_w, conf_ln_b, w_out, norm_ffn_w, w_router,
           b_router, w_gate, b_gate, w_up, b_up, w_down, b_down):
    bsz, seq, d = x.shape
    t = bsz * seq
    n_heads = dt_bias_fwd.shape[0]
    d_ssm = n_heads * HEAD_DIM
    d_xbc = conv_ssm_w.shape[1]
    d_conf = conf_dw_w.shape[1]
    n_experts = w_gate.shape[0]
    row = lambda v: v.reshape(1, -1).astype(F32)

    tn = PROJ_TILE
    c_dt = d_ssm + d_xbc
    c_conf = c_dt + 2 * n_heads
    starts = ([s for s in range(0, d_ssm, tn)] + [c_conf + s for s in range(0, 2 * d_conf, tn)]
              + [d_ssm + s for s in range(0, d_xbc, tn)])
    w_tiles = jnp.stack([w_in[:, s:s + tn] for s in starts]).astype(BF16)
    wf = w_in[:, c_dt:c_dt + n_heads]
    wb = w_in[:, c_dt + n_heads:c_dt + 2 * n_heads]
    zpad = jnp.zeros((d, LANES - 3 * n_heads), F32)
    w_dt = jnp.concatenate([wf, wf, wf, zpad, wb, wb, wb, zpad], axis=1)
    rep3 = lambda v: jnp.concatenate([v, v, v, jnp.zeros((LANES - 3 * n_heads,), F32)])
    x2d = x.reshape(t, d)

    def split2(w):
        hi = w.astype(BF16)
        return hi, (w - hi.astype(F32)).astype(BF16)

    h, dt2, dtt2 = _norm_dt(x2d, row(norm_mix_w), *split2(w_dt), _pick(t, (512, 256, 128)))
    proj4 = _in_proj(h, w_tiles, _pick(t, (2048, 1024, 512, 256, 128))).reshape(-1, bsz, seq, tn)
    tile_conf = d_ssm // tn
    tile_xbc = (d_ssm + 2 * d_conf) // tn

    xbc = _conv_ssm(proj4, conv_ssm_w, row(conv_ssm_b), tile_xbc, _pick(seq, (1024, 512, 256, 128)), 512)
    y_conf = _conformer(proj4, conf_dw_w, row(conf_dw_b), row(conf_ln_w), row(conf_ln_b),
                        tile_conf, tile_conf + d_conf // tn, _pick(seq, (256, 128)))

    e3 = ((jnp.arange(LANES)[:, None] % n_heads == jnp.arange(d_ssm)[None, :] // HEAD_DIM)
          & (jnp.arange(LANES)[:, None] < 3 * n_heads)).astype(BF16)
    prm = {}
    for name, bias, alog in (("f", dt_bias_fwd, a_log_fwd), ("b", dt_bias_bwd, a_log_bwd)):
        b3 = rep3(bias.astype(F32))
        a3 = rep3(alog.astype(F32))
        prm[name] = (b3.reshape(1, LANES), jnp.broadcast_to(b3[:, None], (LANES, CHUNK)),
                     a3.reshape(1, LANES), jnp.broadcast_to(a3[:, None], (LANES, CHUNK)))
    y_f = _ssd(xbc, dt2, dtt2, *prm["f"], e3, reverse=False, n_heads=n_heads)
    dskip_x = jnp.repeat(d_skip.astype(F32), HEAD_DIM).reshape(1, d_ssm)
    y_ssm = _ssd(xbc, dt2, dtt2, *prm["b"], e3, reverse=True, n_heads=n_heads,
                 final_args=(y_f, proj4, dskip_x, row(ssm_norm_w)))

    wr = jnp.zeros((d, LANES), F32).at[:, :n_experts].set(w_router.astype(F32))
    br = jnp.zeros((1, LANES), F32).at[0, :n_experts].set(b_router.astype(F32))
    x1, logits = _out_proj(y_ssm.reshape(t, d_ssm), y_conf.reshape(t, d_conf), w_out.astype(BF16), x2d,
                           row(norm_ffn_w), *split2(wr), br, _pick(t, (512, 256, 128)))

    bm = MOE_ROWS
    n_blocks = -(-(t * TOP_K + n_experts * (bm - 1)) // bm)
    ids_ranks, gates_x, counts = _router(logits, n_experts, _pick(t, (1024, 512, 256, 128)))
    gates = gates_x[:, :TOP_K]
    dest_kt, block_expert, n_used, zero_start, block_quarters = _route_tables(
        ids_ranks, counts, n_experts, bm, n_blocks)
    xs = _dispatch(dest_kt, zero_start, x1, row(norm_ffn_w), n_blocks * bm, _pick(t, (512, 256, 128)), bm)
    act = _moe_up(block_expert, n_used, block_quarters, xs, w_gate, b_gate[:, None, :], w_up, b_up[:, None, :],
                  bm, 1024)
    y_rows = _moe_down(block_expert, n_used, block_quarters, act, w_down, b_down[:, None, :], bm)
    return x1, gates, dest_kt, y_rows


def kernel(x, norm_mix_w, w_in, conv_ssm_w, conv_ssm_b, dt_bias_fwd, dt_bias_bwd, a_log_fwd, a_log_bwd, d_skip,
           ssm_norm_w, conf_dw_w, conf_dw_b, conf_ln_w, conf_ln_b, w_out, norm_ffn_w, w_router, b_router, w_gate,
           b_gate, w_up, b_up, w_down, b_down, norm_final_w):
    assert w_in.shape[0] == 1, "a single layer is supported"
    bsz, seq, d = x.shape
    t = bsz * seq
    x1, gates, dest_kt, y_rows = _layer(
        x, norm_mix_w[0], w_in[0], conv_ssm_w[0], conv_ssm_b[0], dt_bias_fwd[0], dt_bias_bwd[0], a_log_fwd[0],
        a_log_bwd[0], d_skip[0], ssm_norm_w[0], conf_dw_w[0], conf_dw_b[0], conf_ln_w[0], conf_ln_b[0], w_out[0],
        norm_ffn_w[0], w_router[0], b_router[0], w_gate[0], b_gate[0], w_up[0], b_up[0], w_down[0], b_down[0])
    out = _combine(dest_kt, y_rows, x1, gates, norm_final_w.reshape(1, d).astype(F32), _pick(t, (256, 128)))
    return out.reshape(bsz, seq, d)
```

```python
import functools
import math

import jax
import jax.numpy as jnp
from jax import lax
from jax.experimental import pallas as pl
from jax.experimental.pallas import tpu as pltpu

F32 = jnp.float32
BF16 = jnp.bfloat16

EPS = 1e-5
HEAD_DIM = 64
SSM_GROUPS = 4
D_STATE = 128
CHUNK = 128
SSM_CONV = 7
CONF_KERNEL = 31
TOP_K = 4
SWIGLU_LIMIT = 7.0
SWIGLU_ALPHA = 1.702

LANES = 128
SUBLANES = 8
VMEM_LIMIT_BYTES = 56 * 1024 * 1024
HALO = 16
MOE_ROWS = 512
MOE_BUCKETS = 4
ROW_TILE = SUBLANES
PROJ_TILE = 1024

HIGHEST = lax.Precision.HIGHEST


def _cparams(sem):
    return pltpu.CompilerParams(dimension_semantics=sem, vmem_limit_bytes=VMEM_LIMIT_BYTES)


def _sigmoid(v):
    return 1.0 / (1.0 + jnp.exp(-v))


def _silu(v):
    return v * _sigmoid(v)


def _softplus(v):
    return jnp.maximum(v, 0.0) + jnp.log(1.0 + jnp.exp(-jnp.abs(v)))


def _split2(v):
    hi = v.astype(BF16)
    lo = (v - hi.astype(F32)).astype(BF16)
    return hi, lo


def _dot_split(a_hi, a_lo, b_hi_ref, b_lo_ref):
    b_hi = b_hi_ref[...]
    acc = jnp.dot(a_hi, b_hi, preferred_element_type=F32)
    acc = acc + jnp.dot(a_lo, b_hi, preferred_element_type=F32)
    return acc + jnp.dot(a_hi, b_lo_ref[...], preferred_element_type=F32)


def _norm_dt_kernel(x_ref, nw_ref, wdt_hi_ref, wdt_lo_ref, h_ref, dt_ref, dtt_ref):
    x = x_ref[...]
    ms = jnp.mean(x * x, axis=-1, keepdims=True)
    h_hi, h_lo = _split2(x * lax.rsqrt(ms + EPS) * nw_ref[...])
    h_ref[...] = h_hi
    dt = _dot_split(h_hi, h_lo, wdt_hi_ref, wdt_lo_ref)
    dt_ref[...] = dt
    dtt_ref[...] = dt.T


def _norm_dt(x2d, norm_w, w_dt_hi, w_dt_lo, tm):
    t, d = x2d.shape
    n_dt = w_dt_hi.shape[1]
    return pl.pallas_call(
        _norm_dt_kernel,
        grid=(t // tm,),
        in_specs=[
            pl.BlockSpec((tm, d), lambda m: (m, 0)),
            pl.BlockSpec((1, d), lambda m: (0, 0)),
            pl.BlockSpec((d, n_dt), lambda m: (0, 0)),
            pl.BlockSpec((d, n_dt), lambda m: (0, 0)),
        ],
        out_specs=[
            pl.BlockSpec((tm, d), lambda m: (m, 0)),
            pl.BlockSpec((tm, n_dt), lambda m: (m, 0)),
            pl.BlockSpec((n_dt, tm), lambda m: (0, m)),
        ],
        out_shape=[
            jax.ShapeDtypeStruct((t, d), BF16),
            jax.ShapeDtypeStruct((t, n_dt), F32),
            jax.ShapeDtypeStruct((n_dt, t), F32),
        ],
        compiler_params=_cparams(("parallel",)),
        name="norm_dt",
    )(x2d, norm_w, w_dt_hi, w_dt_lo)


def _inproj_kernel(h_ref, w_ref, o_ref):
    o_ref[0] = jnp.dot(h_ref[...], w_ref[0], preferred_element_type=F32).astype(o_ref.dtype)


def _in_proj(h, w_tiles, tm):
    t, d = h.shape
    n_tiles, _, tn = w_tiles.shape
    return pl.pallas_call(
        _inproj_kernel,
        grid=(t // tm, n_tiles),
        in_specs=[
            pl.BlockSpec((tm, d), lambda m, n: (m, 0)),
            pl.BlockSpec((1, d, tn), lambda m, n: (n, 0, 0)),
        ],
        out_specs=pl.BlockSpec((1, tm, tn), lambda m, n: (n, m, 0)),
        out_shape=jax.ShapeDtypeStruct((n_tiles, t, tn), BF16),
        compiler_params=_cparams(("parallel", "arbitrary")),
        name="in_proj",
    )(h, w_tiles)


def _fill_halo_scratch(scr, cur, prev, nxt, lt):
    l = pl.program_id(1)
    last = pl.num_programs(1) - 1
    scr[HALO:HALO + lt, :] = cur
    scr[0:HALO, :] = jnp.where(l > 0, prev, 0.0)
    scr[HALO + lt:HALO + lt + HALO, :] = jnp.where(l < last, nxt, 0.0)


def _conv_ssm_kernel(cur_ref, prev_ref, next_ref, w_ref, b_ref, o_ref, scr, *, lt, rc):
    _fill_halo_scratch(scr, cur_ref[0, 0].astype(F32), prev_ref[0, 0].astype(F32), next_ref[0, 0].astype(F32), lt)
    pad = SSM_CONV // 2
    w = w_ref[...]
    bias = b_ref[...]

    def body(i, carry):
        r0 = pl.multiple_of(i * rc, rc)
        win = scr[pl.ds(r0 + (HALO - SUBLANES), rc + 2 * SUBLANES), :]
        acc = jnp.zeros((rc, scr.shape[1]), F32)
        for k in range(SSM_CONV):
            o = SUBLANES - pad + k
            acc = acc + w[k:k + 1, :] * win[o:o + rc, :]
        o_ref[0, pl.ds(r0, rc), :] = _silu(acc + bias).astype(o_ref.dtype)
        return carry

    lax.fori_loop(0, lt // rc, body, 0)


def _conv_ssm(proj4, conv_w, conv_b, tile0, lt, ct, rc=32):
    _, bsz, seq, tn = proj4.shape
    c = conv_w.shape[1]
    per = tn // ct
    hb = lt // HALO
    n_hb = seq // HALO
    kern = functools.partial(_conv_ssm_kernel, lt=lt, rc=rc)
    return pl.pallas_call(
        kern,
        grid=(bsz, seq // lt, c // ct),
        in_specs=[
            pl.BlockSpec((1, 1, lt, ct), lambda b, l, j: (tile0 + j // per, b, l, j % per)),
            pl.BlockSpec((1, 1, HALO, ct),
                         lambda b, l, j: (tile0 + j // per, b, jnp.maximum(l * hb - 1, 0), j % per)),
            pl.BlockSpec((1, 1, HALO, ct),
                         lambda b, l, j: (tile0 + j // per, b, jnp.minimum((l + 1) * hb, n_hb - 1), j % per)),
            pl.BlockSpec((SSM_CONV, ct), lambda b, l, j: (0, j)),
            pl.BlockSpec((1, ct), lambda b, l, j: (0, j)),
        ],
        out_specs=pl.BlockSpec((1, lt, ct), lambda b, l, j: (b, l, j)),
        out_shape=jax.ShapeDtypeStruct((bsz, seq, c), BF16),
        scratch_shapes=[pltpu.VMEM((lt + 2 * HALO, ct), F32)],
        compiler_params=_cparams(("parallel", "parallel", "parallel")),
        name="conv_ssm",
    )(proj4, proj4, proj4, conv_w, conv_b)


def _conformer_kernel(*refs, lt, rc, lc, n_t):
    tiles = refs[:6 * n_t]
    w_ref, b_ref, lnw_ref, lnb_ref, o_ref, scr, sh_scr, v_scr = refs[6 * n_t:]

    def glu(group):
        a = jnp.concatenate([r[0, 0].astype(F32) for r in tiles[2 * n_t * group:2 * n_t * group + n_t]], axis=1)
        g = jnp.concatenate([r[0, 0].astype(F32) for r in tiles[2 * n_t * group + n_t:2 * n_t * (group + 1)]], axis=1)
        return a * _sigmoid(g)

    _fill_halo_scratch(scr, glu(0), glu(1), glu(2), lt)
    c = scr.shape[1]
    pad = CONF_KERNEL // 2
    n_sh = lt + 2 * HALO - SUBLANES
    for r0 in range(0, n_sh, rc):
        n = min(rc, n_sh - r0)
        for j in range(c // lc):
            win = scr[r0:r0 + n + SUBLANES, j * lc:(j + 1) * lc]
            for m in range(1, SUBLANES):
                sh_scr[m - 1, r0:r0 + n, j * lc:(j + 1) * lc] = win[m:m + n, :]

    def conv_body(i, carry):
        r0 = pl.multiple_of(i * rc, rc)
        for j in range(c // lc):
            lanes = slice(j * lc, (j + 1) * lc)
            acc = jnp.zeros((rc, lc), F32)
            for m in range(SUBLANES):
                offs = [(k, HALO - pad + k - m) for k in range(CONF_KERNEL) if (HALO - pad + k) % SUBLANES == m]
                if not offs:
                    continue
                span = rc + offs[-1][1]
                if m == 0:
                    win = scr[pl.ds(r0, span), lanes]
                else:
                    win = sh_scr[m - 1, pl.ds(r0, span), lanes]
                for k, base in offs:
                    acc = acc + w_ref[k:k + 1, lanes] * win[base:base + rc, :]
            v_scr[pl.ds(r0, rc), lanes] = acc + b_ref[:, lanes]
        return carry

    lax.fori_loop(0, lt // rc, conv_body, 0)
    lnw = lnw_ref[...]
    lnb = lnb_ref[...]
    rn = 2 * SUBLANES

    def norm_body(i, carry):
        r0 = pl.multiple_of(i * rn, rn)
        v = v_scr[pl.ds(r0, rn), :]
        mu = jnp.mean(v, axis=-1, keepdims=True)
        vc = v - mu
        var = jnp.mean(vc * vc, axis=-1, keepdims=True)
        y = vc * lax.rsqrt(var + EPS) * lnw + lnb
        o_ref[0, pl.ds(r0, rn), :] = _silu(y).astype(o_ref.dtype)
        return carry

    lax.fori_loop(0, lt // rn, norm_body, 0, unroll=4)


def _conformer(proj4, dw_w, dw_b, ln_w, ln_b, tile_a, tile_g, lt, rc=64, lc=128):
    _, bsz, seq, tn = proj4.shape
    c = dw_w.shape[1]
    n_t = c // tn
    hb = lt // HALO
    n_hb = seq // HALO
    kern = functools.partial(_conformer_kernel, lt=lt, rc=rc, lc=lc, n_t=n_t)

    def cur(j):
        return pl.BlockSpec((1, 1, lt, tn), lambda b, l: (j, b, l, 0))

    def prev(j):
        return pl.BlockSpec((1, 1, HALO, tn), lambda b, l: (j, b, jnp.maximum(l * hb - 1, 0), 0))

    def nxt(j):
        return pl.BlockSpec((1, 1, HALO, tn), lambda b, l: (j, b, jnp.minimum((l + 1) * hb, n_hb - 1), 0))

    tile_specs = [mk(t0 + j) for mk in (cur, prev, nxt) for t0 in (tile_a, tile_g) for j in range(n_t)]
    vec = pl.BlockSpec((1, c), lambda b, l: (0, 0))
    return pl.pallas_call(
        kern,
        grid=(bsz, seq // lt),
        in_specs=tile_specs + [pl.BlockSpec((CONF_KERNEL, c), lambda b, l: (0, 0)), vec, vec, vec],
        out_specs=pl.BlockSpec((1, lt, c), lambda b, l: (b, l, 0)),
        out_shape=jax.ShapeDtypeStruct((bsz, seq, c), BF16),
        scratch_shapes=[pltpu.VMEM((lt + 2 * HALO, c), F32),
                        pltpu.VMEM((SUBLANES - 1, lt + 2 * HALO, c), F32),
                        pltpu.VMEM((lt, c), F32)],
        compiler_params=_cparams(("parallel", "parallel")),
        name="conformer",
    )(*([proj4] * len(tile_specs)), dw_w, dw_b, ln_w, ln_b)


def _split3(v):
    hi = v.astype(BF16).astype(F32)
    r = v - hi
    mid = r.astype(BF16).astype(F32)
    lo = (r - mid).astype(BF16).astype(F32)
    return hi, mid, lo


def _ssd_kernel(*refs, reverse, final, n_heads, nbs):
    xs_ref, bm_ref, cm_ref, dt_ref = refs[:4]
    dtt_refs = refs[4:4 + nbs]
    consts = refs[4 + nbs:11 + nbs]
    if final:
        yf_ref = refs[11 + nbs]
        z_refs = refs[12 + nbs:-4]
        dsk_ref, nw_ref, o_ref, state_scr = refs[-4:]
    else:
        yf_ref, z_refs, dsk_ref, nw_ref = None, (), None, None
        o_ref, state_scr = refs[-2:]

    @pl.when(pl.program_id(1) == 0)
    def _():
        state_scr[...] = jnp.zeros_like(state_scr)

    for s in range(nbs):
        _ssd_chunk(s, xs_ref, bm_ref, cm_ref, dt_ref, dtt_refs[s], consts, yf_ref, z_refs, dsk_ref, nw_ref, o_ref,
                   state_scr, reverse=reverse, final=final, n_heads=n_heads)


def _ssd_chunk(s, xs_ref, bm_ref, cm_ref, dt_ref, dtt_ref, consts, yf_ref, z_refs, dsk_ref, nw_ref, o_ref,
               state_scr, *, reverse, final, n_heads):
    dtb_ref, dtbt_ref, alog_ref, alogt_ref, tri_ref, trit_ref, e3_ref = consts
    q = CHUNK
    hg = n_heads
    d_ssm = n_heads * HEAD_DIM
    gw = d_ssm // SSM_GROUPS

    lane = lax.broadcasted_iota(jnp.int32, (q, LANES), 1)
    sub = lax.broadcasted_iota(jnp.int32, (LANES, q), 0)

    def sel3_lanes(v):
        hi, mid, lo = _split3(v)
        return jnp.where(lane < hg, hi, jnp.where(lane < 2 * hg, mid, jnp.where(lane < 3 * hg, lo, 0.0))).astype(BF16)

    def sel3_rows(v):
        hi, mid, lo = _split3(v)
        return jnp.where(sub < hg, hi, jnp.where(sub < 2 * hg, mid, jnp.where(sub < 3 * hg, lo, 0.0))).astype(BF16)

    a_row = -jnp.exp(alog_ref[...])
    dt = _softplus(dt_ref[s] + dtb_ref[...])
    da = dt * a_row
    a_col = -jnp.exp(alogt_ref[...])
    dat = _softplus(dtt_ref[...] + dtbt_ref[...]) * a_col

    tri = tri_ref[...]
    cum3 = jnp.dot(tri, sel3_lanes(da), preferred_element_type=F32)
    cum = cum3
    for r in (1, 2, 3):
        cum = cum + pltpu.roll(cum3, r * hg, axis=1)
    cumt3 = jnp.dot(sel3_rows(dat), trit_ref[...], preferred_element_type=F32)
    cumt = cumt3[0:hg] + cumt3[hg:2 * hg] + cumt3[2 * hg:3 * hg]

    tot_row = 0 if reverse else q - 1
    total = cum[tot_row:tot_row + 1, :]

    e3 = e3_ref[...]
    lhs = jnp.concatenate(
        [sel3_lanes(dt), sel3_lanes(jnp.exp(cum)), sel3_lanes(jnp.exp(total - cum)),
         sel3_lanes(jnp.broadcast_to(jnp.exp(total), (q, LANES)))], axis=0)
    ex = jnp.dot(lhs, e3, preferred_element_type=F32)
    dt_x = ex[0:q]
    ecum_x = ex[q:2 * q]
    edte_x = ex[2 * q:3 * q]
    cdec_x = ex[3 * q:3 * q + 1]

    xs = xs_ref[s].astype(F32)
    xdt = xs * dt_x
    xdt_b = xdt.astype(BF16)
    xdte_b = (xdt * edte_x).astype(BF16)
    bm = bm_ref[s]
    cm = cm_ref[s]

    li = lax.broadcasted_iota(jnp.int32, (q, q), 0)
    si = lax.broadcasted_iota(jnp.int32, (q, q), 1)
    mask = (li <= si) if reverse else (li >= si)
    lane_lo = lax.broadcasted_iota(jnp.int32, (q, LANES), 1) < HEAD_DIM

    hpg = n_heads // SSM_GROUPS
    y_parts = []
    for g in range(SSM_GROUPS):
        bg = bm[:, g * D_STATE:(g + 1) * D_STATE]
        cg = cm[:, g * D_STATE:(g + 1) * D_STATE]
        cb = lax.dot_general(cg, bg, (((1,), (1,)), ((), ())), preferred_element_type=F32)
        st = state_scr[s, :, g * gw:(g + 1) * gw]
        y_off = jnp.dot(cg, st.astype(BF16), preferred_element_type=F32) * ecum_x[:, g * gw:(g + 1) * gw]
        diag = []
        for pair in range(hpg // 2):
            h0 = g * hpg + 2 * pair
            ls = []
            for h in (h0, h0 + 1):
                seg = cum[:, h:h + 1] - cumt[h:h + 1, :]
                ls.append((cb * jnp.exp(jnp.where(mask, seg, -jnp.inf))).astype(BF16))
            l2 = jnp.concatenate(ls, axis=1)
            xp = xdt_b[:, h0 * HEAD_DIM:(h0 + 2) * HEAD_DIM]
            zero = jnp.zeros_like(xp)
            r2 = jnp.concatenate([jnp.where(lane_lo, xp, zero), jnp.where(lane_lo, zero, xp)], axis=0)
            diag.append(jnp.dot(l2, r2, preferred_element_type=F32))
        y_parts.append(jnp.concatenate(diag, axis=1) + y_off)
        contrib = lax.dot_general(bg, xdte_b[:, g * gw:(g + 1) * gw], (((0,), (0,)), ((), ())),
                                  preferred_element_type=F32)
        state_scr[s, :, g * gw:(g + 1) * gw] = st * cdec_x[:, g * gw:(g + 1) * gw] + contrib
    y = jnp.concatenate(y_parts, axis=1)

    if not final:
        o_ref[s] = y
    else:
        y = y + yf_ref[s] + dsk_ref[...] * xs
        y = y * _silu(jnp.concatenate([r[0, s].astype(F32) for r in z_refs], axis=1))
        outs = []
        for g in range(SSM_GROUPS):
            yg = y[:, g * gw:(g + 1) * gw]
            ms = jnp.mean(yg * yg, axis=-1, keepdims=True)
            outs.append(yg * lax.rsqrt(ms + EPS))
        o_ref[s] = (jnp.concatenate(outs, axis=1) * nw_ref[...]).astype(o_ref.dtype)


def _ssd(xbc3, dt2, dtt2, dtb, dtbt, alog, alogt, e3, *, reverse, n_heads, final_args=None):
    bsz, seq, _ = xbc3.shape
    d_ssm = n_heads * HEAD_DIM
    gn = SSM_GROUPS * D_STATE
    nc = seq // CHUNK
    q = CHUNK
    d = 1 if reverse else 0
    final = final_args is not None

    def cidx(c):
        return (nc - 1 - c) if reverse else c

    li = jnp.arange(q)[:, None]
    ji = jnp.arange(q)[None, :]
    tri = ((ji >= li) if reverse else (ji <= li)).astype(BF16)
    trit = tri.T

    nbs = 2 if bsz % 2 == 0 else 1
    dt3 = dt2.reshape(bsz, seq, dt2.shape[1])
    const = lambda shape: pl.BlockSpec(shape, lambda b, c: (0,) * len(shape))
    in_specs = [
        pl.BlockSpec((nbs, q, d_ssm), lambda b, c: (b, cidx(c), 0)),
        pl.BlockSpec((nbs, q, gn), lambda b, c: (b, cidx(c), d_ssm // gn)),
        pl.BlockSpec((nbs, q, gn), lambda b, c: (b, cidx(c), d_ssm // gn + 1)),
        pl.BlockSpec((nbs, q, LANES), lambda b, c: (b, cidx(c), d)),
    ]
    in_specs += [pl.BlockSpec((LANES, q), lambda b, c, s=s: (d, (b * nbs + s) * nc + cidx(c))) for s in range(nbs)]
    in_specs += [const((1, LANES)), const((LANES, q)), const((1, LANES)), const((LANES, q)),
                 const((q, q)), const((q, q)), const((LANES, d_ssm))]
    args = [xbc3, xbc3, xbc3, dt3] + [dtt2] * nbs + [dtb, dtbt, alog, alogt, tri, trit, e3]
    if final:
        yf3, proj4, dskip_x, norm_w = final_args
        tn = proj4.shape[3]
        z_specs = [pl.BlockSpec((1, nbs, q, tn), lambda b, c, j=j: (j, b, cidx(c), 0)) for j in range(d_ssm // tn)]
        in_specs += [pl.BlockSpec((nbs, q, d_ssm), lambda b, c: (b, cidx(c), 0))] + z_specs
        in_specs += [const((1, d_ssm)), const((1, d_ssm))]
        args += [yf3] + [proj4] * len(z_specs) + [dskip_x, norm_w]
    kern = functools.partial(_ssd_kernel, reverse=reverse, final=final, n_heads=n_heads, nbs=nbs)
    return pl.pallas_call(
        kern,
        grid=(bsz // nbs, nc),
        in_specs=in_specs,
        out_specs=pl.BlockSpec((nbs, q, d_ssm), lambda b, c: (b, cidx(c), 0)),
        out_shape=jax.ShapeDtypeStruct((bsz, seq, d_ssm), BF16 if final else F32),
        scratch_shapes=[pltpu.VMEM((nbs, D_STATE, d_ssm), F32)],
        compiler_params=_cparams(("parallel", "arbitrary")),
        name="ssd_bwd" if reverse else "ssd_fwd",
    )(*args)


def _outproj_kernel(ys_ref, yc_ref, wa_ref, wb_ref, x_ref, nw_ref, wr_hi_ref, wr_lo_ref, br_ref, x1_ref, lg_ref):
    acc = jnp.dot(ys_ref[...], wa_ref[...], preferred_element_type=F32)
    acc = acc + jnp.dot(yc_ref[...], wb_ref[...], preferred_element_type=F32)
    x1 = x_ref[...] + acc
    x1_ref[...] = x1
    ms = jnp.mean(x1 * x1, axis=-1, keepdims=True)
    h_hi, h_lo = _split2(x1 * lax.rsqrt(ms + EPS) * nw_ref[...])
    lg_ref[...] = _dot_split(h_hi, h_lo, wr_hi_ref, wr_lo_ref) + br_ref[...]


def _out_proj(y_ssm, y_conf, w_out, x2d, norm_w, wr_hi, wr_lo, b_router, tm):
    t, d = x2d.shape
    ka = y_ssm.shape[1]
    ne = wr_hi.shape[1]
    resident = lambda shape, idx: pl.BlockSpec(shape, lambda m: idx, pipeline_mode=pl.Buffered(1))
    return pl.pallas_call(
        _outproj_kernel,
        grid=(t // tm,),
        in_specs=[
            pl.BlockSpec((tm, ka), lambda m: (m, 0)),
            pl.BlockSpec((tm, ka), lambda m: (m, 0)),
            resident((ka, d), (0, 0)),
            resident((ka, d), (1, 0)),
            pl.BlockSpec((tm, d), lambda m: (m, 0)),
            pl.BlockSpec((1, d), lambda m: (0, 0)),
            resident((d, ne), (0, 0)),
            resident((d, ne), (0, 0)),
            pl.BlockSpec((1, ne), lambda m: (0, 0)),
        ],
        out_specs=[
            pl.BlockSpec((tm, d), lambda m: (m, 0)),
            pl.BlockSpec((tm, ne), lambda m: (m, 0)),
        ],
        out_shape=[
            jax.ShapeDtypeStruct((t, d), F32),
            jax.ShapeDtypeStruct((t, ne), F32),
        ],
        compiler_params=_cparams(("parallel",)),
        name="out_proj",
    )(y_ssm, y_conf, w_out, w_out, x2d, norm_w, wr_hi, wr_lo, b_router)


def _router_kernel(lg_ref, tri_ref, ir_ref, gate_ref, cnt_ref, run_scr, *, n_experts):
    @pl.when(pl.program_id(0) == 0)
    def _():
        run_scr[...] = jnp.zeros_like(run_scr)

    tr = lg_ref.shape[0]
    lane = lax.broadcasted_iota(jnp.int32, (tr, LANES), 1)
    lane_f = lane.astype(F32)
    lg = jnp.where(lane < n_experts, lg_ref[...], -jnp.inf)
    vals, ids, hots = [], [], []
    for _ in range(TOP_K):
        m = jnp.max(lg, axis=1, keepdims=True)
        idx = jnp.min(jnp.where(lg == m, lane_f, float(LANES)), axis=1, keepdims=True)
        hot = lane_f == idx
        vals.append(m)
        ids.append(idx)
        hots.append(hot)
        lg = jnp.where(hot, -jnp.inf, lg)
    es = [jnp.exp(v - vals[0]) for v in vals]
    den = es[0]
    for e in es[1:]:
        den = den + e
    hot_any = hots[0].astype(F32)
    for hot in hots[1:]:
        hot_any = hot_any + hot.astype(F32)
    before = jnp.dot(tri_ref[...], hot_any.astype(BF16), preferred_element_type=F32) + run_scr[...]
    run_scr[...] = run_scr[...] + jnp.sum(hot_any, axis=0, keepdims=True)
    cnt_ref[...] = run_scr[...].astype(jnp.int32)
    out_i = jnp.zeros((tr, LANES), F32)
    out_g = jnp.zeros((tr, LANES), F32)
    for k in range(TOP_K):
        rank = jnp.sum(jnp.where(hots[k], before, 0.0), axis=1, keepdims=True)
        out_i = jnp.where(lane == k, ids[k], out_i)
        out_i = jnp.where(lane == TOP_K + k, rank, out_i)
        out_g = jnp.where(lane == k, es[k] / den, out_g)
    ir_ref[...] = out_i.astype(jnp.int32)
    gate_ref[...] = out_g


def _router(logits, n_experts, tr):
    t = logits.shape[0]
    tri = (jnp.arange(tr)[None, :] < jnp.arange(tr)[:, None]).astype(BF16)
    kern = functools.partial(_router_kernel, n_experts=n_experts)
    return pl.pallas_call(
        kern,
        grid=(t // tr,),
        in_specs=[pl.BlockSpec((tr, LANES), lambda i: (i, 0)), pl.BlockSpec((tr, tr), lambda i: (0, 0))],
        out_specs=[pl.BlockSpec((tr, LANES), lambda i: (i, 0)), pl.BlockSpec((tr, LANES), lambda i: (i, 0)),
                   pl.BlockSpec((1, LANES), lambda i: (0, 0))],
        out_shape=[jax.ShapeDtypeStruct((t, LANES), jnp.int32), jax.ShapeDtypeStruct((t, LANES), F32),
                   jax.ShapeDtypeStruct((1, LANES), jnp.int32)],
        scratch_shapes=[pltpu.VMEM((1, LANES), F32)],
        compiler_params=_cparams(("arbitrary",)),
        name="moe_router",
    )(logits, tri)


_HI16 = 0xFFFF0000


def _pack_bf16_pair(a, b):
    au = lax.bitcast_convert_type(a.astype(BF16).astype(F32), jnp.uint32)
    bu = lax.bitcast_convert_type(b.astype(BF16).astype(F32), jnp.uint32)
    return (au & jnp.uint32(_HI16)) | lax.shift_right_logical(bu, jnp.uint32(16))


def _unpack_bf16_pair(w):
    a = lax.bitcast_convert_type(w & jnp.uint32(_HI16), F32).astype(BF16)
    b = lax.bitcast_convert_type(lax.shift_left(w, jnp.uint32(16)), F32).astype(BF16)
    return a, b


def _dispatch_kernel(zs_ref, idx_ref, x1_ref, nw_ref, xs_hbm, zero_scr, pk, sem, zsem, *, tm, bm, n_experts):
    i = pl.program_id(0)
    last = pl.num_programs(0) - 1
    slot = lax.rem(i, 2)
    half = x1_ref.shape[1] // 2
    rt = ROW_TILE

    def zero_copy(e):
        dst = xs_hbm.at[pl.ds(pl.multiple_of(zs_ref[e], rt * bm), rt * bm), :]
        return pltpu.make_async_copy(zero_scr, dst, zsem)

    @pl.when(i == 0)
    def _():
        zero_scr[...] = jnp.zeros_like(zero_scr)
        for e in range(n_experts):
            @pl.when(zs_ref[e] >= 0)
            def _():
                zero_copy(e).start()
        for e in range(n_experts):
            @pl.when(zs_ref[e] >= 0)
            def _():
                zero_copy(e).wait()

    def wait_rows(s):
        rows = xs_hbm.at[pl.ds(0, rt * TOP_K * tm), :]
        pltpu.make_async_copy(rows, rows, sem.at[s]).wait()

    @pl.when(i >= 2)
    def _():
        wait_rows(slot)

    x = x1_ref[...]
    ms = jnp.mean(x * x, axis=-1, keepdims=True)
    h = x * lax.rsqrt(ms + EPS) * nw_ref[...]
    packed = _pack_bf16_pair(h[:, :half], h[:, half:])
    for c in range(rt):
        pk[slot, pl.ds(c, tm, stride=rt), :] = packed[:, c * LANES:(c + 1) * LANES]

    def body(t, carry):
        src = pk.at[slot, pl.ds(pl.multiple_of(t * rt, rt), rt), :]
        for k in range(TOP_K):
            r = pl.multiple_of(idx_ref[0, k, t], rt)
            pltpu.make_async_copy(src, xs_hbm.at[pl.ds(r, rt), :], sem.at[slot]).start()
        return carry

    lax.fori_loop(0, tm, body, 0, unroll=2)

    @pl.when(i == last)
    def _():
        wait_rows(slot)

    @pl.when(jnp.logical_and(i == last, i >= 1))
    def _():
        wait_rows(1 - slot)


def _dispatch(dest_kt, zero_start, x1, norm_w, n_rows, tm, bm):
    t, d = x1.shape
    assert d // 2 == ROW_TILE * LANES
    nt = t // tm
    n_experts = zero_start.shape[0]
    idx3 = (dest_kt * ROW_TILE).reshape(TOP_K, nt, tm).transpose(1, 0, 2)
    zero_start = zero_start * ROW_TILE
    kern = functools.partial(_dispatch_kernel, tm=tm, bm=bm, n_experts=n_experts)
    return pl.pallas_call(
        kern,
        grid_spec=pltpu.PrefetchScalarGridSpec(
            num_scalar_prefetch=1,
            grid=(nt,),
            in_specs=[
                pl.BlockSpec((1, TOP_K, tm), lambda i, zs: (i, 0, 0), memory_space=pltpu.SMEM),
                pl.BlockSpec((tm, d), lambda i, zs: (i, 0)),
                pl.BlockSpec((1, d), lambda i, zs: (0, 0)),
            ],
            out_specs=pl.BlockSpec(memory_space=pl.ANY),
            scratch_shapes=[pltpu.VMEM((ROW_TILE * bm, LANES), jnp.uint32),
                            pltpu.VMEM((2, ROW_TILE * tm, LANES), jnp.uint32),
                            pltpu.SemaphoreType.DMA((2,)), pltpu.SemaphoreType.DMA(())],
        ),
        out_shape=jax.ShapeDtypeStruct((ROW_TILE * n_rows, LANES), jnp.uint32),
        compiler_params=_cparams(("arbitrary",)),
        name="moe_dispatch",
    )(zero_start, idx3, x1, norm_w)


def _for_row_bucket(b, nu_ref, bq_ref, bm, compute):
    for quarters in range(1, MOE_BUCKETS + 1):
        @pl.when(jnp.logical_and(b < nu_ref[0], bq_ref[b] == quarters))
        def _():
            compute(quarters * (bm // MOE_BUCKETS))


def _expert_weights(b, tile, n_tiles, tabs, copies):
    be_ref, nu_ref, first_ref, run_ref, next_ref, nruns_ref = tabs
    run = run_ref[b]
    slot = lax.rem(tile * nruns_ref[0] + run, 2)
    first = jnp.logical_and(b < nu_ref[0], first_ref[b] == 1)
    nxt = next_ref[b]

    @pl.when(jnp.logical_and(first, jnp.logical_and(tile == 0, run == 0)))
    def _():
        for c in copies(be_ref[b], tile, slot):
            c.start()

    @pl.when(first)
    def _():
        for c in copies(be_ref[b], tile, slot):
            c.wait()

    @pl.when(jnp.logical_and(first, nxt >= 0))
    def _():
        for c in copies(nxt, tile, 1 - slot):
            c.start()

    @pl.when(jnp.logical_and(first, jnp.logical_and(nxt < 0, tile + 1 < n_tiles)))
    def _():
        for c in copies(be_ref[0], tile + 1, 1 - slot):
            c.start()

    return slot


def _moe_up_kernel(be_ref, nu_ref, bq_ref, first_ref, run_ref, next_ref, nruns_ref, x_ref, wg_hbm, bg_ref, wu_hbm,
                   bu_ref, o_ref, wbuf, wsem):
    bm = x_ref.shape[0] // ROW_TILE
    tn = o_ref.shape[1]

    def copies(e, tile, s):
        cols = pl.ds(pl.multiple_of(tile * tn, tn), tn)
        return [pltpu.make_async_copy(wg_hbm.at[e, :, cols], wbuf.at[s, 0], wsem.at[s, 0]),
                pltpu.make_async_copy(wu_hbm.at[e, :, cols], wbuf.at[s, 1], wsem.at[s, 1])]

    slot = _expert_weights(pl.program_id(1), pl.program_id(0), pl.num_programs(0),
                           (be_ref, nu_ref, first_ref, run_ref, next_ref, nruns_ref), copies)

    def compute(nv):
        words = jnp.concatenate([x_ref[pl.ds(c, nv, stride=ROW_TILE), :] for c in range(ROW_TILE)], axis=1)
        xa, xb = _unpack_bf16_pair(words)
        half = xa.shape[1]
        wg = wbuf[slot, 0].astype(BF16)
        wu = wbuf[slot, 1].astype(BF16)
        gate = (jnp.dot(xa, wg[:half], preferred_element_type=F32)
                + jnp.dot(xb, wg[half:], preferred_element_type=F32) + bg_ref[0])
        up = (jnp.dot(xa, wu[:half], preferred_element_type=F32)
              + jnp.dot(xb, wu[half:], preferred_element_type=F32) + bu_ref[0])
        gate = jnp.minimum(gate, SWIGLU_LIMIT)
        up = jnp.clip(up, -SWIGLU_LIMIT, SWIGLU_LIMIT)
        act = (up + 1.0) * gate * _sigmoid(SWIGLU_ALPHA * gate)
        o_ref[:nv, :] = act.astype(o_ref.dtype)

    _for_row_bucket(pl.program_id(1), nu_ref, bq_ref, bm, compute)


def _moe_down_kernel(be_ref, nu_ref, bq_ref, first_ref, run_ref, next_ref, nruns_ref, a_ref, wd_hbm, bd_ref, o_ref,
                     wbuf, wsem):
    def copies(e, tile, s):
        return [pltpu.make_async_copy(wd_hbm.at[e], wbuf.at[s], wsem.at[s])]

    slot = _expert_weights(pl.program_id(0), 0, 1, (be_ref, nu_ref, first_ref, run_ref, next_ref, nruns_ref), copies)

    def compute(nv):
        y = jnp.dot(a_ref[:nv, :], wbuf[slot].astype(BF16), preferred_element_type=F32) + bd_ref[0]
        d = y.shape[1]
        packed = _pack_bf16_pair(y[:, :d // 2], y[:, d // 2:])
        for c in range(ROW_TILE):
            o_ref[pl.ds(c, nv, stride=ROW_TILE), :] = packed[:, c * LANES:(c + 1) * LANES]

    _for_row_bucket(pl.program_id(0), nu_ref, bq_ref, a_ref.shape[0], compute)


def _blk(b, nu_ref):
    return jnp.minimum(b, nu_ref[0] - 1)


def _moe_up(tables, xs, w_gate, b_gate3, w_up, b_up3, bm, tn):
    n_rows = xs.shape[0] // ROW_TILE
    d, dff = w_gate.shape[1], w_gate.shape[2]
    nb = n_rows // bm
    hbm = pl.BlockSpec(memory_space=pl.ANY)
    bspec = pl.BlockSpec((1, 1, tn), lambda n, b, be, nu, *_: (be[_blk(b, nu)], 0, n))
    return pl.pallas_call(
        _moe_up_kernel,
        grid_spec=pltpu.PrefetchScalarGridSpec(
            num_scalar_prefetch=len(tables),
            grid=(dff // tn, nb),
            in_specs=[pl.BlockSpec((ROW_TILE * bm, LANES), lambda n, b, be, nu, *_: (_blk(b, nu), 0)),
                      hbm, bspec, hbm, bspec],
            out_specs=pl.BlockSpec((bm, tn), lambda n, b, be, nu, *_: (_blk(b, nu), n)),
            scratch_shapes=[pltpu.VMEM((2, 2, d, tn), F32), pltpu.SemaphoreType.DMA((2, 2))],
        ),
        out_shape=jax.ShapeDtypeStruct((n_rows, dff), BF16),
        compiler_params=_cparams(("arbitrary", "arbitrary")),
        name="moe_up",
    )(*tables, xs, w_gate, b_gate3, w_up, b_up3)


def _moe_down(tables, act, w_down, b_down3, bm):
    n_rows, dff = act.shape
    d = w_down.shape[2]
    assert d // 2 == ROW_TILE * LANES
    nb = n_rows // bm
    return pl.pallas_call(
        _moe_down_kernel,
        grid_spec=pltpu.PrefetchScalarGridSpec(
            num_scalar_prefetch=len(tables),
            grid=(nb,),
            in_specs=[
                pl.BlockSpec((bm, dff), lambda b, be, nu, *_: (_blk(b, nu), 0)),
                pl.BlockSpec(memory_space=pl.ANY),
                pl.BlockSpec((1, 1, d), lambda b, be, nu, *_: (be[_blk(b, nu)], 0, 0)),
            ],
            out_specs=pl.BlockSpec((ROW_TILE * bm, LANES), lambda b, be, nu, *_: (_blk(b, nu), 0)),
            scratch_shapes=[pltpu.VMEM((2, dff, d), F32), pltpu.SemaphoreType.DMA((2,))],
        ),
        out_shape=jax.ShapeDtypeStruct((ROW_TILE * n_rows, LANES), jnp.uint32),
        compiler_params=_cparams(("arbitrary",)),
        name="moe_down",
    )(*tables, act, w_down, b_down3)


def _combine_kernel(idx_ref, idxn_ref, y_hbm, x1_ref, g_ref, nw_ref, o_ref, buf, sem, *, tm):
    i = pl.program_id(0)
    slot = lax.rem(i, 2)
    rt = ROW_TILE

    def start_tile(ids_ref, s):
        def body(t, carry):
            for k in range(TOP_K):
                r = pl.multiple_of(ids_ref[0, k, t], rt)
                dst = buf.at[s, pl.ds(pl.multiple_of((k * tm + t) * rt, rt), rt), :]
                pltpu.make_async_copy(y_hbm.at[pl.ds(r, rt), :], dst, sem.at[s]).start()
            return carry

        lax.fori_loop(0, tm, body, 0, unroll=2)

    @pl.when(i == 0)
    def _():
        start_tile(idx_ref, 0)

    @pl.when(i + 1 < pl.num_programs(0))
    def _():
        start_tile(idxn_ref, 1 - slot)

    pltpu.make_async_copy(y_hbm.at[pl.ds(0, rt * TOP_K * tm), :], buf.at[slot], sem.at[slot]).wait()
    acc = x1_ref[...]
    g = g_ref[...]
    for k in range(TOP_K):
        words = jnp.concatenate(
            [buf[slot, pl.ds(rt * k * tm + c, tm, stride=rt), :] for c in range(rt)], axis=1)
        y = jnp.concatenate([lax.bitcast_convert_type(words & jnp.uint32(_HI16), F32),
                             lax.bitcast_convert_type(lax.shift_left(words, jnp.uint32(16)), F32)], axis=1)
        acc = acc + g[:, k:k + 1] * y
    ms = jnp.mean(acc * acc, axis=-1, keepdims=True)
    o_ref[...] = acc * lax.rsqrt(ms + EPS) * nw_ref[...]


def _combine(dest_kt, y_rows, x1, gates, norm_w, tm):
    t, d = x1.shape
    nt = t // tm
    idx3 = (dest_kt * ROW_TILE).reshape(TOP_K, nt, tm).transpose(1, 0, 2)
    kern = functools.partial(_combine_kernel, tm=tm)
    return pl.pallas_call(
        kern,
        grid=(nt,),
        in_specs=[
            pl.BlockSpec((1, TOP_K, tm), lambda i: (i, 0, 0), memory_space=pltpu.SMEM),
            pl.BlockSpec((1, TOP_K, tm), lambda i: (jnp.minimum(i + 1, nt - 1), 0, 0), memory_space=pltpu.SMEM),
            pl.BlockSpec(memory_space=pl.ANY),
            pl.BlockSpec((tm, d), lambda i: (i, 0)),
            pl.BlockSpec((tm, TOP_K), lambda i: (i, 0)),
            pl.BlockSpec((1, d), lambda i: (0, 0)),
        ],
        out_specs=pl.BlockSpec((tm, d), lambda i: (i, 0)),
        out_shape=jax.ShapeDtypeStruct((t, d), F32),
        scratch_shapes=[pltpu.VMEM((2, ROW_TILE * TOP_K * tm, LANES), jnp.uint32), pltpu.SemaphoreType.DMA((2,))],
        compiler_params=_cparams(("arbitrary",)),
        name="moe_combine",
    )(idx3, idx3, y_rows, x1, gates, norm_w)


def _route_tables(ids_ranks, counts, n_experts, bm, n_blocks):
    top_idx = ids_ranks[:, :TOP_K]
    rank = ids_ranks[:, TOP_K:2 * TOP_K]
    cnt = counts[0, :n_experts]
    padded = ((cnt + bm - 1) // bm) * bm
    padded_end = jnp.cumsum(padded)
    padded_start = padded_end - padded
    hot = top_idx[:, :, None] == jnp.arange(n_experts, dtype=jnp.int32)[None, None, :]
    dest = rank + jnp.sum(jnp.where(hot, padded_start[None, None, :], 0), axis=-1)
    n_used = (padded_end[-1] // bm).astype(jnp.int32)
    block_start = jnp.arange(n_blocks, dtype=jnp.int32) * bm
    block_expert = jnp.minimum(jnp.sum(padded_end[None, :] <= block_start[:, None], axis=1), n_experts - 1)
    zero_start = jnp.where(cnt > 0, padded_end - bm, -1).astype(jnp.int32)
    hot_b = block_expert[:, None] == jnp.arange(n_experts, dtype=block_expert.dtype)[None, :]
    group_end = jnp.sum(jnp.where(hot_b, (padded_start + cnt)[None, :], 0), axis=1)
    rows = jnp.clip(group_end - block_start, 0, bm)
    q = bm // MOE_BUCKETS
    block_quarters = jnp.clip((rows + q - 1) // q, 1, MOE_BUCKETS).astype(jnp.int32)
    block_expert = block_expert.astype(jnp.int32)
    ar = jnp.arange(n_blocks, dtype=jnp.int32)
    prev_e = jnp.concatenate([jnp.full((1,), -1, jnp.int32), block_expert[:-1]])
    first = jnp.logical_and(ar < n_used, block_expert != prev_e)
    run_id = (jnp.cumsum(first.astype(jnp.int32)) - 1).astype(jnp.int32)
    cand = jnp.where(first, ar, n_blocks)
    later = jnp.concatenate([lax.cummin(cand[::-1])[::-1][1:], jnp.full((1,), n_blocks, jnp.int32)])
    run_next = jnp.where(later < n_blocks, block_expert[jnp.minimum(later, n_blocks - 1)], -1).astype(jnp.int32)
    tables = (block_expert, n_used.reshape(1), block_quarters, first.astype(jnp.int32), run_id, run_next,
              jnp.sum(first.astype(jnp.int32)).reshape(1))
    return dest.astype(jnp.int32).T, zero_start, tables


def _pick(n, candidates):
    for c in candidates:
        if n % c == 0:
            return c
    return n


def _layer(x, norm_mix_w, w_in, conv_ssm_w, conv_ssm_b, dt_bias_fwd, dt_bias_bwd, a_log_fwd, a_log_bwd,
           d_skip, ssm_norm_w, conf_dw_w, conf_dw_b, conf_ln_w, conf_ln_b, w_out, norm_ffn_w, w_router,
           b_router, w_gate, b_gate, w_up, b_up, w_down, b_down):
    bsz, seq, d = x.shape
    t = bsz * seq
    n_heads = dt_bias_fwd.shape[0]
    d_ssm = n_heads * HEAD_DIM
    d_xbc = conv_ssm_w.shape[1]
    d_conf = conf_dw_w.shape[1]
    n_experts = w_gate.shape[0]
    row = lambda v: v.reshape(1, -1).astype(F32)

    tn = PROJ_TILE
    c_dt = d_ssm + d_xbc
    c_conf = c_dt + 2 * n_heads
    starts = ([s for s in range(0, d_ssm, tn)] + [c_conf + s for s in range(0, 2 * d_conf, tn)]
              + [d_ssm + s for s in range(0, d_xbc, tn)])
    w_tiles = jnp.stack([w_in[:, s:s + tn] for s in starts]).astype(BF16)
    wf = w_in[:, c_dt:c_dt + n_heads]
    wb = w_in[:, c_dt + n_heads:c_dt + 2 * n_heads]
    zpad = jnp.zeros((d, LANES - 3 * n_heads), F32)
    w_dt = jnp.concatenate([wf, wf, wf, zpad, wb, wb, wb, zpad], axis=1)
    rep3 = lambda v: jnp.concatenate([v, v, v, jnp.zeros((LANES - 3 * n_heads,), F32)])
    x2d = x.reshape(t, d)

    def split2(w):
        hi = w.astype(BF16)
        return hi, (w - hi.astype(F32)).astype(BF16)

    h, dt2, dtt2 = _norm_dt(x2d, row(norm_mix_w), *split2(w_dt), _pick(t, (512, 256, 128)))
    proj4 = _in_proj(h, w_tiles, _pick(t, (2048, 1024, 512, 256, 128))).reshape(-1, bsz, seq, tn)
    tile_conf = d_ssm // tn
    tile_xbc = (d_ssm + 2 * d_conf) // tn

    xbc = _conv_ssm(proj4, conv_ssm_w, row(conv_ssm_b), tile_xbc, _pick(seq, (1024, 512, 256, 128)), 512)
    y_conf = _conformer(proj4, conf_dw_w, row(conf_dw_b), row(conf_ln_w), row(conf_ln_b),
                        tile_conf, tile_conf + d_conf // tn, _pick(seq, (256, 128)))

    e3 = ((jnp.arange(LANES)[:, None] % n_heads == jnp.arange(d_ssm)[None, :] // HEAD_DIM)
          & (jnp.arange(LANES)[:, None] < 3 * n_heads)).astype(BF16)
    prm = {}
    for name, bias, alog in (("f", dt_bias_fwd, a_log_fwd), ("b", dt_bias_bwd, a_log_bwd)):
        b3 = rep3(bias.astype(F32))
        a3 = rep3(alog.astype(F32))
        prm[name] = (b3.reshape(1, LANES), jnp.broadcast_to(b3[:, None], (LANES, CHUNK)),
                     a3.reshape(1, LANES), jnp.broadcast_to(a3[:, None], (LANES, CHUNK)))
    y_f = _ssd(xbc, dt2, dtt2, *prm["f"], e3, reverse=False, n_heads=n_heads)
    dskip_x = jnp.repeat(d_skip.astype(F32), HEAD_DIM).reshape(1, d_ssm)
    y_ssm = _ssd(xbc, dt2, dtt2, *prm["b"], e3, reverse=True, n_heads=n_heads,
                 final_args=(y_f, proj4, dskip_x, row(ssm_norm_w)))

    wr = jnp.zeros((d, LANES), F32).at[:, :n_experts].set(w_router.astype(F32))
    br = jnp.zeros((1, LANES), F32).at[0, :n_experts].set(b_router.astype(F32))
    x1, logits = _out_proj(y_ssm.reshape(t, d_ssm), y_conf.reshape(t, d_conf), w_out.astype(BF16), x2d,
                           row(norm_ffn_w), *split2(wr), br, _pick(t, (512, 256, 128)))

    bm = MOE_ROWS
    n_blocks = -(-(t * TOP_K + n_experts * (bm - 1)) // bm)
    ids_ranks, gates_x, counts = _router(logits, n_experts, _pick(t, (1024, 512, 256, 128)))
    gates = gates_x[:, :TOP_K]
    dest_kt, zero_start, tables = _route_tables(ids_ranks, counts, n_experts, bm, n_blocks)
    xs = _dispatch(dest_kt, zero_start, x1, row(norm_ffn_w), n_blocks * bm, _pick(t, (512, 256, 128)), bm)
    act = _moe_up(tables, xs, w_gate, b_gate[:, None, :], w_up, b_up[:, None, :], bm, 1024)
    y_rows = _moe_down(tables, act, w_down, b_down[:, None, :], bm)
    return x1, gates, dest_kt, y_rows


def kernel(x, norm_mix_w, w_in, conv_ssm_w, conv_ssm_b, dt_bias_fwd, dt_bias_bwd, a_log_fwd, a_log_bwd, d_skip,
           ssm_norm_w, conf_dw_w, conf_dw_b, conf_ln_w, conf_ln_b, w_out, norm_ffn_w, w_router, b_router, w_gate,
           b_gate, w_up, b_up, w_down, b_down, norm_final_w):
    assert w_in.shape[0] == 1, "a single layer is supported"
    bsz, seq, d = x.shape
    t = bsz * seq
    x1, gates, dest_kt, y_rows = _layer(
        x, norm_mix_w[0], w_in[0], conv_ssm_w[0], conv_ssm_b[0], dt_bias_fwd[0], dt_bias_bwd[0], a_log_fwd[0],
        a_log_bwd[0], d_skip[0], ssm_norm_w[0], conf_dw_w[0], conf_dw_b[0], conf_ln_w[0], conf_ln_b[0], w_out[0],
        norm_ffn_w[0], w_router[0], b_router[0], w_gate[0], b_gate[0], w_up[0], b_up[0], w_down[0], b_down[0])
    out = _combine(dest_kt, y_rows, x1, gates, norm_final_w.reshape(1, d).astype(F32), _pick(t, (256, 128)))
    return out.reshape(bsz, seq, d)
```

```python
import functools

import jax
import jax.numpy as jnp
from jax import lax
from jax.experimental import pallas as pl
from jax.experimental.pallas import tpu as pltpu

F32 = jnp.float32
BF16 = jnp.bfloat16

EPS = 1e-5
HEAD_DIM = 64
SSM_GROUPS = 4
D_STATE = 128
CHUNK = 128
SSM_CONV = 7
CONF_KERNEL = 31
TOP_K = 4
SWIGLU_LIMIT = 7.0
SWIGLU_ALPHA = 1.702

LANES = 128
SUBLANES = 8
VMEM_LIMIT_BYTES = 56 * 1024 * 1024
HALO = 16
MOE_ROWS = 512
MOE_BUCKETS = 4
ROW_TILE = SUBLANES
PROJ_TILE = 1024


def _cparams(sem):
    return pltpu.CompilerParams(dimension_semantics=sem, vmem_limit_bytes=VMEM_LIMIT_BYTES)


def _sigmoid(v):
    return 1.0 / (1.0 + jnp.exp(-v))


def _silu(v):
    return v * _sigmoid(v)


def _softplus(v):
    return jnp.maximum(v, 0.0) + jnp.log(1.0 + jnp.exp(-jnp.abs(v)))


def _split2(v):
    hi = v.astype(BF16)
    lo = (v - hi.astype(F32)).astype(BF16)
    return hi, lo


def _dot_split(a_hi, a_lo, b_hi_ref, b_lo_ref):
    b_hi = b_hi_ref[...]
    acc = jnp.dot(a_hi, b_hi, preferred_element_type=F32)
    acc = acc + jnp.dot(a_lo, b_hi, preferred_element_type=F32)
    return acc + jnp.dot(a_hi, b_lo_ref[...], preferred_element_type=F32)


def _norm_dt_kernel(x_ref, nw_ref, wdt_hi_ref, wdt_lo_ref, h_ref, dt_ref, dtt_ref):
    x = x_ref[...]
    ms = jnp.mean(x * x, axis=-1, keepdims=True)
    h_hi, h_lo = _split2(x * lax.rsqrt(ms + EPS) * nw_ref[...])
    h_ref[...] = h_hi
    dt = _dot_split(h_hi, h_lo, wdt_hi_ref, wdt_lo_ref)
    dt_ref[...] = dt
    dtt_ref[...] = dt.T


def _norm_dt(x2d, norm_w, w_dt_hi, w_dt_lo, tm):
    t, d = x2d.shape
    n_dt = w_dt_hi.shape[1]
    return pl.pallas_call(
        _norm_dt_kernel,
        grid=(t // tm,),
        in_specs=[
            pl.BlockSpec((tm, d), lambda m: (m, 0)),
            pl.BlockSpec((1, d), lambda m: (0, 0)),
            pl.BlockSpec((d, n_dt), lambda m: (0, 0)),
            pl.BlockSpec((d, n_dt), lambda m: (0, 0)),
        ],
        out_specs=[
            pl.BlockSpec((tm, d), lambda m: (m, 0)),
            pl.BlockSpec((tm, n_dt), lambda m: (m, 0)),
            pl.BlockSpec((n_dt, tm), lambda m: (0, m)),
        ],
        out_shape=[
            jax.ShapeDtypeStruct((t, d), BF16),
            jax.ShapeDtypeStruct((t, n_dt), F32),
            jax.ShapeDtypeStruct((n_dt, t), F32),
        ],
        compiler_params=_cparams(("parallel",)),
        name="norm_dt",
    )(x2d, norm_w, w_dt_hi, w_dt_lo)


def _inproj_kernel(h_ref, w_ref, o_ref):
    o_ref[0] = jnp.dot(h_ref[...], w_ref[0], preferred_element_type=F32).astype(o_ref.dtype)


def _in_proj(h, w_tiles, tm):
    t, d = h.shape
    n_tiles, _, tn = w_tiles.shape
    return pl.pallas_call(
        _inproj_kernel,
        grid=(t // tm, n_tiles),
        in_specs=[
            pl.BlockSpec((tm, d), lambda m, n: (m, 0)),
            pl.BlockSpec((1, d, tn), lambda m, n: (n, 0, 0)),
        ],
        out_specs=pl.BlockSpec((1, tm, tn), lambda m, n: (n, m, 0)),
        out_shape=jax.ShapeDtypeStruct((n_tiles, t, tn), BF16),
        compiler_params=_cparams(("parallel", "arbitrary")),
        name="in_proj",
    )(h, w_tiles)


def _fill_halo_scratch(scr, cur, prev, nxt, lt):
    l = pl.program_id(1)
    last = pl.num_programs(1) - 1
    scr[HALO:HALO + lt, :] = cur
    scr[0:HALO, :] = jnp.where(l > 0, prev, 0.0)
    scr[HALO + lt:HALO + lt + HALO, :] = jnp.where(l < last, nxt, 0.0)


def _conv_ssm_kernel(cur_ref, prev_ref, next_ref, w_ref, b_ref, o_ref, scr, *, lt, rc):
    _fill_halo_scratch(scr, cur_ref[0, 0].astype(F32), prev_ref[0, 0].astype(F32), next_ref[0, 0].astype(F32), lt)
    pad = SSM_CONV // 2
    w = w_ref[...]
    bias = b_ref[...]

    def body(i, carry):
        r0 = pl.multiple_of(i * rc, rc)
        win = scr[pl.ds(r0 + (HALO - SUBLANES), rc + 2 * SUBLANES), :]
        acc = jnp.zeros((rc, scr.shape[1]), F32)
        for k in range(SSM_CONV):
            o = SUBLANES - pad + k
            acc = acc + w[k:k + 1, :] * win[o:o + rc, :]
        o_ref[0, pl.ds(r0, rc), :] = _silu(acc + bias).astype(o_ref.dtype)
        return carry

    lax.fori_loop(0, lt // rc, body, 0)


def _conv_ssm(proj4, conv_w, conv_b, tile0, lt, ct, rc=32):
    _, bsz, seq, tn = proj4.shape
    c = conv_w.shape[1]
    per = tn // ct
    hb = lt // HALO
    n_hb = seq // HALO
    kern = functools.partial(_conv_ssm_kernel, lt=lt, rc=rc)
    return pl.pallas_call(
        kern,
        grid=(bsz, seq // lt, c // ct),
        in_specs=[
            pl.BlockSpec((1, 1, lt, ct), lambda b, l, j: (tile0 + j // per, b, l, j % per)),
            pl.BlockSpec((1, 1, HALO, ct),
                         lambda b, l, j: (tile0 + j // per, b, jnp.maximum(l * hb - 1, 0), j % per)),
            pl.BlockSpec((1, 1, HALO, ct),
                         lambda b, l, j: (tile0 + j // per, b, jnp.minimum((l + 1) * hb, n_hb - 1), j % per)),
            pl.BlockSpec((SSM_CONV, ct), lambda b, l, j: (0, j)),
            pl.BlockSpec((1, ct), lambda b, l, j: (0, j)),
        ],
        out_specs=pl.BlockSpec((1, lt, ct), lambda b, l, j: (b, l, j)),
        out_shape=jax.ShapeDtypeStruct((bsz, seq, c), BF16),
        scratch_shapes=[pltpu.VMEM((lt + 2 * HALO, ct), F32)],
        compiler_params=_cparams(("parallel", "parallel", "parallel")),
        name="conv_ssm",
    )(proj4, proj4, proj4, conv_w, conv_b)


def _conformer_kernel(*refs, lt, rc, lc, n_t):
    tiles = refs[:6 * n_t]
    w_ref, b_ref, lnw_ref, lnb_ref, o_ref, scr, sh_scr, v_scr = refs[6 * n_t:]

    def glu(group):
        a = jnp.concatenate([r[0, 0].astype(F32) for r in tiles[2 * n_t * group:2 * n_t * group + n_t]], axis=1)
        g = jnp.concatenate([r[0, 0].astype(F32) for r in tiles[2 * n_t * group + n_t:2 * n_t * (group + 1)]], axis=1)
        return a * _sigmoid(g)

    _fill_halo_scratch(scr, glu(0), glu(1), glu(2), lt)
    c = scr.shape[1]
    pad = CONF_KERNEL // 2
    n_sh = lt + 2 * HALO - SUBLANES
    for r0 in range(0, n_sh, rc):
        n = min(rc, n_sh - r0)
        for j in range(c // lc):
            win = scr[r0:r0 + n + SUBLANES, j * lc:(j + 1) * lc]
            for m in range(1, SUBLANES):
                sh_scr[m - 1, r0:r0 + n, j * lc:(j + 1) * lc] = win[m:m + n, :]

    def conv_body(i, carry):
        r0 = pl.multiple_of(i * rc, rc)
        for j in range(c // lc):
            lanes = slice(j * lc, (j + 1) * lc)
            acc = jnp.zeros((rc, lc), F32)
            for m in range(SUBLANES):
                offs = [(k, HALO - pad + k - m) for k in range(CONF_KERNEL) if (HALO - pad + k) % SUBLANES == m]
                if not offs:
                    continue
                span = rc + offs[-1][1]
                if m == 0:
                    win = scr[pl.ds(r0, span), lanes]
                else:
                    win = sh_scr[m - 1, pl.ds(r0, span), lanes]
                for k, base in offs:
                    acc = acc + w_ref[k:k + 1, lanes] * win[base:base + rc, :]
            v_scr[pl.ds(r0, rc), lanes] = acc + b_ref[:, lanes]
        return carry

    lax.fori_loop(0, lt // rc, conv_body, 0)
    lnw = lnw_ref[...]
    lnb = lnb_ref[...]
    rn = 2 * SUBLANES

    def norm_body(i, carry):
        r0 = pl.multiple_of(i * rn, rn)
        v = v_scr[pl.ds(r0, rn), :]
        mu = jnp.mean(v, axis=-1, keepdims=True)
        vc = v - mu
        var = jnp.mean(vc * vc, axis=-1, keepdims=True)
        y = vc * lax.rsqrt(var + EPS) * lnw + lnb
        o_ref[0, pl.ds(r0, rn), :] = _silu(y).astype(o_ref.dtype)
        return carry

    lax.fori_loop(0, lt // rn, norm_body, 0, unroll=4)


def _conformer(proj4, dw_w, dw_b, ln_w, ln_b, tile_a, tile_g, lt, rc=64, lc=128):
    _, bsz, seq, tn = proj4.shape
    c = dw_w.shape[1]
    n_t = c // tn
    hb = lt // HALO
    n_hb = seq // HALO
    kern = functools.partial(_conformer_kernel, lt=lt, rc=rc, lc=lc, n_t=n_t)

    def cur(j):
        return pl.BlockSpec((1, 1, lt, tn), lambda b, l: (j, b, l, 0))

    def prev(j):
        return pl.BlockSpec((1, 1, HALO, tn), lambda b, l: (j, b, jnp.maximum(l * hb - 1, 0), 0))

    def nxt(j):
        return pl.BlockSpec((1, 1, HALO, tn), lambda b, l: (j, b, jnp.minimum((l + 1) * hb, n_hb - 1), 0))

    tile_specs = [mk(t0 + j) for mk in (cur, prev, nxt) for t0 in (tile_a, tile_g) for j in range(n_t)]
    vec = pl.BlockSpec((1, c), lambda b, l: (0, 0))
    return pl.pallas_call(
        kern,
        grid=(bsz, seq // lt),
        in_specs=tile_specs + [pl.BlockSpec((CONF_KERNEL, c), lambda b, l: (0, 0)), vec, vec, vec],
        out_specs=pl.BlockSpec((1, lt, c), lambda b, l: (b, l, 0)),
        out_shape=jax.ShapeDtypeStruct((bsz, seq, c), BF16),
        scratch_shapes=[pltpu.VMEM((lt + 2 * HALO, c), F32),
                        pltpu.VMEM((SUBLANES - 1, lt + 2 * HALO, c), F32),
                        pltpu.VMEM((lt, c), F32)],
        compiler_params=_cparams(("parallel", "parallel")),
        name="conformer",
    )(*([proj4] * len(tile_specs)), dw_w, dw_b, ln_w, ln_b)


def _split3(v):
    hi = v.astype(BF16).astype(F32)
    r = v - hi
    mid = r.astype(BF16).astype(F32)
    lo = (r - mid).astype(BF16).astype(F32)
    return hi, mid, lo


def _ssd_kernel(*refs, reverse, final, n_heads, nbs):
    xs_ref, bm_ref, cm_ref, dt_ref = refs[:4]
    dtt_refs = refs[4:4 + nbs]
    consts = refs[4 + nbs:11 + nbs]
    if final:
        yf_ref = refs[11 + nbs]
        z_refs = refs[12 + nbs:-4]
        dsk_ref, nw_ref, o_ref, state_scr = refs[-4:]
    else:
        yf_ref, z_refs, dsk_ref, nw_ref = None, (), None, None
        o_ref, state_scr = refs[-2:]

    @pl.when(pl.program_id(1) == 0)
    def _():
        state_scr[...] = jnp.zeros_like(state_scr)

    for s in range(nbs):
        _ssd_chunk(s, xs_ref, bm_ref, cm_ref, dt_ref, dtt_refs[s], consts, yf_ref, z_refs, dsk_ref, nw_ref, o_ref,
                   state_scr, reverse=reverse, final=final, n_heads=n_heads)


def _ssd_chunk(s, xs_ref, bm_ref, cm_ref, dt_ref, dtt_ref, consts, yf_ref, z_refs, dsk_ref, nw_ref, o_ref,
               state_scr, *, reverse, final, n_heads):
    dtb_ref, dtbt_ref, alog_ref, alogt_ref, tri_ref, trit_ref, e3_ref = consts
    q = CHUNK
    hg = n_heads
    d_ssm = n_heads * HEAD_DIM
    gw = d_ssm // SSM_GROUPS

    lane = lax.broadcasted_iota(jnp.int32, (q, LANES), 1)
    sub = lax.broadcasted_iota(jnp.int32, (LANES, q), 0)

    def sel3_lanes(v):
        hi, mid, lo = _split3(v)
        return jnp.where(lane < hg, hi, jnp.where(lane < 2 * hg, mid, jnp.where(lane < 3 * hg, lo, 0.0))).astype(BF16)

    def sel3_rows(v):
        hi, mid, lo = _split3(v)
        return jnp.where(sub < hg, hi, jnp.where(sub < 2 * hg, mid, jnp.where(sub < 3 * hg, lo, 0.0))).astype(BF16)

    a_row = -jnp.exp(alog_ref[...])
    dt = _softplus(dt_ref[s] + dtb_ref[...])
    da = dt * a_row
    a_col = -jnp.exp(alogt_ref[...])
    dat = _softplus(dtt_ref[...] + dtbt_ref[...]) * a_col

    tri = tri_ref[...]
    cum3 = jnp.dot(tri, sel3_lanes(da), preferred_element_type=F32)
    cum = cum3
    for r in (1, 2, 3):
        cum = cum + pltpu.roll(cum3, r * hg, axis=1)
    cumt3 = jnp.dot(sel3_rows(dat), trit_ref[...], preferred_element_type=F32)
    cumt = cumt3[0:hg] + cumt3[hg:2 * hg] + cumt3[2 * hg:3 * hg]

    tot_row = 0 if reverse else q - 1
    total = cum[tot_row:tot_row + 1, :]

    e3 = e3_ref[...]
    lhs = jnp.concatenate(
        [sel3_lanes(dt), sel3_lanes(jnp.exp(cum)), sel3_lanes(jnp.exp(total - cum)),
         sel3_lanes(jnp.broadcast_to(jnp.exp(total), (q, LANES)))], axis=0)
    ex = jnp.dot(lhs, e3, preferred_element_type=F32)
    dt_x = ex[0:q]
    ecum_x = ex[q:2 * q]
    edte_x = ex[2 * q:3 * q]
    cdec_x = ex[3 * q:3 * q + 1]

    xs = xs_ref[s].astype(F32)
    xdt = xs * dt_x
    xdt_b = xdt.astype(BF16)
    xdte_b = (xdt * edte_x).astype(BF16)
    bm = bm_ref[s]
    cm = cm_ref[s]

    li = lax.broadcasted_iota(jnp.int32, (q, q), 0)
    si = lax.broadcasted_iota(jnp.int32, (q, q), 1)
    mask = (li <= si) if reverse else (li >= si)
    lane_lo = lax.broadcasted_iota(jnp.int32, (q, LANES), 1) < HEAD_DIM

    hpg = n_heads // SSM_GROUPS
    y_parts = []
    for g in range(SSM_GROUPS):
        bg = bm[:, g * D_STATE:(g + 1) * D_STATE]
        cg = cm[:, g * D_STATE:(g + 1) * D_STATE]
        cb = lax.dot_general(cg, bg, (((1,), (1,)), ((), ())), preferred_element_type=F32)
        st = state_scr[s, :, g * gw:(g + 1) * gw]
        y_off = jnp.dot(cg, st.astype(BF16), preferred_element_type=F32) * ecum_x[:, g * gw:(g + 1) * gw]
        diag = []
        for pair in range(hpg // 2):
            h0 = g * hpg + 2 * pair
            ls = []
            for h in (h0, h0 + 1):
                seg = cum[:, h:h + 1] - cumt[h:h + 1, :]
                ls.append((cb * jnp.exp(jnp.where(mask, seg, -jnp.inf))).astype(BF16))
            l2 = jnp.concatenate(ls, axis=1)
            xp = xdt_b[:, h0 * HEAD_DIM:(h0 + 2) * HEAD_DIM]
            zero = jnp.zeros_like(xp)
            r2 = jnp.concatenate([jnp.where(lane_lo, xp, zero), jnp.where(lane_lo, zero, xp)], axis=0)
            diag.append(jnp.dot(l2, r2, preferred_element_type=F32))
        y_parts.append(jnp.concatenate(diag, axis=1) + y_off)
        contrib = lax.dot_general(bg, xdte_b[:, g * gw:(g + 1) * gw], (((0,), (0,)), ((), ())),
                                  preferred_element_type=F32)
        state_scr[s, :, g * gw:(g + 1) * gw] = st * cdec_x[:, g * gw:(g + 1) * gw] + contrib
    y = jnp.concatenate(y_parts, axis=1)

    if not final:
        o_ref[s] = y
    else:
        y = y + yf_ref[s] + dsk_ref[...] * xs
        y = y * _silu(jnp.concatenate([r[0, s].astype(F32) for r in z_refs], axis=1))
        outs = []
        for g in range(SSM_GROUPS):
            yg = y[:, g * gw:(g + 1) * gw]
            ms = jnp.mean(yg * yg, axis=-1, keepdims=True)
            outs.append(yg * lax.rsqrt(ms + EPS))
        o_ref[s] = (jnp.concatenate(outs, axis=1) * nw_ref[...]).astype(o_ref.dtype)


def _ssd(xbc3, dt2, dtt2, dtb, dtbt, alog, alogt, e3, *, reverse, n_heads, final_args=None):
    bsz, seq, _ = xbc3.shape
    d_ssm = n_heads * HEAD_DIM
    gn = SSM_GROUPS * D_STATE
    nc = seq // CHUNK
    q = CHUNK
    d = 1 if reverse else 0
    final = final_args is not None

    def cidx(c):
        return (nc - 1 - c) if reverse else c

    li = jnp.arange(q)[:, None]
    ji = jnp.arange(q)[None, :]
    tri = ((ji >= li) if reverse else (ji <= li)).astype(BF16)
    trit = tri.T

    nbs = 2 if bsz % 2 == 0 else 1
    dt3 = dt2.reshape(bsz, seq, dt2.shape[1])
    const = lambda shape: pl.BlockSpec(shape, lambda b, c: (0,) * len(shape))
    in_specs = [
        pl.BlockSpec((nbs, q, d_ssm), lambda b, c: (b, cidx(c), 0)),
        pl.BlockSpec((nbs, q, gn), lambda b, c: (b, cidx(c), d_ssm // gn)),
        pl.BlockSpec((nbs, q, gn), lambda b, c: (b, cidx(c), d_ssm // gn + 1)),
        pl.BlockSpec((nbs, q, LANES), lambda b, c: (b, cidx(c), d)),
    ]
    in_specs += [pl.BlockSpec((LANES, q), lambda b, c, s=s: (d, (b * nbs + s) * nc + cidx(c))) for s in range(nbs)]
    in_specs += [const((1, LANES)), const((LANES, q)), const((1, LANES)), const((LANES, q)),
                 const((q, q)), const((q, q)), const((LANES, d_ssm))]
    args = [xbc3, xbc3, xbc3, dt3] + [dtt2] * nbs + [dtb, dtbt, alog, alogt, tri, trit, e3]
    if final:
        yf3, proj4, dskip_x, norm_w = final_args
        tn = proj4.shape[3]
        z_specs = [pl.BlockSpec((1, nbs, q, tn), lambda b, c, j=j: (j, b, cidx(c), 0)) for j in range(d_ssm // tn)]
        in_specs += [pl.BlockSpec((nbs, q, d_ssm), lambda b, c: (b, cidx(c), 0))] + z_specs
        in_specs += [const((1, d_ssm)), const((1, d_ssm))]
        args += [yf3] + [proj4] * len(z_specs) + [dskip_x, norm_w]
    kern = functools.partial(_ssd_kernel, reverse=reverse, final=final, n_heads=n_heads, nbs=nbs)
    return pl.pallas_call(
        kern,
        grid=(bsz // nbs, nc),
        in_specs=in_specs,
        out_specs=pl.BlockSpec((nbs, q, d_ssm), lambda b, c: (b, cidx(c), 0)),
        out_shape=jax.ShapeDtypeStruct((bsz, seq, d_ssm), BF16 if final else F32),
        scratch_shapes=[pltpu.VMEM((nbs, D_STATE, d_ssm), F32)],
        compiler_params=_cparams(("parallel", "arbitrary")),
        name="ssd_bwd" if reverse else "ssd_fwd",
    )(*args)


def _outproj_kernel(ys_ref, yc_ref, wa_ref, wb_ref, x_ref, nw_ref, wr_hi_ref, wr_lo_ref, br_ref, x1_ref, lg_ref):
    acc = jnp.dot(ys_ref[...], wa_ref[...], preferred_element_type=F32)
    acc = acc + jnp.dot(yc_ref[...], wb_ref[...], preferred_element_type=F32)
    x1 = x_ref[...] + acc
    x1_ref[...] = x1
    ms = jnp.mean(x1 * x1, axis=-1, keepdims=True)
    h_hi, h_lo = _split2(x1 * lax.rsqrt(ms + EPS) * nw_ref[...])
    lg_ref[...] = _dot_split(h_hi, h_lo, wr_hi_ref, wr_lo_ref) + br_ref[...]


def _out_proj(y_ssm, y_conf, w_out, x2d, norm_w, wr_hi, wr_lo, b_router, tm):
    t, d = x2d.shape
    ka = y_ssm.shape[1]
    ne = wr_hi.shape[1]
    resident = lambda shape, idx: pl.BlockSpec(shape, lambda m: idx, pipeline_mode=pl.Buffered(1))
    return pl.pallas_call(
        _outproj_kernel,
        grid=(t // tm,),
        in_specs=[
            pl.BlockSpec((tm, ka), lambda m: (m, 0)),
            pl.BlockSpec((tm, ka), lambda m: (m, 0)),
            resident((ka, d), (0, 0)),
            resident((ka, d), (1, 0)),
            pl.BlockSpec((tm, d), lambda m: (m, 0)),
            pl.BlockSpec((1, d), lambda m: (0, 0)),
            resident((d, ne), (0, 0)),
            resident((d, ne), (0, 0)),
            pl.BlockSpec((1, ne), lambda m: (0, 0)),
        ],
        out_specs=[
            pl.BlockSpec((tm, d), lambda m: (m, 0)),
            pl.BlockSpec((tm, ne), lambda m: (m, 0)),
        ],
        out_shape=[
            jax.ShapeDtypeStruct((t, d), F32),
            jax.ShapeDtypeStruct((t, ne), F32),
        ],
        compiler_params=_cparams(("parallel",)),
        name="out_proj",
    )(y_ssm, y_conf, w_out, w_out, x2d, norm_w, wr_hi, wr_lo, b_router)


def _router_kernel(lg_ref, tri_ref, ir_ref, gate_ref, cnt_ref, run_scr, *, n_experts):
    @pl.when(pl.program_id(0) == 0)
    def _():
        run_scr[...] = jnp.zeros_like(run_scr)

    tr = lg_ref.shape[0]
    lane = lax.broadcasted_iota(jnp.int32, (tr, LANES), 1)
    lane_f = lane.astype(F32)
    lg = jnp.where(lane < n_experts, lg_ref[...], -jnp.inf)
    vals, ids, hots = [], [], []
    for _ in range(TOP_K):
        m = jnp.max(lg, axis=1, keepdims=True)
        idx = jnp.min(jnp.where(lg == m, lane_f, float(LANES)), axis=1, keepdims=True)
        hot = lane_f == idx
        vals.append(m)
        ids.append(idx)
        hots.append(hot)
        lg = jnp.where(hot, -jnp.inf, lg)
    es = [jnp.exp(v - vals[0]) for v in vals]
    den = es[0]
    for e in es[1:]:
        den = den + e
    hot_any = hots[0].astype(F32)
    for hot in hots[1:]:
        hot_any = hot_any + hot.astype(F32)
    before = jnp.dot(tri_ref[...], hot_any.astype(BF16), preferred_element_type=F32) + run_scr[...]
    run_scr[...] = run_scr[...] + jnp.sum(hot_any, axis=0, keepdims=True)
    cnt_ref[...] = run_scr[...].astype(jnp.int32)
    out_i = jnp.zeros((tr, LANES), F32)
    out_g = jnp.zeros((tr, LANES), F32)
    for k in range(TOP_K):
        rank = jnp.sum(jnp.where(hots[k], before, 0.0), axis=1, keepdims=True)
        out_i = jnp.where(lane == k, ids[k], out_i)
        out_i = jnp.where(lane == TOP_K + k, rank, out_i)
        out_g = jnp.where(lane == k, es[k] / den, out_g)
    ir_ref[...] = out_i.astype(jnp.int32)
    gate_ref[...] = out_g


def _router(logits, n_experts, tr):
    t = logits.shape[0]
    tri = (jnp.arange(tr)[None, :] < jnp.arange(tr)[:, None]).astype(BF16)
    kern = functools.partial(_router_kernel, n_experts=n_experts)
    return pl.pallas_call(
        kern,
        grid=(t // tr,),
        in_specs=[pl.BlockSpec((tr, LANES), lambda i: (i, 0)), pl.BlockSpec((tr, tr), lambda i: (0, 0))],
        out_specs=[pl.BlockSpec((tr, LANES), lambda i: (i, 0)), pl.BlockSpec((tr, LANES), lambda i: (i, 0)),
                   pl.BlockSpec((1, LANES), lambda i: (0, 0))],
        out_shape=[jax.ShapeDtypeStruct((t, LANES), jnp.int32), jax.ShapeDtypeStruct((t, LANES), F32),
                   jax.ShapeDtypeStruct((1, LANES), jnp.int32)],
        scratch_shapes=[pltpu.VMEM((1, LANES), F32)],
        compiler_params=_cparams(("arbitrary",)),
        name="moe_router",
    )(logits, tri)


_HI16 = 0xFFFF0000


def _pack_bf16_pair(a, b):
    au = lax.bitcast_convert_type(a.astype(BF16).astype(F32), jnp.uint32)
    bu = lax.bitcast_convert_type(b.astype(BF16).astype(F32), jnp.uint32)
    return (au & jnp.uint32(_HI16)) | lax.shift_right_logical(bu, jnp.uint32(16))


def _unpack_bf16_pair(w):
    a = lax.bitcast_convert_type(w & jnp.uint32(_HI16), F32).astype(BF16)
    b = lax.bitcast_convert_type(lax.shift_left(w, jnp.uint32(16)), F32).astype(BF16)
    return a, b


def _dispatch_kernel(zs_ref, idx_ref, x1_ref, nw_ref, xs_hbm, zero_scr, pk, sem, zsem, *, tm, bm, n_experts):
    i = pl.program_id(0)
    last = pl.num_programs(0) - 1
    slot = lax.rem(i, 2)
    half = x1_ref.shape[1] // 2
    rt = ROW_TILE

    def zero_copy(e):
        dst = xs_hbm.at[pl.ds(pl.multiple_of(zs_ref[e], rt * bm), rt * bm), :]
        return pltpu.make_async_copy(zero_scr, dst, zsem)

    @pl.when(i == 0)
    def _():
        zero_scr[...] = jnp.zeros_like(zero_scr)
        for e in range(n_experts):
            @pl.when(zs_ref[e] >= 0)
            def _():
                zero_copy(e).start()
        for e in range(n_experts):
            @pl.when(zs_ref[e] >= 0)
            def _():
                zero_copy(e).wait()

    def wait_rows(s):
        rows = xs_hbm.at[pl.ds(0, rt * TOP_K * tm), :]
        pltpu.make_async_copy(rows, rows, sem.at[s]).wait()

    @pl.when(i >= 2)
    def _():
        wait_rows(slot)

    x = x1_ref[...]
    ms = jnp.mean(x * x, axis=-1, keepdims=True)
    h = x * lax.rsqrt(ms + EPS) * nw_ref[...]
    packed = _pack_bf16_pair(h[:, :half], h[:, half:])
    for c in range(rt):
        pk[slot, pl.ds(c, tm, stride=rt), :] = packed[:, c * LANES:(c + 1) * LANES]

    def body(t, carry):
        src = pk.at[slot, pl.ds(pl.multiple_of(t * rt, rt), rt), :]
        for k in range(TOP_K):
            r = pl.multiple_of(idx_ref[0, 0, t * TOP_K + k], rt)
            pltpu.make_async_copy(src, xs_hbm.at[pl.ds(r, rt), :], sem.at[slot]).start()
        return carry

    lax.fori_loop(0, tm, body, 0, unroll=4)

    @pl.when(i == last)
    def _():
        wait_rows(slot)

    @pl.when(jnp.logical_and(i == last, i >= 1))
    def _():
        wait_rows(1 - slot)


def _dispatch(dest, zero_start, x1, norm_w, n_rows, tm, bm):
    t, d = x1.shape
    assert d // 2 == ROW_TILE * LANES
    nt = t // tm
    n_experts = zero_start.shape[0]
    idx3 = (dest * ROW_TILE).reshape(nt, 1, tm * TOP_K)
    zero_start = zero_start * ROW_TILE
    kern = functools.partial(_dispatch_kernel, tm=tm, bm=bm, n_experts=n_experts)
    return pl.pallas_call(
        kern,
        grid_spec=pltpu.PrefetchScalarGridSpec(
            num_scalar_prefetch=1,
            grid=(nt,),
            in_specs=[
                pl.BlockSpec((1, 1, TOP_K * tm), lambda i, zs: (i, 0, 0), memory_space=pltpu.SMEM),
                pl.BlockSpec((tm, d), lambda i, zs: (i, 0)),
                pl.BlockSpec((1, d), lambda i, zs: (0, 0)),
            ],
            out_specs=pl.BlockSpec(memory_space=pl.ANY),
            scratch_shapes=[pltpu.VMEM((ROW_TILE * bm, LANES), jnp.uint32),
                            pltpu.VMEM((2, ROW_TILE * tm, LANES), jnp.uint32),
                            pltpu.SemaphoreType.DMA((2,)), pltpu.SemaphoreType.DMA(())],
        ),
        out_shape=jax.ShapeDtypeStruct((ROW_TILE * n_rows, LANES), jnp.uint32),
        compiler_params=_cparams(("arbitrary",)),
        name="moe_dispatch",
    )(zero_start, idx3, x1, norm_w)


def _for_row_bucket(b, nu_ref, bq_ref, bm, compute):
    for quarters in range(1, MOE_BUCKETS + 1):
        @pl.when(jnp.logical_and(b < nu_ref[0], bq_ref[b] == quarters))
        def _():
            compute(quarters * (bm // MOE_BUCKETS))


def _expert_weights(b, tile, n_tiles, tabs, copies):
    be_ref, nu_ref, first_ref, run_ref, next_ref, nruns_ref = tabs
    run = run_ref[b]
    slot = lax.rem(tile * nruns_ref[0] + run, 2)
    first = jnp.logical_and(b < nu_ref[0], first_ref[b] == 1)
    nxt = next_ref[b]

    @pl.when(jnp.logical_and(first, jnp.logical_and(tile == 0, run == 0)))
    def _():
        for c in copies(be_ref[b], tile, slot):
            c.start()

    @pl.when(first)
    def _():
        for c in copies(be_ref[b], tile, slot):
            c.wait()

    @pl.when(jnp.logical_and(first, nxt >= 0))
    def _():
        for c in copies(nxt, tile, 1 - slot):
            c.start()

    @pl.when(jnp.logical_and(first, jnp.logical_and(nxt < 0, tile + 1 < n_tiles)))
    def _():
        for c in copies(be_ref[0], tile + 1, 1 - slot):
            c.start()

    return slot


def _moe_up_kernel(be_ref, nu_ref, bq_ref, first_ref, run_ref, next_ref, nruns_ref, x_ref, wg_hbm, bg_ref, wu_hbm,
                   bu_ref, o_ref, wbuf, wsem):
    bm = x_ref.shape[0] // ROW_TILE
    tn = o_ref.shape[1]

    def copies(e, tile, s):
        cols = pl.ds(pl.multiple_of(tile * tn, tn), tn)
        return [pltpu.make_async_copy(wg_hbm.at[e, :, cols], wbuf.at[s, 0], wsem.at[s, 0]),
                pltpu.make_async_copy(wu_hbm.at[e, :, cols], wbuf.at[s, 1], wsem.at[s, 1])]

    slot = _expert_weights(pl.program_id(1), pl.program_id(0), pl.num_programs(0),
                           (be_ref, nu_ref, first_ref, run_ref, next_ref, nruns_ref), copies)

    def compute(nv):
        words = jnp.concatenate([x_ref[pl.ds(c, nv, stride=ROW_TILE), :] for c in range(ROW_TILE)], axis=1)
        xa, xb = _unpack_bf16_pair(words)
        half = xa.shape[1]
        wg = wbuf[slot, 0].astype(BF16)
        wu = wbuf[slot, 1].astype(BF16)
        gate = (jnp.dot(xa, wg[:half], preferred_element_type=F32)
                + jnp.dot(xb, wg[half:], preferred_element_type=F32) + bg_ref[0])
        up = (jnp.dot(xa, wu[:half], preferred_element_type=F32)
              + jnp.dot(xb, wu[half:], preferred_element_type=F32) + bu_ref[0])
        gate = jnp.minimum(gate, SWIGLU_LIMIT)
        up = jnp.clip(up, -SWIGLU_LIMIT, SWIGLU_LIMIT)
        act = (up + 1.0) * gate * _sigmoid(SWIGLU_ALPHA * gate)
        o_ref[:nv, :] = act.astype(o_ref.dtype)

    _for_row_bucket(pl.program_id(1), nu_ref, bq_ref, bm, compute)


def _moe_down_kernel(be_ref, nu_ref, bq_ref, first_ref, run_ref, next_ref, nruns_ref, a_ref, wd_hbm, bd_ref, o_ref,
                     wbuf, wsem):
    def copies(e, tile, s):
        return [pltpu.make_async_copy(wd_hbm.at[e], wbuf.at[s], wsem.at[s])]

    slot = _expert_weights(pl.program_id(0), 0, 1, (be_ref, nu_ref, first_ref, run_ref, next_ref, nruns_ref), copies)

    def compute(nv):
        y = jnp.dot(a_ref[:nv, :], wbuf[slot].astype(BF16), preferred_element_type=F32) + bd_ref[0]
        d = y.shape[1]
        packed = _pack_bf16_pair(y[:, :d // 2], y[:, d // 2:])
        for c in range(ROW_TILE):
            o_ref[pl.ds(c, nv, stride=ROW_TILE), :] = packed[:, c * LANES:(c + 1) * LANES]

    _for_row_bucket(pl.program_id(0), nu_ref, bq_ref, a_ref.shape[0], compute)


def _blk(b, nu_ref):
    return jnp.minimum(b, nu_ref[0] - 1)


def _moe_up(tables, xs, w_gate, b_gate3, w_up, b_up3, bm, tn):
    n_rows = xs.shape[0] // ROW_TILE
    d, dff = w_gate.shape[1], w_gate.shape[2]
    nb = n_rows // bm
    hbm = pl.BlockSpec(memory_space=pl.ANY)
    bspec = pl.BlockSpec((1, 1, tn), lambda n, b, be, nu, *_: (be[_blk(b, nu)], 0, n))
    return pl.pallas_call(
        _moe_up_kernel,
        grid_spec=pltpu.PrefetchScalarGridSpec(
            num_scalar_prefetch=len(tables),
            grid=(dff // tn, nb),
            in_specs=[pl.BlockSpec((ROW_TILE * bm, LANES), lambda n, b, be, nu, *_: (_blk(b, nu), 0)),
                      hbm, bspec, hbm, bspec],
            out_specs=pl.BlockSpec((bm, tn), lambda n, b, be, nu, *_: (_blk(b, nu), n)),
            scratch_shapes=[pltpu.VMEM((2, 2, d, tn), F32), pltpu.SemaphoreType.DMA((2, 2))],
        ),
        out_shape=jax.ShapeDtypeStruct((n_rows, dff), BF16),
        compiler_params=_cparams(("arbitrary", "arbitrary")),
        name="moe_up",
    )(*tables, xs, w_gate, b_gate3, w_up, b_up3)


def _moe_down(tables, act, w_down, b_down3, bm):
    n_rows, dff = act.shape
    d = w_down.shape[2]
    assert d // 2 == ROW_TILE * LANES
    nb = n_rows // bm
    return pl.pallas_call(
        _moe_down_kernel,
        grid_spec=pltpu.PrefetchScalarGridSpec(
            num_scalar_prefetch=len(tables),
            grid=(nb,),
            in_specs=[
                pl.BlockSpec((bm, dff), lambda b, be, nu, *_: (_blk(b, nu), 0)),
                pl.BlockSpec(memory_space=pl.ANY),
                pl.BlockSpec((1, 1, d), lambda b, be, nu, *_: (be[_blk(b, nu)], 0, 0)),
            ],
            out_specs=pl.BlockSpec((ROW_TILE * bm, LANES), lambda b, be, nu, *_: (_blk(b, nu), 0)),
            scratch_shapes=[pltpu.VMEM((2, dff, d), F32), pltpu.SemaphoreType.DMA((2,))],
        ),
        out_shape=jax.ShapeDtypeStruct((ROW_TILE * n_rows, LANES), jnp.uint32),
        compiler_params=_cparams(("arbitrary",)),
        name="moe_down",
    )(*tables, act, w_down, b_down3)


def _combine_kernel(idx_ref, idxn_ref, y_hbm, x1_ref, g_ref, nw_ref, o_ref, buf, sem, *, tm):
    i = pl.program_id(0)
    slot = lax.rem(i, 2)
    rt = ROW_TILE

    def start_tile(ids_ref, s):
        def body(t, carry):
            for k in range(TOP_K):
                r = pl.multiple_of(ids_ref[0, 0, t * TOP_K + k], rt)
                dst = buf.at[s, pl.ds(pl.multiple_of((k * tm + t) * rt, rt), rt), :]
                pltpu.make_async_copy(y_hbm.at[pl.ds(r, rt), :], dst, sem.at[s]).start()
            return carry

        lax.fori_loop(0, tm, body, 0, unroll=4)

    @pl.when(i == 0)
    def _():
        start_tile(idx_ref, 0)

    @pl.when(i + 1 < pl.num_programs(0))
    def _():
        start_tile(idxn_ref, 1 - slot)

    pltpu.make_async_copy(y_hbm.at[pl.ds(0, rt * TOP_K * tm), :], buf.at[slot], sem.at[slot]).wait()
    acc = x1_ref[...]
    g = g_ref[...]
    for k in range(TOP_K):
        words = jnp.concatenate(
            [buf[slot, pl.ds(rt * k * tm + c, tm, stride=rt), :] for c in range(rt)], axis=1)
        y = jnp.concatenate([lax.bitcast_convert_type(words & jnp.uint32(_HI16), F32),
                             lax.bitcast_convert_type(lax.shift_left(words, jnp.uint32(16)), F32)], axis=1)
        acc = acc + g[:, k:k + 1] * y
    ms = jnp.mean(acc * acc, axis=-1, keepdims=True)
    o_ref[...] = acc * lax.rsqrt(ms + EPS) * nw_ref[...]


def _combine(dest, y_rows, x1, gates, norm_w, tm):
    t, d = x1.shape
    nt = t // tm
    idx3 = (dest * ROW_TILE).reshape(nt, 1, tm * TOP_K)
    kern = functools.partial(_combine_kernel, tm=tm)
    return pl.pallas_call(
        kern,
        grid=(nt,),
        in_specs=[
            pl.BlockSpec((1, 1, TOP_K * tm), lambda i: (i, 0, 0), memory_space=pltpu.SMEM),
            pl.BlockSpec((1, 1, TOP_K * tm), lambda i: (jnp.minimum(i + 1, nt - 1), 0, 0), memory_space=pltpu.SMEM),
            pl.BlockSpec(memory_space=pl.ANY),
            pl.BlockSpec((tm, d), lambda i: (i, 0)),
            pl.BlockSpec((tm, TOP_K), lambda i: (i, 0)),
            pl.BlockSpec((1, d), lambda i: (0, 0)),
        ],
        out_specs=pl.BlockSpec((tm, d), lambda i: (i, 0)),
        out_shape=jax.ShapeDtypeStruct((t, d), F32),
        scratch_shapes=[pltpu.VMEM((2, ROW_TILE * TOP_K * tm, LANES), jnp.uint32), pltpu.SemaphoreType.DMA((2,))],
        compiler_params=_cparams(("arbitrary",)),
        name="moe_combine",
    )(idx3, idx3, y_rows, x1, gates, norm_w)


def _route_tables(ids_ranks, counts, n_experts, bm, n_blocks):
    top_idx = ids_ranks[:, :TOP_K]
    rank = ids_ranks[:, TOP_K:2 * TOP_K]
    cnt = counts[0, :n_experts]
    padded = ((cnt + bm - 1) // bm) * bm
    padded_end = jnp.cumsum(padded)
    padded_start = padded_end - padded
    hot = top_idx[:, :, None] == jnp.arange(n_experts, dtype=jnp.int32)[None, None, :]
    dest = rank + jnp.sum(jnp.where(hot, padded_start[None, None, :], 0), axis=-1)
    n_used = (padded_end[-1] // bm).astype(jnp.int32)
    block_start = jnp.arange(n_blocks, dtype=jnp.int32) * bm
    block_expert = jnp.minimum(jnp.sum(padded_end[None, :] <= block_start[:, None], axis=1), n_experts - 1)
    zero_start = jnp.where(cnt > 0, padded_end - bm, -1).astype(jnp.int32)
    hot_b = block_expert[:, None] == jnp.arange(n_experts, dtype=block_expert.dtype)[None, :]
    group_end = jnp.sum(jnp.where(hot_b, (padded_start + cnt)[None, :], 0), axis=1)
    rows = jnp.clip(group_end - block_start, 0, bm)
    q = bm // MOE_BUCKETS
    block_quarters = jnp.clip((rows + q - 1) // q, 1, MOE_BUCKETS).astype(jnp.int32)
    block_expert = block_expert.astype(jnp.int32)
    ar = jnp.arange(n_blocks, dtype=jnp.int32)
    prev_e = jnp.concatenate([jnp.full((1,), -1, jnp.int32), block_expert[:-1]])
    first = jnp.logical_and(ar < n_used, block_expert != prev_e)
    run_id = (jnp.cumsum(first.astype(jnp.int32)) - 1).astype(jnp.int32)
    cand = jnp.where(first, ar, n_blocks)
    later = jnp.concatenate([lax.cummin(cand[::-1])[::-1][1:], jnp.full((1,), n_blocks, jnp.int32)])
    run_next = jnp.where(later < n_blocks, block_expert[jnp.minimum(later, n_blocks - 1)], -1).astype(jnp.int32)
    tables = (block_expert, n_used.reshape(1), block_quarters, first.astype(jnp.int32), run_id, run_next,
              jnp.sum(first.astype(jnp.int32)).reshape(1))
    return dest.astype(jnp.int32), zero_start, tables


def _pick(n, candidates):
    for c in candidates:
        if n % c == 0:
            return c
    return n


def _layer(x, norm_mix_w, w_in, conv_ssm_w, conv_ssm_b, dt_bias_fwd, dt_bias_bwd, a_log_fwd, a_log_bwd,
           d_skip, ssm_norm_w, conf_dw_w, conf_dw_b, conf_ln_w, conf_ln_b, w_out, norm_ffn_w, w_router,
           b_router, w_gate, b_gate, w_up, b_up, w_down, b_down):
    bsz, seq, d = x.shape
    t = bsz * seq
    n_heads = dt_bias_fwd.shape[0]
    d_ssm = n_heads * HEAD_DIM
    d_xbc = conv_ssm_w.shape[1]
    d_conf = conf_dw_w.shape[1]
    n_experts = w_gate.shape[0]
    row = lambda v: v.reshape(1, -1).astype(F32)

    tn = PROJ_TILE
    c_dt = d_ssm + d_xbc
    c_conf = c_dt + 2 * n_heads
    starts = ([s for s in range(0, d_ssm, tn)] + [c_conf + s for s in range(0, 2 * d_conf, tn)]
              + [d_ssm + s for s in range(0, d_xbc, tn)])
    w_tiles = jnp.stack([w_in[:, s:s + tn] for s in starts]).astype(BF16)
    wf = w_in[:, c_dt:c_dt + n_heads]
    wb = w_in[:, c_dt + n_heads:c_dt + 2 * n_heads]
    zpad = jnp.zeros((d, LANES - 3 * n_heads), F32)
    w_dt = jnp.concatenate([wf, wf, wf, zpad, wb, wb, wb, zpad], axis=1)
    rep3 = lambda v: jnp.concatenate([v, v, v, jnp.zeros((LANES - 3 * n_heads,), F32)])
    x2d = x.reshape(t, d)

    def split2(w):
        hi = w.astype(BF16)
        return hi, (w - hi.astype(F32)).astype(BF16)

    h, dt2, dtt2 = _norm_dt(x2d, row(norm_mix_w), *split2(w_dt), _pick(t, (512, 256, 128)))
    proj4 = _in_proj(h, w_tiles, _pick(t, (2048, 1024, 512, 256, 128))).reshape(-1, bsz, seq, tn)
    tile_conf = d_ssm // tn
    tile_xbc = (d_ssm + 2 * d_conf) // tn

    xbc = _conv_ssm(proj4, conv_ssm_w, row(conv_ssm_b), tile_xbc, _pick(seq, (1024, 512, 256, 128)), 512)
    y_conf = _conformer(proj4, conf_dw_w, row(conf_dw_b), row(conf_ln_w), row(conf_ln_b),
                        tile_conf, tile_conf + d_conf // tn, _pick(seq, (256, 128)))

    e3 = ((jnp.arange(LANES)[:, None] % n_heads == jnp.arange(d_ssm)[None, :] // HEAD_DIM)
          & (jnp.arange(LANES)[:, None] < 3 * n_heads)).astype(BF16)
    prm = {}
    for name, bias, alog in (("f", dt_bias_fwd, a_log_fwd), ("b", dt_bias_bwd, a_log_bwd)):
        b3 = rep3(bias.astype(F32))
        a3 = rep3(alog.astype(F32))
        prm[name] = (b3.reshape(1, LANES), jnp.broadcast_to(b3[:, None], (LANES, CHUNK)),
                     a3.reshape(1, LANES), jnp.broadcast_to(a3[:, None], (LANES, CHUNK)))
    y_f = _ssd(xbc, dt2, dtt2, *prm["f"], e3, reverse=False, n_heads=n_heads)
    dskip_x = jnp.repeat(d_skip.astype(F32), HEAD_DIM).reshape(1, d_ssm)
    y_ssm = _ssd(xbc, dt2, dtt2, *prm["b"], e3, reverse=True, n_heads=n_heads,
                 final_args=(y_f, proj4, dskip_x, row(ssm_norm_w)))

    wr = jnp.zeros((d, LANES), F32).at[:, :n_experts].set(w_router.astype(F32))
    br = jnp.zeros((1, LANES), F32).at[0, :n_experts].set(b_router.astype(F32))
    x1, logits = _out_proj(y_ssm.reshape(t, d_ssm), y_conf.reshape(t, d_conf), w_out.astype(BF16), x2d,
                           row(norm_ffn_w), *split2(wr), br, _pick(t, (512, 256, 128)))

    bm = MOE_ROWS
    n_blocks = -(-(t * TOP_K + n_experts * (bm - 1)) // bm)
    ids_ranks, gates_x, counts = _router(logits, n_experts, _pick(t, (1024, 512, 256, 128)))
    gates = gates_x[:, :TOP_K]
    dest, zero_start, tables = _route_tables(ids_ranks, counts, n_experts, bm, n_blocks)
    xs = _dispatch(dest, zero_start, x1, row(norm_ffn_w), n_blocks * bm, _pick(t, (512, 256, 128)), bm)
    act = _moe_up(tables, xs, w_gate, b_gate[:, None, :], w_up, b_up[:, None, :], bm, 1024)
    y_rows = _moe_down(tables, act, w_down, b_down[:, None, :], bm)
    return x1, gates, dest, y_rows


def kernel(x, norm_mix_w, w_in, conv_ssm_w, conv_ssm_b, dt_bias_fwd, dt_bias_bwd, a_log_fwd, a_log_bwd, d_skip,
           ssm_norm_w, conf_dw_w, conf_dw_b, conf_ln_w, conf_ln_b, w_out, norm_ffn_w, w_router, b_router, w_gate,
           b_gate, w_up, b_up, w_down, b_down, norm_final_w):
    assert w_in.shape[0] == 1, "a single layer is supported"
    bsz, seq, d = x.shape
    t = bsz * seq
    x1, gates, dest, y_rows = _layer(
        x, norm_mix_w[0], w_in[0], conv_ssm_w[0], conv_ssm_b[0], dt_bias_fwd[0], dt_bias_bwd[0], a_log_fwd[0],
        a_log_bwd[0], d_skip[0], ssm_norm_w[0], conf_dw_w[0], conf_dw_b[0], conf_ln_w[0], conf_ln_b[0], w_out[0],
        norm_ffn_w[0], w_router[0], b_router[0], w_gate[0], b_gate[0], w_up[0], b_up[0], w_down[0], b_down[0])
    out = _combine(dest, y_rows, x1, gates, norm_final_w.reshape(1, d).astype(F32), _pick(t, (256, 128)))
    return out.reshape(bsz, seq, d)
```

```python
import functools

import jax
import jax.numpy as jnp
from jax import lax
from jax.experimental import pallas as pl
from jax.experimental.pallas import tpu as pltpu

F32 = jnp.float32
BF16 = jnp.bfloat16

EPS = 1e-5
HEAD_DIM = 64
SSM_GROUPS = 4
D_STATE = 128
CHUNK = 128
SSM_CONV = 7
CONF_KERNEL = 31
TOP_K = 4
SWIGLU_LIMIT = 7.0
SWIGLU_ALPHA = 1.702

LANES = 128
SUBLANES = 8
VMEM_LIMIT_BYTES = 56 * 1024 * 1024
HALO = 16
MOE_ROWS = 512
MOE_BUCKETS = 4
ROW_TILE = SUBLANES
PROJ_TILE = 1024


def _cparams(sem):
    return pltpu.CompilerParams(dimension_semantics=sem, vmem_limit_bytes=VMEM_LIMIT_BYTES)


def _sigmoid(v):
    return 1.0 / (1.0 + jnp.exp(-v))


def _silu(v):
    return v * _sigmoid(v)


def _softplus(v):
    return jnp.maximum(v, 0.0) + jnp.log(1.0 + jnp.exp(-jnp.abs(v)))


def _split2(v):
    hi = v.astype(BF16)
    lo = (v - hi.astype(F32)).astype(BF16)
    return hi, lo


def _dot_split(a_hi, a_lo, b_hi_ref, b_lo_ref):
    b_hi = b_hi_ref[...]
    acc = jnp.dot(a_hi, b_hi, preferred_element_type=F32)
    acc = acc + jnp.dot(a_lo, b_hi, preferred_element_type=F32)
    return acc + jnp.dot(a_hi, b_lo_ref[...], preferred_element_type=F32)


def _norm_dt_kernel(x_ref, nw_ref, wdt_hi_ref, wdt_lo_ref, h_ref, dt_ref, dtt_ref):
    x = x_ref[...]
    ms = jnp.mean(x * x, axis=-1, keepdims=True)
    h_hi, h_lo = _split2(x * lax.rsqrt(ms + EPS) * nw_ref[...])
    h_ref[...] = h_hi
    dt = _dot_split(h_hi, h_lo, wdt_hi_ref, wdt_lo_ref)
    dt_ref[...] = dt
    dtt_ref[...] = dt.T


def _norm_dt(x2d, norm_w, w_dt_hi, w_dt_lo, tm):
    t, d = x2d.shape
    n_dt = w_dt_hi.shape[1]
    return pl.pallas_call(
        _norm_dt_kernel,
        grid=(t // tm,),
        in_specs=[
            pl.BlockSpec((tm, d), lambda m: (m, 0)),
            pl.BlockSpec((1, d), lambda m: (0, 0)),
            pl.BlockSpec((d, n_dt), lambda m: (0, 0)),
            pl.BlockSpec((d, n_dt), lambda m: (0, 0)),
        ],
        out_specs=[
            pl.BlockSpec((tm, d), lambda m: (m, 0)),
            pl.BlockSpec((tm, n_dt), lambda m: (m, 0)),
            pl.BlockSpec((n_dt, tm), lambda m: (0, m)),
        ],
        out_shape=[
            jax.ShapeDtypeStruct((t, d), BF16),
            jax.ShapeDtypeStruct((t, n_dt), F32),
            jax.ShapeDtypeStruct((n_dt, t), F32),
        ],
        compiler_params=_cparams(("parallel",)),
        name="norm_dt",
    )(x2d, norm_w, w_dt_hi, w_dt_lo)


def _inproj_kernel(h_ref, w_ref, o_ref):
    o_ref[0] = jnp.dot(h_ref[...], w_ref[0], preferred_element_type=F32).astype(o_ref.dtype)


def _in_proj(h, w_tiles, tm):
    t, d = h.shape
    n_tiles, _, tn = w_tiles.shape
    return pl.pallas_call(
        _inproj_kernel,
        grid=(t // tm, n_tiles),
        in_specs=[
            pl.BlockSpec((tm, d), lambda m, n: (m, 0)),
            pl.BlockSpec((1, d, tn), lambda m, n: (n, 0, 0)),
        ],
        out_specs=pl.BlockSpec((1, tm, tn), lambda m, n: (n, m, 0)),
        out_shape=jax.ShapeDtypeStruct((n_tiles, t, tn), BF16),
        compiler_params=_cparams(("parallel", "arbitrary")),
        name="in_proj",
    )(h, w_tiles)


def _fill_halo_scratch(scr, cur, prev, nxt, lt):
    l = pl.program_id(1)
    last = pl.num_programs(1) - 1
    scr[HALO:HALO + lt, :] = cur
    scr[0:HALO, :] = jnp.where(l > 0, prev, 0.0)
    scr[HALO + lt:HALO + lt + HALO, :] = jnp.where(l < last, nxt, 0.0)


def _conv_ssm_kernel(cur_ref, prev_ref, next_ref, w_ref, b_ref, o_ref, scr, *, lt, rc):
    _fill_halo_scratch(scr, cur_ref[0, 0].astype(F32), prev_ref[0, 0].astype(F32), next_ref[0, 0].astype(F32), lt)
    pad = SSM_CONV // 2
    w = w_ref[...]
    bias = b_ref[...]

    def body(i, carry):
        r0 = pl.multiple_of(i * rc, rc)
        win = scr[pl.ds(r0 + (HALO - SUBLANES), rc + 2 * SUBLANES), :]
        acc = jnp.zeros((rc, scr.shape[1]), F32)
        for k in range(SSM_CONV):
            o = SUBLANES - pad + k
            acc = acc + w[k:k + 1, :] * win[o:o + rc, :]
        o_ref[0, pl.ds(r0, rc), :] = _silu(acc + bias).astype(o_ref.dtype)
        return carry

    lax.fori_loop(0, lt // rc, body, 0)


def _conv_ssm(proj4, conv_w, conv_b, tile0, lt, ct, rc=32):
    _, bsz, seq, tn = proj4.shape
    c = conv_w.shape[1]
    per = tn // ct
    hb = lt // HALO
    n_hb = seq // HALO
    kern = functools.partial(_conv_ssm_kernel, lt=lt, rc=rc)
    return pl.pallas_call(
        kern,
        grid=(bsz, seq // lt, c // ct),
        in_specs=[
            pl.BlockSpec((1, 1, lt, ct), lambda b, l, j: (tile0 + j // per, b, l, j % per)),
            pl.BlockSpec((1, 1, HALO, ct),
                         lambda b, l, j: (tile0 + j // per, b, jnp.maximum(l * hb - 1, 0), j % per)),
            pl.BlockSpec((1, 1, HALO, ct),
                         lambda b, l, j: (tile0 + j // per, b, jnp.minimum((l + 1) * hb, n_hb - 1), j % per)),
            pl.BlockSpec((SSM_CONV, ct), lambda b, l, j: (0, j)),
            pl.BlockSpec((1, ct), lambda b, l, j: (0, j)),
        ],
        out_specs=pl.BlockSpec((1, lt, ct), lambda b, l, j: (b, l, j)),
        out_shape=jax.ShapeDtypeStruct((bsz, seq, c), BF16),
        scratch_shapes=[pltpu.VMEM((lt + 2 * HALO, ct), F32)],
        compiler_params=_cparams(("parallel", "parallel", "parallel")),
        name="conv_ssm",
    )(proj4, proj4, proj4, conv_w, conv_b)


def _conformer_kernel(*refs, lt, rc, lc, n_t):
    tiles = refs[:6 * n_t]
    w_ref, b_ref, lnw_ref, lnb_ref, o_ref, scr, sh_scr, v_scr = refs[6 * n_t:]

    def glu(group):
        a = jnp.concatenate([r[0, 0].astype(F32) for r in tiles[2 * n_t * group:2 * n_t * group + n_t]], axis=1)
        g = jnp.concatenate([r[0, 0].astype(F32) for r in tiles[2 * n_t * group + n_t:2 * n_t * (group + 1)]], axis=1)
        return a * _sigmoid(g)

    _fill_halo_scratch(scr, glu(0), glu(1), glu(2), lt)
    c = scr.shape[1]
    pad = CONF_KERNEL // 2
    n_sh = lt + 2 * HALO - SUBLANES
    for r0 in range(0, n_sh, rc):
        n = min(rc, n_sh - r0)
        for j in range(c // lc):
            win = scr[r0:r0 + n + SUBLANES, j * lc:(j + 1) * lc]
            for m in range(1, SUBLANES):
                sh_scr[m - 1, r0:r0 + n, j * lc:(j + 1) * lc] = win[m:m + n, :]

    def conv_body(i, carry):
        r0 = pl.multiple_of(i * rc, rc)
        for j in range(c // lc):
            lanes = slice(j * lc, (j + 1) * lc)
            acc = jnp.zeros((rc, lc), F32)
            for m in range(SUBLANES):
                offs = [(k, HALO - pad + k - m) for k in range(CONF_KERNEL) if (HALO - pad + k) % SUBLANES == m]
                if not offs:
                    continue
                span = rc + offs[-1][1]
                if m == 0:
                    win = scr[pl.ds(r0, span), lanes]
                else:
                    win = sh_scr[m - 1, pl.ds(r0, span), lanes]
                for k, base in offs:
                    acc = acc + w_ref[k:k + 1, lanes] * win[base:base + rc, :]
            v_scr[pl.ds(r0, rc), lanes] = acc + b_ref[:, lanes]
        return carry

    lax.fori_loop(0, lt // rc, conv_body, 0)
    lnw = lnw_ref[...]
    lnb = lnb_ref[...]
    rn = 2 * SUBLANES

    def norm_body(i, carry):
        r0 = pl.multiple_of(i * rn, rn)
        v = v_scr[pl.ds(r0, rn), :]
        mu = jnp.mean(v, axis=-1, keepdims=True)
        vc = v - mu
        var = jnp.mean(vc * vc, axis=-1, keepdims=True)
        y = vc * lax.rsqrt(var + EPS) * lnw + lnb
        o_ref[0, pl.ds(r0, rn), :] = _silu(y).astype(o_ref.dtype)
        return carry

    lax.fori_loop(0, lt // rn, norm_body, 0, unroll=4)


def _conformer(proj4, dw_w, dw_b, ln_w, ln_b, tile_a, tile_g, lt, rc=64, lc=128):
    _, bsz, seq, tn = proj4.shape
    c = dw_w.shape[1]
    n_t = c // tn
    hb = lt // HALO
    n_hb = seq // HALO
    kern = functools.partial(_conformer_kernel, lt=lt, rc=rc, lc=lc, n_t=n_t)

    def cur(j):
        return pl.BlockSpec((1, 1, lt, tn), lambda b, l: (j, b, l, 0))

    def prev(j):
        return pl.BlockSpec((1, 1, HALO, tn), lambda b, l: (j, b, jnp.maximum(l * hb - 1, 0), 0))

    def nxt(j):
        return pl.BlockSpec((1, 1, HALO, tn), lambda b, l: (j, b, jnp.minimum((l + 1) * hb, n_hb - 1), 0))

    tile_specs = [mk(t0 + j) for mk in (cur, prev, nxt) for t0 in (tile_a, tile_g) for j in range(n_t)]
    vec = pl.BlockSpec((1, c), lambda b, l: (0, 0))
    return pl.pallas_call(
        kern,
        grid=(bsz, seq // lt),
        in_specs=tile_specs + [pl.BlockSpec((CONF_KERNEL, c), lambda b, l: (0, 0)), vec, vec, vec],
        out_specs=pl.BlockSpec((1, lt, c), lambda b, l: (b, l, 0)),
        out_shape=jax.ShapeDtypeStruct((bsz, seq, c), BF16),
        scratch_shapes=[pltpu.VMEM((lt + 2 * HALO, c), F32),
                        pltpu.VMEM((SUBLANES - 1, lt + 2 * HALO, c), F32),
                        pltpu.VMEM((lt, c), F32)],
        compiler_params=_cparams(("parallel", "parallel")),
        name="conformer",
    )(*([proj4] * len(tile_specs)), dw_w, dw_b, ln_w, ln_b)


def _split3(v):
    hi = v.astype(BF16).astype(F32)
    r = v - hi
    mid = r.astype(BF16).astype(F32)
    lo = (r - mid).astype(BF16).astype(F32)
    return hi, mid, lo


def _ssd_kernel(*refs, reverse, final, n_heads, nbs):
    xs_ref, bm_ref, cm_ref, dt_ref = refs[:4]
    dtt_refs = refs[4:4 + nbs]
    consts = refs[4 + nbs:11 + nbs]
    if final:
        yf_ref = refs[11 + nbs]
        z_refs = refs[12 + nbs:-4]
        dsk_ref, nw_ref, o_ref, state_scr = refs[-4:]
    else:
        yf_ref, z_refs, dsk_ref, nw_ref = None, (), None, None
        o_ref, state_scr = refs[-2:]

    @pl.when(pl.program_id(1) == 0)
    def _():
        state_scr[...] = jnp.zeros_like(state_scr)

    for s in range(nbs):
        _ssd_chunk(s, xs_ref, bm_ref, cm_ref, dt_ref, dtt_refs[s], consts, yf_ref, z_refs, dsk_ref, nw_ref, o_ref,
                   state_scr, reverse=reverse, final=final, n_heads=n_heads)


def _ssd_chunk(s, xs_ref, bm_ref, cm_ref, dt_ref, dtt_ref, consts, yf_ref, z_refs, dsk_ref, nw_ref, o_ref,
               state_scr, *, reverse, final, n_heads):
    dtb_ref, dtbt_ref, alog_ref, alogt_ref, tri_ref, trit_ref, e3_ref = consts
    q = CHUNK
    hg = n_heads
    d_ssm = n_heads * HEAD_DIM
    gw = d_ssm // SSM_GROUPS

    lane = lax.broadcasted_iota(jnp.int32, (q, LANES), 1)
    sub = lax.broadcasted_iota(jnp.int32, (LANES, q), 0)

    def sel3_lanes(v):
        hi, mid, lo = _split3(v)
        return jnp.where(lane < hg, hi, jnp.where(lane < 2 * hg, mid, jnp.where(lane < 3 * hg, lo, 0.0))).astype(BF16)

    def sel3_rows(v):
        hi, mid, lo = _split3(v)
        return jnp.where(sub < hg, hi, jnp.where(sub < 2 * hg, mid, jnp.where(sub < 3 * hg, lo, 0.0))).astype(BF16)

    a_row = -jnp.exp(alog_ref[...])
    dt = _softplus(dt_ref[s] + dtb_ref[...])
    da = dt * a_row
    a_col = -jnp.exp(alogt_ref[...])
    dat = _softplus(dtt_ref[...] + dtbt_ref[...]) * a_col

    tri = tri_ref[...]
    cum3 = jnp.dot(tri, sel3_lanes(da), preferred_element_type=F32)
    cum = cum3
    for r in (1, 2, 3):
        cum = cum + pltpu.roll(cum3, r * hg, axis=1)
    cumt3 = jnp.dot(sel3_rows(dat), trit_ref[...], preferred_element_type=F32)
    cumt = cumt3[0:hg] + cumt3[hg:2 * hg] + cumt3[2 * hg:3 * hg]

    tot_row = 0 if reverse else q - 1
    total = cum[tot_row:tot_row + 1, :]

    e3 = e3_ref[...]
    lhs = jnp.concatenate(
        [sel3_lanes(dt), sel3_lanes(jnp.exp(cum)), sel3_lanes(jnp.exp(total - cum)),
         sel3_lanes(jnp.broadcast_to(jnp.exp(total), (q, LANES)))], axis=0)
    ex = jnp.dot(lhs, e3, preferred_element_type=F32)
    dt_x = ex[0:q]
    ecum_x = ex[q:2 * q]
    edte_x = ex[2 * q:3 * q]
    cdec_x = ex[3 * q:3 * q + 1]

    xs = xs_ref[s].astype(F32)
    xdt = xs * dt_x
    xdt_b = xdt.astype(BF16)
    xdte_b = (xdt * edte_x).astype(BF16)
    bm = bm_ref[s]
    cm = cm_ref[s]

    li = lax.broadcasted_iota(jnp.int32, (q, q), 0)
    si = lax.broadcasted_iota(jnp.int32, (q, q), 1)
    mask = (li <= si) if reverse else (li >= si)
    lane_lo = lax.broadcasted_iota(jnp.int32, (q, LANES), 1) < HEAD_DIM

    hpg = n_heads // SSM_GROUPS
    y_parts = []
    for g in range(SSM_GROUPS):
        bg = bm[:, g * D_STATE:(g + 1) * D_STATE]
        cg = cm[:, g * D_STATE:(g + 1) * D_STATE]
        cb = lax.dot_general(cg, bg, (((1,), (1,)), ((), ())), preferred_element_type=F32)
        st = state_scr[s, :, g * gw:(g + 1) * gw]
        y_off = jnp.dot(cg, st.astype(BF16), preferred_element_type=F32) * ecum_x[:, g * gw:(g + 1) * gw]
        diag = []
        for pair in range(hpg // 2):
            h0 = g * hpg + 2 * pair
            ls = []
            for h in (h0, h0 + 1):
                seg = cum[:, h:h + 1] - cumt[h:h + 1, :]
                ls.append((cb * jnp.exp(jnp.where(mask, seg, -jnp.inf))).astype(BF16))
            l2 = jnp.concatenate(ls, axis=1)
            xp = xdt_b[:, h0 * HEAD_DIM:(h0 + 2) * HEAD_DIM]
            zero = jnp.zeros_like(xp)
            r2 = jnp.concatenate([jnp.where(lane_lo, xp, zero), jnp.where(lane_lo, zero, xp)], axis=0)
            diag.append(jnp.dot(l2, r2, preferred_element_type=F32))
        y_parts.append(jnp.concatenate(diag, axis=1) + y_off)
        contrib = lax.dot_general(bg, xdte_b[:, g * gw:(g + 1) * gw], (((0,), (0,)), ((), ())),
                                  preferred_element_type=F32)
        state_scr[s, :, g * gw:(g + 1) * gw] = st * cdec_x[:, g * gw:(g + 1) * gw] + contrib
    y = jnp.concatenate(y_parts, axis=1)

    if not final:
        o_ref[s] = y
    else:
        y = y + yf_ref[s] + dsk_ref[...] * xs
        y = y * _silu(jnp.concatenate([r[0, s].astype(F32) for r in z_refs], axis=1))
        outs = []
        for g in range(SSM_GROUPS):
            yg = y[:, g * gw:(g + 1) * gw]
            ms = jnp.mean(yg * yg, axis=-1, keepdims=True)
            outs.append(yg * lax.rsqrt(ms + EPS))
        o_ref[s] = (jnp.concatenate(outs, axis=1) * nw_ref[...]).astype(o_ref.dtype)


def _ssd(xbc3, dt2, dtt2, dtb, dtbt, alog, alogt, e3, *, reverse, n_heads, final_args=None):
    bsz, seq, _ = xbc3.shape
    d_ssm = n_heads * HEAD_DIM
    gn = SSM_GROUPS * D_STATE
    nc = seq // CHUNK
    q = CHUNK
    d = 1 if reverse else 0
    final = final_args is not None

    def cidx(c):
        return (nc - 1 - c) if reverse else c

    li = jnp.arange(q)[:, None]
    ji = jnp.arange(q)[None, :]
    tri = ((ji >= li) if reverse else (ji <= li)).astype(BF16)
    trit = tri.T

    nbs = 2 if bsz % 2 == 0 else 1
    dt3 = dt2.reshape(bsz, seq, dt2.shape[1])
    const = lambda shape: pl.BlockSpec(shape, lambda b, c: (0,) * len(shape))
    in_specs = [
        pl.BlockSpec((nbs, q, d_ssm), lambda b, c: (b, cidx(c), 0)),
        pl.BlockSpec((nbs, q, gn), lambda b, c: (b, cidx(c), d_ssm // gn)),
        pl.BlockSpec((nbs, q, gn), lambda b, c: (b, cidx(c), d_ssm // gn + 1)),
        pl.BlockSpec((nbs, q, LANES), lambda b, c: (b, cidx(c), d)),
    ]
    in_specs += [pl.BlockSpec((LANES, q), lambda b, c, s=s: (d, (b * nbs + s) * nc + cidx(c))) for s in range(nbs)]
    in_specs += [const((1, LANES)), const((LANES, q)), const((1, LANES)), const((LANES, q)),
                 const((q, q)), const((q, q)), const((LANES, d_ssm))]
    args = [xbc3, xbc3, xbc3, dt3] + [dtt2] * nbs + [dtb, dtbt, alog, alogt, tri, trit, e3]
    if final:
        yf3, proj4, dskip_x, norm_w = final_args
        tn = proj4.shape[3]
        z_specs = [pl.BlockSpec((1, nbs, q, tn), lambda b, c, j=j: (j, b, cidx(c), 0)) for j in range(d_ssm // tn)]
        in_specs += [pl.BlockSpec((nbs, q, d_ssm), lambda b, c: (b, cidx(c), 0))] + z_specs
        in_specs += [const((1, d_ssm)), const((1, d_ssm))]
        args += [yf3] + [proj4] * len(z_specs) + [dskip_x, norm_w]
    kern = functools.partial(_ssd_kernel, reverse=reverse, final=final, n_heads=n_heads, nbs=nbs)
    return pl.pallas_call(
        kern,
        grid=(bsz // nbs, nc),
        in_specs=in_specs,
        out_specs=pl.BlockSpec((nbs, q, d_ssm), lambda b, c: (b, cidx(c), 0)),
        out_shape=jax.ShapeDtypeStruct((bsz, seq, d_ssm), BF16 if final else F32),
        scratch_shapes=[pltpu.VMEM((nbs, D_STATE, d_ssm), F32)],
        compiler_params=_cparams(("parallel", "arbitrary")),
        name="ssd_bwd" if reverse else "ssd_fwd",
    )(*args)


def _outproj_kernel(ys_ref, yc_ref, wa_ref, wb_ref, x_ref, nw_ref, wr_hi_ref, wr_lo_ref, br_ref, x1_ref, lg_ref):
    acc = jnp.dot(ys_ref[...], wa_ref[...], preferred_element_type=F32)
    acc = acc + jnp.dot(yc_ref[...], wb_ref[...], preferred_element_type=F32)
    x1 = x_ref[...] + acc
    x1_ref[...] = x1
    ms = jnp.mean(x1 * x1, axis=-1, keepdims=True)
    h_hi, h_lo = _split2(x1 * lax.rsqrt(ms + EPS) * nw_ref[...])
    lg_ref[...] = _dot_split(h_hi, h_lo, wr_hi_ref, wr_lo_ref) + br_ref[...]


def _out_proj(y_ssm, y_conf, w_out, x2d, norm_w, wr_hi, wr_lo, b_router, tm):
    t, d = x2d.shape
    ka = y_ssm.shape[1]
    ne = wr_hi.shape[1]
    resident = lambda shape, idx: pl.BlockSpec(shape, lambda m: idx, pipeline_mode=pl.Buffered(1))
    return pl.pallas_call(
        _outproj_kernel,
        grid=(t // tm,),
        in_specs=[
            pl.BlockSpec((tm, ka), lambda m: (m, 0)),
            pl.BlockSpec((tm, ka), lambda m: (m, 0)),
            resident((ka, d), (0, 0)),
            resident((ka, d), (1, 0)),
            pl.BlockSpec((tm, d), lambda m: (m, 0)),
            pl.BlockSpec((1, d), lambda m: (0, 0)),
            resident((d, ne), (0, 0)),
            resident((d, ne), (0, 0)),
            pl.BlockSpec((1, ne), lambda m: (0, 0)),
        ],
        out_specs=[
            pl.BlockSpec((tm, d), lambda m: (m, 0)),
            pl.BlockSpec((tm, ne), lambda m: (m, 0)),
        ],
        out_shape=[
            jax.ShapeDtypeStruct((t, d), F32),
            jax.ShapeDtypeStruct((t, ne), F32),
        ],
        compiler_params=_cparams(("parallel",)),
        name="out_proj",
    )(y_ssm, y_conf, w_out, w_out, x2d, norm_w, wr_hi, wr_lo, b_router)


def _router_kernel(lg_ref, tri_ref, ir_ref, gate_ref, cnt_ref, run_scr, *, n_experts):
    @pl.when(pl.program_id(0) == 0)
    def _():
        run_scr[...] = jnp.zeros_like(run_scr)

    tr = lg_ref.shape[0]
    lane = lax.broadcasted_iota(jnp.int32, (tr, LANES), 1)
    lane_f = lane.astype(F32)
    lg = jnp.where(lane < n_experts, lg_ref[...], -jnp.inf)
    vals, ids, hots = [], [], []
    for _ in range(TOP_K):
        m = jnp.max(lg, axis=1, keepdims=True)
        idx = jnp.min(jnp.where(lg == m, lane_f, float(LANES)), axis=1, keepdims=True)
        hot = lane_f == idx
        vals.append(m)
        ids.append(idx)
        hots.append(hot)
        lg = jnp.where(hot, -jnp.inf, lg)
    es = [jnp.exp(v - vals[0]) for v in vals]
    den = es[0]
    for e in es[1:]:
        den = den + e
    hot_any = hots[0].astype(F32)
    for hot in hots[1:]:
        hot_any = hot_any + hot.astype(F32)
    before = jnp.dot(tri_ref[...], hot_any.astype(BF16), preferred_element_type=F32) + run_scr[...]
    run_scr[...] = run_scr[...] + jnp.sum(hot_any, axis=0, keepdims=True)
    cnt_ref[...] = run_scr[...].astype(jnp.int32)
    out_i = jnp.zeros((tr, LANES), F32)
    out_g = jnp.zeros((tr, LANES), F32)
    for k in range(TOP_K):
        rank = jnp.sum(jnp.where(hots[k], before, 0.0), axis=1, keepdims=True)
        out_i = jnp.where(lane == k, ids[k], out_i)
        out_i = jnp.where(lane == TOP_K + k, rank, out_i)
        out_g = jnp.where(lane == k, es[k] / den, out_g)
    ir_ref[...] = out_i.astype(jnp.int32)
    gate_ref[...] = out_g


def _router(logits, n_experts, tr):
    t = logits.shape[0]
    tri = (jnp.arange(tr)[None, :] < jnp.arange(tr)[:, None]).astype(BF16)
    kern = functools.partial(_router_kernel, n_experts=n_experts)
    return pl.pallas_call(
        kern,
        grid=(t // tr,),
        in_specs=[pl.BlockSpec((tr, LANES), lambda i: (i, 0)), pl.BlockSpec((tr, tr), lambda i: (0, 0))],
        out_specs=[pl.BlockSpec((tr, LANES), lambda i: (i, 0)), pl.BlockSpec((tr, LANES), lambda i: (i, 0)),
                   pl.BlockSpec((1, LANES), lambda i: (0, 0))],
        out_shape=[jax.ShapeDtypeStruct((t, LANES), jnp.int32), jax.ShapeDtypeStruct((t, LANES), F32),
                   jax.ShapeDtypeStruct((1, LANES), jnp.int32)],
        scratch_shapes=[pltpu.VMEM((1, LANES), F32)],
        compiler_params=_cparams(("arbitrary",)),
        name="moe_router",
    )(logits, tri)


_HI16 = 0xFFFF0000


def _pack_bf16_pair(a, b):
    au = lax.bitcast_convert_type(a.astype(BF16).astype(F32), jnp.uint32)
    bu = lax.bitcast_convert_type(b.astype(BF16).astype(F32), jnp.uint32)
    return (au & jnp.uint32(_HI16)) | lax.shift_right_logical(bu, jnp.uint32(16))


def _unpack_bf16_pair(w):
    a = lax.bitcast_convert_type(w & jnp.uint32(_HI16), F32).astype(BF16)
    b = lax.bitcast_convert_type(lax.shift_left(w, jnp.uint32(16)), F32).astype(BF16)
    return a, b


def _dispatch_kernel(zs_ref, idx_ref, x1_ref, nw_ref, xs_hbm, zero_scr, pk, sem, zsem, *, tm, bm, n_experts):
    i = pl.program_id(0)
    last = pl.num_programs(0) - 1
    slot = lax.rem(i, 2)
    half = x1_ref.shape[1] // 2
    rt = ROW_TILE

    def zero_copy(e):
        dst = xs_hbm.at[pl.ds(pl.multiple_of(zs_ref[e], rt * bm), rt * bm), :]
        return pltpu.make_async_copy(zero_scr, dst, zsem)

    @pl.when(i == 0)
    def _():
        zero_scr[...] = jnp.zeros_like(zero_scr)
        for e in range(n_experts):
            @pl.when(zs_ref[e] >= 0)
            def _():
                zero_copy(e).start()
        for e in range(n_experts):
            @pl.when(zs_ref[e] >= 0)
            def _():
                zero_copy(e).wait()

    def wait_rows(s):
        rows = xs_hbm.at[pl.ds(0, rt * TOP_K * tm), :]
        pltpu.make_async_copy(rows, rows, sem.at[s]).wait()

    @pl.when(i >= 2)
    def _():
        wait_rows(slot)

    x = x1_ref[...]
    ms = jnp.mean(x * x, axis=-1, keepdims=True)
    h = x * lax.rsqrt(ms + EPS) * nw_ref[...]
    packed = _pack_bf16_pair(h[:, :half], h[:, half:])
    for c in range(rt):
        pk[slot, pl.ds(c, tm, stride=rt), :] = packed[:, c * LANES:(c + 1) * LANES]

    def body(t, carry):
        src = pk.at[slot, pl.ds(pl.multiple_of(t * rt, rt), rt), :]
        for k in range(TOP_K):
            r = pl.multiple_of(idx_ref[0, 0, t * TOP_K + k], rt)
            pltpu.make_async_copy(src, xs_hbm.at[pl.ds(r, rt), :], sem.at[slot]).start(priority=k % 2)
        return carry

    lax.fori_loop(0, tm, body, 0, unroll=4)

    @pl.when(i == last)
    def _():
        wait_rows(slot)

    @pl.when(jnp.logical_and(i == last, i >= 1))
    def _():
        wait_rows(1 - slot)


def _dispatch(dest, zero_start, x1, norm_w, n_rows, tm, bm):
    t, d = x1.shape
    assert d // 2 == ROW_TILE * LANES
    nt = t // tm
    n_experts = zero_start.shape[0]
    idx3 = (dest * ROW_TILE).reshape(nt, 1, tm * TOP_K)
    zero_start = zero_start * ROW_TILE
    kern = functools.partial(_dispatch_kernel, tm=tm, bm=bm, n_experts=n_experts)
    return pl.pallas_call(
        kern,
        grid_spec=pltpu.PrefetchScalarGridSpec(
            num_scalar_prefetch=1,
            grid=(nt,),
            in_specs=[
                pl.BlockSpec((1, 1, TOP_K * tm), lambda i, zs: (i, 0, 0), memory_space=pltpu.SMEM),
                pl.BlockSpec((tm, d), lambda i, zs: (i, 0)),
                pl.BlockSpec((1, d), lambda i, zs: (0, 0)),
            ],
            out_specs=pl.BlockSpec(memory_space=pl.ANY),
            scratch_shapes=[pltpu.VMEM((ROW_TILE * bm, LANES), jnp.uint32),
                            pltpu.VMEM((2, ROW_TILE * tm, LANES), jnp.uint32),
                            pltpu.SemaphoreType.DMA((2,)), pltpu.SemaphoreType.DMA(())],
        ),
        out_shape=jax.ShapeDtypeStruct((ROW_TILE * n_rows, LANES), jnp.uint32),
        compiler_params=_cparams(("arbitrary",)),
        name="moe_dispatch",
    )(zero_start, idx3, x1, norm_w)


def _for_row_bucket(b, nu_ref, bq_ref, bm, compute):
    for quarters in range(1, MOE_BUCKETS + 1):
        @pl.when(jnp.logical_and(b < nu_ref[0], bq_ref[b] == quarters))
        def _():
            compute(quarters * (bm // MOE_BUCKETS))


def _expert_weights(b, tile, n_tiles, tabs, copies):
    be_ref, nu_ref, first_ref, run_ref, next_ref, nruns_ref = tabs
    run = run_ref[b]
    slot = lax.rem(tile * nruns_ref[0] + run, 2)
    first = jnp.logical_and(b < nu_ref[0], first_ref[b] == 1)
    nxt = next_ref[b]

    @pl.when(jnp.logical_and(first, jnp.logical_and(tile == 0, run == 0)))
    def _():
        for c in copies(be_ref[b], tile, slot):
            c.start()

    @pl.when(first)
    def _():
        for c in copies(be_ref[b], tile, slot):
            c.wait()

    @pl.when(jnp.logical_and(first, nxt >= 0))
    def _():
        for c in copies(nxt, tile, 1 - slot):
            c.start()

    @pl.when(jnp.logical_and(first, jnp.logical_and(nxt < 0, tile + 1 < n_tiles)))
    def _():
        for c in copies(be_ref[0], tile + 1, 1 - slot):
            c.start()

    return slot


def _moe_up_kernel(be_ref, nu_ref, bq_ref, first_ref, run_ref, next_ref, nruns_ref, x_ref, wg_hbm, bg_ref, wu_hbm,
                   bu_ref, o_ref, wbuf, wsem):
    bm = x_ref.shape[0] // ROW_TILE
    tn = o_ref.shape[1]

    def copies(e, tile, s):
        cols = pl.ds(pl.multiple_of(tile * tn, tn), tn)
        return [pltpu.make_async_copy(wg_hbm.at[e, :, cols], wbuf.at[s, 0], wsem.at[s, 0]),
                pltpu.make_async_copy(wu_hbm.at[e, :, cols], wbuf.at[s, 1], wsem.at[s, 1])]

    slot = _expert_weights(pl.program_id(1), pl.program_id(0), pl.num_programs(0),
                           (be_ref, nu_ref, first_ref, run_ref, next_ref, nruns_ref), copies)

    def compute(nv):
        words = jnp.concatenate([x_ref[pl.ds(c, nv, stride=ROW_TILE), :] for c in range(ROW_TILE)], axis=1)
        xa, xb = _unpack_bf16_pair(words)
        half = xa.shape[1]
        wg = wbuf[slot, 0].astype(BF16)
        wu = wbuf[slot, 1].astype(BF16)
        gate = (jnp.dot(xa, wg[:half], preferred_element_type=F32)
                + jnp.dot(xb, wg[half:], preferred_element_type=F32) + bg_ref[0])
        up = (jnp.dot(xa, wu[:half], preferred_element_type=F32)
              + jnp.dot(xb, wu[half:], preferred_element_type=F32) + bu_ref[0])
        gate = jnp.minimum(gate, SWIGLU_LIMIT)
        up = jnp.clip(up, -SWIGLU_LIMIT, SWIGLU_LIMIT)
        act = (up + 1.0) * gate * _sigmoid(SWIGLU_ALPHA * gate)
        o_ref[:nv, :] = act.astype(o_ref.dtype)

    _for_row_bucket(pl.program_id(1), nu_ref, bq_ref, bm, compute)


def _moe_down_kernel(be_ref, nu_ref, bq_ref, first_ref, run_ref, next_ref, nruns_ref, a_ref, wd_hbm, bd_ref, o_ref,
                     wbuf, wsem):
    def copies(e, tile, s):
        return [pltpu.make_async_copy(wd_hbm.at[e], wbuf.at[s], wsem.at[s])]

    slot = _expert_weights(pl.program_id(0), 0, 1, (be_ref, nu_ref, first_ref, run_ref, next_ref, nruns_ref), copies)

    def compute(nv):
        y = jnp.dot(a_ref[:nv, :], wbuf[slot].astype(BF16), preferred_element_type=F32) + bd_ref[0]
        d = y.shape[1]
        packed = _pack_bf16_pair(y[:, :d // 2], y[:, d // 2:])
        for c in range(ROW_TILE):
            o_ref[pl.ds(c, nv, stride=ROW_TILE), :] = packed[:, c * LANES:(c + 1) * LANES]

    _for_row_bucket(pl.program_id(0), nu_ref, bq_ref, a_ref.shape[0], compute)


def _blk(b, nu_ref):
    return jnp.minimum(b, nu_ref[0] - 1)


def _moe_up(tables, xs, w_gate, b_gate3, w_up, b_up3, bm, tn):
    n_rows = xs.shape[0] // ROW_TILE
    d, dff = w_gate.shape[1], w_gate.shape[2]
    nb = n_rows // bm
    hbm = pl.BlockSpec(memory_space=pl.ANY)
    bspec = pl.BlockSpec((1, 1, tn), lambda n, b, be, nu, *_: (be[_blk(b, nu)], 0, n))
    return pl.pallas_call(
        _moe_up_kernel,
        grid_spec=pltpu.PrefetchScalarGridSpec(
            num_scalar_prefetch=len(tables),
            grid=(dff // tn, nb),
            in_specs=[pl.BlockSpec((ROW_TILE * bm, LANES), lambda n, b, be, nu, *_: (_blk(b, nu), 0)),
                      hbm, bspec, hbm, bspec],
            out_specs=pl.BlockSpec((bm, tn), lambda n, b, be, nu, *_: (_blk(b, nu), n)),
            scratch_shapes=[pltpu.VMEM((2, 2, d, tn), F32), pltpu.SemaphoreType.DMA((2, 2))],
        ),
        out_shape=jax.ShapeDtypeStruct((n_rows, dff), BF16),
        compiler_params=_cparams(("arbitrary", "arbitrary")),
        name="moe_up",
    )(*tables, xs, w_gate, b_gate3, w_up, b_up3)


def _moe_down(tables, act, w_down, b_down3, bm):
    n_rows, dff = act.shape
    d = w_down.shape[2]
    assert d // 2 == ROW_TILE * LANES
    nb = n_rows // bm
    return pl.pallas_call(
        _moe_down_kernel,
        grid_spec=pltpu.PrefetchScalarGridSpec(
            num_scalar_prefetch=len(tables),
            grid=(nb,),
            in_specs=[
                pl.BlockSpec((bm, dff), lambda b, be, nu, *_: (_blk(b, nu), 0)),
                pl.BlockSpec(memory_space=pl.ANY),
                pl.BlockSpec((1, 1, d), lambda b, be, nu, *_: (be[_blk(b, nu)], 0, 0)),
            ],
            out_specs=pl.BlockSpec((ROW_TILE * bm, LANES), lambda b, be, nu, *_: (_blk(b, nu), 0)),
            scratch_shapes=[pltpu.VMEM((2, dff, d), F32), pltpu.SemaphoreType.DMA((2,))],
        ),
        out_shape=jax.ShapeDtypeStruct((ROW_TILE * n_rows, LANES), jnp.uint32),
        compiler_params=_cparams(("arbitrary",)),
        name="moe_down",
    )(*tables, act, w_down, b_down3)


def _combine_kernel(idx_ref, idxn_ref, y_hbm, x1_ref, g_ref, nw_ref, o_ref, buf, sem, *, tm):
    i = pl.program_id(0)
    slot = lax.rem(i, 2)
    rt = ROW_TILE

    def start_tile(ids_ref, s):
        def body(t, carry):
            for k in range(TOP_K):
                r = pl.multiple_of(ids_ref[0, 0, t * TOP_K + k], rt)
                dst = buf.at[s, pl.ds(pl.multiple_of((k * tm + t) * rt, rt), rt), :]
                pltpu.make_async_copy(y_hbm.at[pl.ds(r, rt), :], dst, sem.at[s]).start(priority=k % 2)
            return carry

        lax.fori_loop(0, tm, body, 0, unroll=4)

    @pl.when(i == 0)
    def _():
        start_tile(idx_ref, 0)

    @pl.when(i + 1 < pl.num_programs(0))
    def _():
        start_tile(idxn_ref, 1 - slot)

    pltpu.make_async_copy(y_hbm.at[pl.ds(0, rt * TOP_K * tm), :], buf.at[slot], sem.at[slot]).wait()
    acc = x1_ref[...]
    g = g_ref[...]
    for k in range(TOP_K):
        words = jnp.concatenate(
            [buf[slot, pl.ds(rt * k * tm + c, tm, stride=rt), :] for c in range(rt)], axis=1)
        y = jnp.concatenate([lax.bitcast_convert_type(words & jnp.uint32(_HI16), F32),
                             lax.bitcast_convert_type(lax.shift_left(words, jnp.uint32(16)), F32)], axis=1)
        acc = acc + g[:, k:k + 1] * y
    ms = jnp.mean(acc * acc, axis=-1, keepdims=True)
    o_ref[...] = acc * lax.rsqrt(ms + EPS) * nw_ref[...]


def _combine(dest, y_rows, x1, gates, norm_w, tm):
    t, d = x1.shape
    nt = t // tm
    idx3 = (dest * ROW_TILE).reshape(nt, 1, tm * TOP_K)
    kern = functools.partial(_combine_kernel, tm=tm)
    return pl.pallas_call(
        kern,
        grid=(nt,),
        in_specs=[
            pl.BlockSpec((1, 1, TOP_K * tm), lambda i: (i, 0, 0), memory_space=pltpu.SMEM),
            pl.BlockSpec((1, 1, TOP_K * tm), lambda i: (jnp.minimum(i + 1, nt - 1), 0, 0), memory_space=pltpu.SMEM),
            pl.BlockSpec(memory_space=pl.ANY),
            pl.BlockSpec((tm, d), lambda i: (i, 0)),
            pl.BlockSpec((tm, TOP_K), lambda i: (i, 0)),
            pl.BlockSpec((1, d), lambda i: (0, 0)),
        ],
        out_specs=pl.BlockSpec((tm, d), lambda i: (i, 0)),
        out_shape=jax.ShapeDtypeStruct((t, d), F32),
        scratch_shapes=[pltpu.VMEM((2, ROW_TILE * TOP_K * tm, LANES), jnp.uint32), pltpu.SemaphoreType.DMA((2,))],
        compiler_params=_cparams(("arbitrary",)),
        name="moe_combine",
    )(idx3, idx3, y_rows, x1, gates, norm_w)


def _route_tables(ids_ranks, counts, n_experts, bm, n_blocks):
    top_idx = ids_ranks[:, :TOP_K]
    rank = ids_ranks[:, TOP_K:2 * TOP_K]
    cnt = counts[0, :n_experts]
    padded = ((cnt + bm - 1) // bm) * bm
    padded_end = jnp.cumsum(padded)
    padded_start = padded_end - padded
    hot = top_idx[:, :, None] == jnp.arange(n_experts, dtype=jnp.int32)[None, None, :]
    dest = rank + jnp.sum(jnp.where(hot, padded_start[None, None, :], 0), axis=-1)
    n_used = (padded_end[-1] // bm).astype(jnp.int32)
    block_start = jnp.arange(n_blocks, dtype=jnp.int32) * bm
    block_expert = jnp.minimum(jnp.sum(padded_end[None, :] <= block_start[:, None], axis=1), n_experts - 1)
    zero_start = jnp.where(cnt > 0, padded_end - bm, -1).astype(jnp.int32)
    hot_b = block_expert[:, None] == jnp.arange(n_experts, dtype=block_expert.dtype)[None, :]
    group_end = jnp.sum(jnp.where(hot_b, (padded_start + cnt)[None, :], 0), axis=1)
    rows = jnp.clip(group_end - block_start, 0, bm)
    q = bm // MOE_BUCKETS
    block_quarters = jnp.clip((rows + q - 1) // q, 1, MOE_BUCKETS).astype(jnp.int32)
    block_expert = block_expert.astype(jnp.int32)
    ar = jnp.arange(n_blocks, dtype=jnp.int32)
    prev_e = jnp.concatenate([jnp.full((1,), -1, jnp.int32), block_expert[:-1]])
    first = jnp.logical_and(ar < n_used, block_expert != prev_e)
    run_id = (jnp.cumsum(first.astype(jnp.int32)) - 1).astype(jnp.int32)
    cand = jnp.where(first, ar, n_blocks)
    later = jnp.concatenate([lax.cummin(cand[::-1])[::-1][1:], jnp.full((1,), n_blocks, jnp.int32)])
    run_next = jnp.where(later < n_blocks, block_expert[jnp.minimum(later, n_blocks - 1)], -1).astype(jnp.int32)
    tables = (block_expert, n_used.reshape(1), block_quarters, first.astype(jnp.int32), run_id, run_next,
              jnp.sum(first.astype(jnp.int32)).reshape(1))
    return dest.astype(jnp.int32), zero_start, tables


def _pick(n, candidates):
    for c in candidates:
        if n % c == 0:
            return c
    return n


def _layer(x, norm_mix_w, w_in, conv_ssm_w, conv_ssm_b, dt_bias_fwd, dt_bias_bwd, a_log_fwd, a_log_bwd,
           d_skip, ssm_norm_w, conf_dw_w, conf_dw_b, conf_ln_w, conf_ln_b, w_out, norm_ffn_w, w_router,
           b_router, w_gate, b_gate, w_up, b_up, w_down, b_down):
    bsz, seq, d = x.shape
    t = bsz * seq
    n_heads = dt_bias_fwd.shape[0]
    d_ssm = n_heads * HEAD_DIM
    d_xbc = conv_ssm_w.shape[1]
    d_conf = conf_dw_w.shape[1]
    n_experts = w_gate.shape[0]
    row = lambda v: v.reshape(1, -1).astype(F32)

    tn = PROJ_TILE
    c_dt = d_ssm + d_xbc
    c_conf = c_dt + 2 * n_heads
    starts = ([s for s in range(0, d_ssm, tn)] + [c_conf + s for s in range(0, 2 * d_conf, tn)]
              + [d_ssm + s for s in range(0, d_xbc, tn)])
    w_tiles = jnp.stack([w_in[:, s:s + tn] for s in starts]).astype(BF16)
    wf = w_in[:, c_dt:c_dt + n_heads]
    wb = w_in[:, c_dt + n_heads:c_dt + 2 * n_heads]
    zpad = jnp.zeros((d, LANES - 3 * n_heads), F32)
    w_dt = jnp.concatenate([wf, wf, wf, zpad, wb, wb, wb, zpad], axis=1)
    rep3 = lambda v: jnp.concatenate([v, v, v, jnp.zeros((LANES - 3 * n_heads,), F32)])
    x2d = x.reshape(t, d)

    def split2(w):
        hi = w.astype(BF16)
        return hi, (w - hi.astype(F32)).astype(BF16)

    h, dt2, dtt2 = _norm_dt(x2d, row(norm_mix_w), *split2(w_dt), _pick(t, (512, 256, 128)))
    proj4 = _in_proj(h, w_tiles, _pick(t, (2048, 1024, 512, 256, 128))).reshape(-1, bsz, seq, tn)
    tile_conf = d_ssm // tn
    tile_xbc = (d_ssm + 2 * d_conf) // tn

    xbc = _conv_ssm(proj4, conv_ssm_w, row(conv_ssm_b), tile_xbc, _pick(seq, (1024, 512, 256, 128)), 512)
    y_conf = _conformer(proj4, conf_dw_w, row(conf_dw_b), row(conf_ln_w), row(conf_ln_b),
                        tile_conf, tile_conf + d_conf // tn, _pick(seq, (256, 128)))

    e3 = ((jnp.arange(LANES)[:, None] % n_heads == jnp.arange(d_ssm)[None, :] // HEAD_DIM)
          & (jnp.arange(LANES)[:, None] < 3 * n_heads)).astype(BF16)
    prm = {}
    for name, bias, alog in (("f", dt_bias_fwd, a_log_fwd), ("b", dt_bias_bwd, a_log_bwd)):
        b3 = rep3(bias.astype(F32))
        a3 = rep3(alog.astype(F32))
        prm[name] = (b3.reshape(1, LANES), jnp.broadcast_to(b3[:, None], (LANES, CHUNK)),
                     a3.reshape(1, LANES), jnp.broadcast_to(a3[:, None], (LANES, CHUNK)))
    y_f = _ssd(xbc, dt2, dtt2, *prm["f"], e3, reverse=False, n_heads=n_heads)
    dskip_x = jnp.repeat(d_skip.astype(F32), HEAD_DIM).reshape(1, d_ssm)
    y_ssm = _ssd(xbc, dt2, dtt2, *prm["b"], e3, reverse=True, n_heads=n_heads,
                 final_args=(y_f, proj4, dskip_x, row(ssm_norm_w)))

    wr = jnp.zeros((d, LANES), F32).at[:, :n_experts].set(w_router.astype(F32))
    br = jnp.zeros((1, LANES), F32).at[0, :n_experts].set(b_router.astype(F32))
    x1, logits = _out_proj(y_ssm.reshape(t, d_ssm), y_conf.reshape(t, d_conf), w_out.astype(BF16), x2d,
                           row(norm_ffn_w), *split2(wr), br, _pick(t, (512, 256, 128)))

    bm = MOE_ROWS
    n_blocks = -(-(t * TOP_K + n_experts * (bm - 1)) // bm)
    ids_ranks, gates_x, counts = _router(logits, n_experts, _pick(t, (1024, 512, 256, 128)))
    gates = gates_x[:, :TOP_K]
    dest, zero_start, tables = _route_tables(ids_ranks, counts, n_experts, bm, n_blocks)
    xs = _dispatch(dest, zero_start, x1, row(norm_ffn_w), n_blocks * bm, _pick(t, (512, 256, 128)), bm)
    act = _moe_up(tables, xs, w_gate, b_gate[:, None, :], w_up, b_up[:, None, :], bm, 1024)
    y_rows = _moe_down(tables, act, w_down, b_down[:, None, :], bm)
    return x1, gates, dest, y_rows


def kernel(x, norm_mix_w, w_in, conv_ssm_w, conv_ssm_b, dt_bias_fwd, dt_bias_bwd, a_log_fwd, a_log_bwd, d_skip,
           ssm_norm_w, conf_dw_w, conf_dw_b, conf_ln_w, conf_ln_b, w_out, norm_ffn_w, w_router, b_router, w_gate,
           b_gate, w_up, b_up, w_down, b_down, norm_final_w):
    assert w_in.shape[0] == 1, "a single layer is supported"
    bsz, seq, d = x.shape
    t = bsz * seq
    x1, gates, dest, y_rows = _layer(
        x, norm_mix_w[0], w_in[0], conv_ssm_w[0], conv_ssm_b[0], dt_bias_fwd[0], dt_bias_bwd[0], a_log_fwd[0],
        a_log_bwd[0], d_skip[0], ssm_norm_w[0], conf_dw_w[0], conf_dw_b[0], conf_ln_w[0], conf_ln_b[0], w_out[0],
        norm_ffn_w[0], w_router[0], b_router[0], w_gate[0], b_gate[0], w_up[0], b_up[0], w_down[0], b_down[0])
    out = _combine(dest, y_rows, x1, gates, norm_final_w.reshape(1, d).astype(F32), _pick(t, (256, 128)))
    return out.reshape(bsz, seq, d)
```

```python
import functools

import jax
import jax.numpy as jnp
from jax import lax
from jax.experimental import pallas as pl
from jax.experimental.pallas import tpu as pltpu

F32 = jnp.float32
BF16 = jnp.bfloat16

EPS = 1e-5
HEAD_DIM = 64
SSM_GROUPS = 4
D_STATE = 128
CHUNK = 128
SSM_CONV = 7
CONF_KERNEL = 31
TOP_K = 4
SWIGLU_LIMIT = 7.0
SWIGLU_ALPHA = 1.702

LANES = 128
SUBLANES = 8
VMEM_LIMIT_BYTES = 56 * 1024 * 1024
HALO = 16
MOE_ROWS = 512
MOE_BUCKETS = 4
ROW_TILE = SUBLANES
PROJ_TILE = 1024


def _cparams(sem):
    return pltpu.CompilerParams(dimension_semantics=sem, vmem_limit_bytes=VMEM_LIMIT_BYTES)


def _sigmoid(v):
    return 1.0 / (1.0 + jnp.exp(-v))


def _silu(v):
    return v * _sigmoid(v)


def _softplus(v):
    return jnp.maximum(v, 0.0) + jnp.log(1.0 + jnp.exp(-jnp.abs(v)))


def _split2(v):
    hi = v.astype(BF16)
    lo = (v - hi.astype(F32)).astype(BF16)
    return hi, lo


def _dot_split(a_hi, a_lo, b_hi_ref, b_lo_ref):
    b_hi = b_hi_ref[...]
    acc = jnp.dot(a_hi, b_hi, preferred_element_type=F32)
    acc = acc + jnp.dot(a_lo, b_hi, preferred_element_type=F32)
    return acc + jnp.dot(a_hi, b_lo_ref[...], preferred_element_type=F32)


def _norm_dt_kernel(x_ref, nw_ref, wdt_hi_ref, wdt_lo_ref, h_ref, dt_ref, dtt_ref):
    x = x_ref[...]
    ms = jnp.mean(x * x, axis=-1, keepdims=True)
    h_hi, h_lo = _split2(x * lax.rsqrt(ms + EPS) * nw_ref[...])
    h_ref[...] = h_hi
    dt = _dot_split(h_hi, h_lo, wdt_hi_ref, wdt_lo_ref)
    dt_ref[...] = dt
    dtt_ref[...] = dt.T


def _norm_dt(x2d, norm_w, w_dt_hi, w_dt_lo, tm):
    t, d = x2d.shape
    n_dt = w_dt_hi.shape[1]
    return pl.pallas_call(
        _norm_dt_kernel,
        grid=(t // tm,),
        in_specs=[
            pl.BlockSpec((tm, d), lambda m: (m, 0)),
            pl.BlockSpec((1, d), lambda m: (0, 0)),
            pl.BlockSpec((d, n_dt), lambda m: (0, 0)),
            pl.BlockSpec((d, n_dt), lambda m: (0, 0)),
        ],
        out_specs=[
            pl.BlockSpec((tm, d), lambda m: (m, 0)),
            pl.BlockSpec((tm, n_dt), lambda m: (m, 0)),
            pl.BlockSpec((n_dt, tm), lambda m: (0, m)),
        ],
        out_shape=[
            jax.ShapeDtypeStruct((t, d), BF16),
            jax.ShapeDtypeStruct((t, n_dt), F32),
            jax.ShapeDtypeStruct((n_dt, t), F32),
        ],
        compiler_params=_cparams(("parallel",)),
        name="norm_dt",
    )(x2d, norm_w, w_dt_hi, w_dt_lo)


def _inproj_kernel(h_ref, w_ref, o_ref):
    o_ref[0] = jnp.dot(h_ref[...], w_ref[0], preferred_element_type=F32).astype(o_ref.dtype)


def _in_proj(h, w_tiles, tm):
    t, d = h.shape
    n_tiles, _, tn = w_tiles.shape
    return pl.pallas_call(
        _inproj_kernel,
        grid=(t // tm, n_tiles),
        in_specs=[
            pl.BlockSpec((tm, d), lambda m, n: (m, 0)),
            pl.BlockSpec((1, d, tn), lambda m, n: (n, 0, 0)),
        ],
        out_specs=pl.BlockSpec((1, tm, tn), lambda m, n: (n, m, 0)),
        out_shape=jax.ShapeDtypeStruct((n_tiles, t, tn), BF16),
        compiler_params=_cparams(("parallel", "arbitrary")),
        name="in_proj",
    )(h, w_tiles)


def _fill_halo_scratch(scr, cur, prev, nxt, lt):
    l = pl.program_id(1)
    last = pl.num_programs(1) - 1
    scr[HALO:HALO + lt, :] = cur
    scr[0:HALO, :] = jnp.where(l > 0, prev, 0.0)
    scr[HALO + lt:HALO + lt + HALO, :] = jnp.where(l < last, nxt, 0.0)


def _conv_ssm_kernel(cur_ref, prev_ref, next_ref, w_ref, b_ref, o_ref, scr, *, lt, rc):
    _fill_halo_scratch(scr, cur_ref[0, 0].astype(F32), prev_ref[0, 0].astype(F32), next_ref[0, 0].astype(F32), lt)
    pad = SSM_CONV // 2
    w = w_ref[...]
    bias = b_ref[...]

    def body(i, carry):
        r0 = pl.multiple_of(i * rc, rc)
        win = scr[pl.ds(r0 + (HALO - SUBLANES), rc + 2 * SUBLANES), :]
        acc = jnp.zeros((rc, scr.shape[1]), F32)
        for k in range(SSM_CONV):
            o = SUBLANES - pad + k
            acc = acc + w[k:k + 1, :] * win[o:o + rc, :]
        o_ref[0, pl.ds(r0, rc), :] = _silu(acc + bias).astype(o_ref.dtype)
        return carry

    lax.fori_loop(0, lt // rc, body, 0)


def _conv_ssm(proj4, conv_w, conv_b, tile0, lt, ct, rc=64):
    _, bsz, seq, tn = proj4.shape
    c = conv_w.shape[1]
    per = tn // ct
    hb = lt // HALO
    n_hb = seq // HALO
    kern = functools.partial(_conv_ssm_kernel, lt=lt, rc=rc)
    return pl.pallas_call(
        kern,
        grid=(bsz, seq // lt, c // ct),
        in_specs=[
            pl.BlockSpec((1, 1, lt, ct), lambda b, l, j: (tile0 + j // per, b, l, j % per)),
            pl.BlockSpec((1, 1, HALO, ct),
                         lambda b, l, j: (tile0 + j // per, b, jnp.maximum(l * hb - 1, 0), j % per)),
            pl.BlockSpec((1, 1, HALO, ct),
                         lambda b, l, j: (tile0 + j // per, b, jnp.minimum((l + 1) * hb, n_hb - 1), j % per)),
            pl.BlockSpec((SSM_CONV, ct), lambda b, l, j: (0, j)),
            pl.BlockSpec((1, ct), lambda b, l, j: (0, j)),
        ],
        out_specs=pl.BlockSpec((1, lt, ct), lambda b, l, j: (b, l, j)),
        out_shape=jax.ShapeDtypeStruct((bsz, seq, c), BF16),
        scratch_shapes=[pltpu.VMEM((lt + 2 * HALO, ct), F32)],
        compiler_params=_cparams(("parallel", "parallel", "parallel")),
        name="conv_ssm",
    )(proj4, proj4, proj4, conv_w, conv_b)


def _conformer_kernel(*refs, lt, rc, lc, n_t):
    tiles = refs[:6 * n_t]
    w_ref, b_ref, lnw_ref, lnb_ref, o_ref, scr, sh_scr, v_scr = refs[6 * n_t:]

    def glu(group):
        a = jnp.concatenate([r[0, 0].astype(F32) for r in tiles[2 * n_t * group:2 * n_t * group + n_t]], axis=1)
        g = jnp.concatenate([r[0, 0].astype(F32) for r in tiles[2 * n_t * group + n_t:2 * n_t * (group + 1)]], axis=1)
        return a * _sigmoid(g)

    _fill_halo_scratch(scr, glu(0), glu(1), glu(2), lt)
    c = scr.shape[1]
    pad = CONF_KERNEL // 2
    n_sh = lt + 2 * HALO - SUBLANES
    for r0 in range(0, n_sh, rc):
        n = min(rc, n_sh - r0)
        for j in range(c // lc):
            win = scr[r0:r0 + n + SUBLANES, j * lc:(j + 1) * lc]
            for m in range(1, SUBLANES):
                sh_scr[m - 1, r0:r0 + n, j * lc:(j + 1) * lc] = win[m:m + n, :]

    def conv_body(i, carry):
        r0 = pl.multiple_of(i * rc, rc)
        for j in range(c // lc):
            lanes = slice(j * lc, (j + 1) * lc)
            acc = jnp.zeros((rc, lc), F32)
            for m in range(SUBLANES):
                offs = [(k, HALO - pad + k - m) for k in range(CONF_KERNEL) if (HALO - pad + k) % SUBLANES == m]
                if not offs:
                    continue
                span = rc + offs[-1][1]
                if m == 0:
                    win = scr[pl.ds(r0, span), lanes]
                else:
                    win = sh_scr[m - 1, pl.ds(r0, span), lanes]
                for k, base in offs:
                    acc = acc + w_ref[k:k + 1, lanes] * win[base:base + rc, :]
            v_scr[pl.ds(r0, rc), lanes] = acc + b_ref[:, lanes]
        return carry

    lax.fori_loop(0, lt // rc, conv_body, 0)
    lnw = lnw_ref[...]
    lnb = lnb_ref[...]
    rn = 2 * SUBLANES

    def norm_body(i, carry):
        r0 = pl.multiple_of(i * rn, rn)
        v = v_scr[pl.ds(r0, rn), :]
        mu = jnp.mean(v, axis=-1, keepdims=True)
        vc = v - mu
        var = jnp.mean(vc * vc, axis=-1, keepdims=True)
        y = vc * lax.rsqrt(var + EPS) * lnw + lnb
        o_ref[0, pl.ds(r0, rn), :] = _silu(y).astype(o_ref.dtype)
        return carry

    lax.fori_loop(0, lt // rn, norm_body, 0, unroll=4)


def _conformer(proj4, dw_w, dw_b, ln_w, ln_b, tile_a, tile_g, lt, rc=64, lc=128):
    _, bsz, seq, tn = proj4.shape
    c = dw_w.shape[1]
    n_t = c // tn
    hb = lt // HALO
    n_hb = seq // HALO
    kern = functools.partial(_conformer_kernel, lt=lt, rc=rc, lc=lc, n_t=n_t)

    def cur(j):
        return pl.BlockSpec((1, 1, lt, tn), lambda b, l: (j, b, l, 0))

    def prev(j):
        return pl.BlockSpec((1, 1, HALO, tn), lambda b, l: (j, b, jnp.maximum(l * hb - 1, 0), 0))

    def nxt(j):
        return pl.BlockSpec((1, 1, HALO, tn), lambda b, l: (j, b, jnp.minimum((l + 1) * hb, n_hb - 1), 0))

    tile_specs = [mk(t0 + j) for mk in (cur, prev, nxt) for t0 in (tile_a, tile_g) for j in range(n_t)]
    vec = pl.BlockSpec((1, c), lambda b, l: (0, 0))
    return pl.pallas_call(
        kern,
        grid=(bsz, seq // lt),
        in_specs=tile_specs + [pl.BlockSpec((CONF_KERNEL, c), lambda b, l: (0, 0)), vec, vec, vec],
        out_specs=pl.BlockSpec((1, lt, c), lambda b, l: (b, l, 0)),
        out_shape=jax.ShapeDtypeStruct((bsz, seq, c), BF16),
        scratch_shapes=[pltpu.VMEM((lt + 2 * HALO, c), F32),
                        pltpu.VMEM((SUBLANES - 1, lt + 2 * HALO, c), F32),
                        pltpu.VMEM((lt, c), F32)],
        compiler_params=_cparams(("parallel", "parallel")),
        name="conformer",
    )(*([proj4] * len(tile_specs)), dw_w, dw_b, ln_w, ln_b)


def _split3(v):
    hi = v.astype(BF16).astype(F32)
    r = v - hi
    mid = r.astype(BF16).astype(F32)
    lo = (r - mid).astype(BF16).astype(F32)
    return hi, mid, lo


def _ssd_kernel(*refs, reverse, final, n_heads, nbs):
    xs_ref, bm_ref, cm_ref, dt_ref = refs[:4]
    dtt_refs = refs[4:4 + nbs]
    consts = refs[4 + nbs:11 + nbs]
    if final:
        yf_ref = refs[11 + nbs]
        z_refs = refs[12 + nbs:-4]
        dsk_ref, nw_ref, o_ref, state_scr = refs[-4:]
    else:
        yf_ref, z_refs, dsk_ref, nw_ref = None, (), None, None
        o_ref, state_scr = refs[-2:]

    @pl.when(pl.program_id(1) == 0)
    def _():
        state_scr[...] = jnp.zeros_like(state_scr)

    for s in range(nbs):
        _ssd_chunk(s, xs_ref, bm_ref, cm_ref, dt_ref, dtt_refs[s], consts, yf_ref, z_refs, dsk_ref, nw_ref, o_ref,
                   state_scr, reverse=reverse, final=final, n_heads=n_heads)


def _ssd_chunk(s, xs_ref, bm_ref, cm_ref, dt_ref, dtt_ref, consts, yf_ref, z_refs, dsk_ref, nw_ref, o_ref,
               state_scr, *, reverse, final, n_heads):
    dtb_ref, dtbt_ref, alog_ref, alogt_ref, tri_ref, trit_ref, e3_ref = consts
    q = CHUNK
    hg = n_heads
    d_ssm = n_heads * HEAD_DIM
    gw = d_ssm // SSM_GROUPS

    lane = lax.broadcasted_iota(jnp.int32, (q, LANES), 1)
    sub = lax.broadcasted_iota(jnp.int32, (LANES, q), 0)

    def sel3_lanes(v):
        hi, mid, lo = _split3(v)
        return jnp.where(lane < hg, hi, jnp.where(lane < 2 * hg, mid, jnp.where(lane < 3 * hg, lo, 0.0))).astype(BF16)

    def sel3_rows(v):
        hi, mid, lo = _split3(v)
        return jnp.where(sub < hg, hi, jnp.where(sub < 2 * hg, mid, jnp.where(sub < 3 * hg, lo, 0.0))).astype(BF16)

    a_row = -jnp.exp(alog_ref[...])
    dt = _softplus(dt_ref[s] + dtb_ref[...])
    da = dt * a_row
    a_col = -jnp.exp(alogt_ref[...])
    dat = _softplus(dtt_ref[...] + dtbt_ref[...]) * a_col

    tri = tri_ref[...]
    cum3 = jnp.dot(tri, sel3_lanes(da), preferred_element_type=F32)
    cum = cum3
    for r in (1, 2, 3):
        cum = cum + pltpu.roll(cum3, r * hg, axis=1)
    cumt3 = jnp.dot(sel3_rows(dat), trit_ref[...], preferred_element_type=F32)
    cumt = cumt3[0:hg] + cumt3[hg:2 * hg] + cumt3[2 * hg:3 * hg]

    tot_row = 0 if reverse else q - 1
    total = cum[tot_row:tot_row + 1, :]

    e3 = e3_ref[...]
    lhs = jnp.concatenate(
        [sel3_lanes(dt), sel3_lanes(jnp.exp(cum)), sel3_lanes(jnp.exp(total - cum)),
         sel3_lanes(jnp.broadcast_to(jnp.exp(total), (q, LANES)))], axis=0)
    ex = jnp.dot(lhs, e3, preferred_element_type=F32)
    dt_x = ex[0:q]
    ecum_x = ex[q:2 * q]
    edte_x = ex[2 * q:3 * q]
    cdec_x = ex[3 * q:3 * q + 1]

    xs = xs_ref[s].astype(F32)
    xdt = xs * dt_x
    xdt_b = xdt.astype(BF16)
    xdte_b = (xdt * edte_x).astype(BF16)
    bm = bm_ref[s]
    cm = cm_ref[s]

    li = lax.broadcasted_iota(jnp.int32, (q, q), 0)
    si = lax.broadcasted_iota(jnp.int32, (q, q), 1)
    mask = (li <= si) if reverse else (li >= si)
    lane_lo = lax.broadcasted_iota(jnp.int32, (q, LANES), 1) < HEAD_DIM

    hpg = n_heads // SSM_GROUPS
    y_parts = []
    for g in range(SSM_GROUPS):
        bg = bm[:, g * D_STATE:(g + 1) * D_STATE]
        cg = cm[:, g * D_STATE:(g + 1) * D_STATE]
        cb = lax.dot_general(cg, bg, (((1,), (1,)), ((), ())), preferred_element_type=F32)
        st = state_scr[s, :, g * gw:(g + 1) * gw]
        y_off = jnp.dot(cg, st.astype(BF16), preferred_element_type=F32) * ecum_x[:, g * gw:(g + 1) * gw]
        diag = []
        for pair in range(hpg // 2):
            h0 = g * hpg + 2 * pair
            ls = []
            for h in (h0, h0 + 1):
                seg = cum[:, h:h + 1] - cumt[h:h + 1, :]
                ls.append((cb * jnp.exp(jnp.where(mask, seg, -jnp.inf))).astype(BF16))
            l2 = jnp.concatenate(ls, axis=1)
            xp = xdt_b[:, h0 * HEAD_DIM:(h0 + 2) * HEAD_DIM]
            zero = jnp.zeros_like(xp)
            r2 = jnp.concatenate([jnp.where(lane_lo, xp, zero), jnp.where(lane_lo, zero, xp)], axis=0)
            diag.append(jnp.dot(l2, r2, preferred_element_type=F32))
        y_parts.append(jnp.concatenate(diag, axis=1) + y_off)
        contrib = lax.dot_general(bg, xdte_b[:, g * gw:(g + 1) * gw], (((0,), (0,)), ((), ())),
                                  preferred_element_type=F32)
        state_scr[s, :, g * gw:(g + 1) * gw] = st * cdec_x[:, g * gw:(g + 1) * gw] + contrib
    y = jnp.concatenate(y_parts, axis=1)

    if not final:
        o_ref[s] = y
    else:
        y = y + yf_ref[s] + dsk_ref[...] * xs
        y = y * _silu(jnp.concatenate([r[0, s].astype(F32) for r in z_refs], axis=1))
        outs = []
        for g in range(SSM_GROUPS):
            yg = y[:, g * gw:(g + 1) * gw]
            ms = jnp.mean(yg * yg, axis=-1, keepdims=True)
            outs.append(yg * lax.rsqrt(ms + EPS))
        o_ref[s] = (jnp.concatenate(outs, axis=1) * nw_ref[...]).astype(o_ref.dtype)


def _ssd(xbc3, dt2, dtt2, dtb, dtbt, alog, alogt, e3, *, reverse, n_heads, final_args=None):
    bsz, seq, _ = xbc3.shape
    d_ssm = n_heads * HEAD_DIM
    gn = SSM_GROUPS * D_STATE
    nc = seq // CHUNK
    q = CHUNK
    d = 1 if reverse else 0
    final = final_args is not None

    def cidx(c):
        return (nc - 1 - c) if reverse else c

    li = jnp.arange(q)[:, None]
    ji = jnp.arange(q)[None, :]
    tri = ((ji >= li) if reverse else (ji <= li)).astype(BF16)
    trit = tri.T

    nbs = 2 if bsz % 2 == 0 else 1
    dt3 = dt2.reshape(bsz, seq, dt2.shape[1])
    const = lambda shape: pl.BlockSpec(shape, lambda b, c: (0,) * len(shape))
    in_specs = [
        pl.BlockSpec((nbs, q, d_ssm), lambda b, c: (b, cidx(c), 0)),
        pl.BlockSpec((nbs, q, gn), lambda b, c: (b, cidx(c), d_ssm // gn)),
        pl.BlockSpec((nbs, q, gn), lambda b, c: (b, cidx(c), d_ssm // gn + 1)),
        pl.BlockSpec((nbs, q, LANES), lambda b, c: (b, cidx(c), d)),
    ]
    in_specs += [pl.BlockSpec((LANES, q), lambda b, c, s=s: (d, (b * nbs + s) * nc + cidx(c))) for s in range(nbs)]
    in_specs += [const((1, LANES)), const((LANES, q)), const((1, LANES)), const((LANES, q)),
                 const((q, q)), const((q, q)), const((LANES, d_ssm))]
    args = [xbc3, xbc3, xbc3, dt3] + [dtt2] * nbs + [dtb, dtbt, alog, alogt, tri, trit, e3]
    if final:
        yf3, proj4, dskip_x, norm_w = final_args
        tn = proj4.shape[3]
        z_specs = [pl.BlockSpec((1, nbs, q, tn), lambda b, c, j=j: (j, b, cidx(c), 0)) for j in range(d_ssm // tn)]
        in_specs += [pl.BlockSpec((nbs, q, d_ssm), lambda b, c: (b, cidx(c), 0))] + z_specs
        in_specs += [const((1, d_ssm)), const((1, d_ssm))]
        args += [yf3] + [proj4] * len(z_specs) + [dskip_x, norm_w]
    kern = functools.partial(_ssd_kernel, reverse=reverse, final=final, n_heads=n_heads, nbs=nbs)
    return pl.pallas_call(
        kern,
        grid=(bsz // nbs, nc),
        in_specs=in_specs,
        out_specs=pl.BlockSpec((nbs, q, d_ssm), lambda b, c: (b, cidx(c), 0)),
        out_shape=jax.ShapeDtypeStruct((bsz, seq, d_ssm), BF16 if final else F32),
        scratch_shapes=[pltpu.VMEM((nbs, D_STATE, d_ssm), F32)],
        compiler_params=_cparams(("parallel", "arbitrary")),
        name="ssd_bwd" if reverse else "ssd_fwd",
    )(*args)


def _outproj_kernel(ys_ref, yc_ref, wa_ref, wb_ref, x_ref, nw_ref, wr_hi_ref, wr_lo_ref, br_ref, x1_ref, lg_ref):
    acc = jnp.dot(ys_ref[...], wa_ref[...], preferred_element_type=F32)
    acc = acc + jnp.dot(yc_ref[...], wb_ref[...], preferred_element_type=F32)
    x1 = x_ref[...] + acc
    x1_ref[...] = x1
    ms = jnp.mean(x1 * x1, axis=-1, keepdims=True)
    h_hi, h_lo = _split2(x1 * lax.rsqrt(ms + EPS) * nw_ref[...])
    lg_ref[...] = _dot_split(h_hi, h_lo, wr_hi_ref, wr_lo_ref) + br_ref[...]


def _out_proj(y_ssm, y_conf, w_out, x2d, norm_w, wr_hi, wr_lo, b_router, tm):
    t, d = x2d.shape
    ka = y_ssm.shape[1]
    ne = wr_hi.shape[1]
    resident = lambda shape, idx: pl.BlockSpec(shape, lambda m: idx, pipeline_mode=pl.Buffered(1))
    return pl.pallas_call(
        _outproj_kernel,
        grid=(t // tm,),
        in_specs=[
            pl.BlockSpec((tm, ka), lambda m: (m, 0)),
            pl.BlockSpec((tm, ka), lambda m: (m, 0)),
            resident((ka, d), (0, 0)),
            resident((ka, d), (1, 0)),
            pl.BlockSpec((tm, d), lambda m: (m, 0)),
            pl.BlockSpec((1, d), lambda m: (0, 0)),
            resident((d, ne), (0, 0)),
            resident((d, ne), (0, 0)),
            pl.BlockSpec((1, ne), lambda m: (0, 0)),
        ],
        out_specs=[
            pl.BlockSpec((tm, d), lambda m: (m, 0)),
            pl.BlockSpec((tm, ne), lambda m: (m, 0)),
        ],
        out_shape=[
            jax.ShapeDtypeStruct((t, d), F32),
            jax.ShapeDtypeStruct((t, ne), F32),
        ],
        compiler_params=_cparams(("parallel",)),
        name="out_proj",
    )(y_ssm, y_conf, w_out, w_out, x2d, norm_w, wr_hi, wr_lo, b_router)


def _router_kernel(lg_ref, tri_ref, ir_ref, gate_ref, cnt_ref, run_scr, *, n_experts):
    @pl.when(pl.program_id(0) == 0)
    def _():
        run_scr[...] = jnp.zeros_like(run_scr)

    tr = lg_ref.shape[0]
    lane = lax.broadcasted_iota(jnp.int32, (tr, LANES), 1)
    lane_f = lane.astype(F32)
    lg = jnp.where(lane < n_experts, lg_ref[...], -jnp.inf)
    vals, ids, hots = [], [], []
    for _ in range(TOP_K):
        m = jnp.max(lg, axis=1, keepdims=True)
        idx = jnp.min(jnp.where(lg == m, lane_f, float(LANES)), axis=1, keepdims=True)
        hot = lane_f == idx
        vals.append(m)
        ids.append(idx)
        hots.append(hot)
        lg = jnp.where(hot, -jnp.inf, lg)
    es = [jnp.exp(v - vals[0]) for v in vals]
    den = es[0]
    for e in es[1:]:
        den = den + e
    hot_any = hots[0].astype(F32)
    for hot in hots[1:]:
        hot_any = hot_any + hot.astype(F32)
    before = jnp.dot(tri_ref[...], hot_any.astype(BF16), preferred_element_type=F32) + run_scr[...]
    run_scr[...] = run_scr[...] + jnp.sum(hot_any, axis=0, keepdims=True)
    cnt_ref[...] = run_scr[...].astype(jnp.int32)
    out_i = jnp.zeros((tr, LANES), F32)
    out_g = jnp.zeros((tr, LANES), F32)
    for k in range(TOP_K):
        rank = jnp.sum(jnp.where(hots[k], before, 0.0), axis=1, keepdims=True)
        out_i = jnp.where(lane == k, ids[k], out_i)
        out_i = jnp.where(lane == TOP_K + k, rank, out_i)
        out_g = jnp.where(lane == k, es[k] / den, out_g)
    ir_ref[...] = out_i.astype(jnp.int32)
    gate_ref[...] = out_g


def _router(logits, n_experts, tr):
    t = logits.shape[0]
    tri = (jnp.arange(tr)[None, :] < jnp.arange(tr)[:, None]).astype(BF16)
    kern = functools.partial(_router_kernel, n_experts=n_experts)
    return pl.pallas_call(
        kern,
        grid=(t // tr,),
        in_specs=[pl.BlockSpec((tr, LANES), lambda i: (i, 0)), pl.BlockSpec((tr, tr), lambda i: (0, 0))],
        out_specs=[pl.BlockSpec((tr, LANES), lambda i: (i, 0)), pl.BlockSpec((tr, LANES), lambda i: (i, 0)),
                   pl.BlockSpec((1, LANES), lambda i: (0, 0))],
        out_shape=[jax.ShapeDtypeStruct((t, LANES), jnp.int32), jax.ShapeDtypeStruct((t, LANES), F32),
                   jax.ShapeDtypeStruct((1, LANES), jnp.int32)],
        scratch_shapes=[pltpu.VMEM((1, LANES), F32)],
        compiler_params=_cparams(("arbitrary",)),
        name="moe_router",
    )(logits, tri)


_HI16 = 0xFFFF0000


def _pack_bf16_pair(a, b):
    au = lax.bitcast_convert_type(a.astype(BF16).astype(F32), jnp.uint32)
    bu = lax.bitcast_convert_type(b.astype(BF16).astype(F32), jnp.uint32)
    return (au & jnp.uint32(_HI16)) | lax.shift_right_logical(bu, jnp.uint32(16))


def _unpack_bf16_pair(w):
    a = lax.bitcast_convert_type(w & jnp.uint32(_HI16), F32).astype(BF16)
    b = lax.bitcast_convert_type(lax.shift_left(w, jnp.uint32(16)), F32).astype(BF16)
    return a, b


def _dispatch_kernel(zs_ref, idx_ref, x1_ref, nw_ref, xs_hbm, zero_scr, pk, sem, zsem, *, tm, bm, n_experts):
    i = pl.program_id(0)
    last = pl.num_programs(0) - 1
    slot = lax.rem(i, 2)
    half = x1_ref.shape[1] // 2
    rt = ROW_TILE

    def zero_copy(e):
        dst = xs_hbm.at[pl.ds(pl.multiple_of(zs_ref[e], rt * bm), rt * bm), :]
        return pltpu.make_async_copy(zero_scr, dst, zsem)

    @pl.when(i == 0)
    def _():
        zero_scr[...] = jnp.zeros_like(zero_scr)
        for e in range(n_experts):
            @pl.when(zs_ref[e] >= 0)
            def _():
                zero_copy(e).start()
        for e in range(n_experts):
            @pl.when(zs_ref[e] >= 0)
            def _():
                zero_copy(e).wait()

    def wait_rows(s):
        rows = xs_hbm.at[pl.ds(0, rt * TOP_K * tm), :]
        pltpu.make_async_copy(rows, rows, sem.at[s]).wait()

    @pl.when(i >= 2)
    def _():
        wait_rows(slot)

    x = x1_ref[...]
    ms = jnp.mean(x * x, axis=-1, keepdims=True)
    h = x * lax.rsqrt(ms + EPS) * nw_ref[...]
    packed = _pack_bf16_pair(h[:, :half], h[:, half:])
    for c in range(rt):
        pk[slot, pl.ds(c, tm, stride=rt), :] = packed[:, c * LANES:(c + 1) * LANES]

    def body(t, carry):
        src = pk.at[slot, pl.ds(pl.multiple_of(t * rt, rt), rt), :]
        for k in range(TOP_K):
            r = pl.multiple_of(idx_ref[0, 0, t * TOP_K + k], rt)
            pltpu.make_async_copy(src, xs_hbm.at[pl.ds(r, rt), :], sem.at[slot]).start(priority=k % 2)
        return carry

    lax.fori_loop(0, tm, body, 0, unroll=4)

    @pl.when(i == last)
    def _():
        wait_rows(slot)

    @pl.when(jnp.logical_and(i == last, i >= 1))
    def _():
        wait_rows(1 - slot)


def _dispatch(dest, zero_start, x1, norm_w, n_rows, tm, bm):
    t, d = x1.shape
    assert d // 2 == ROW_TILE * LANES
    nt = t // tm
    n_experts = zero_start.shape[0]
    idx3 = (dest * ROW_TILE).reshape(nt, 1, tm * TOP_K)
    zero_start = zero_start * ROW_TILE
    kern = functools.partial(_dispatch_kernel, tm=tm, bm=bm, n_experts=n_experts)
    return pl.pallas_call(
        kern,
        grid_spec=pltpu.PrefetchScalarGridSpec(
            num_scalar_prefetch=1,
            grid=(nt,),
            in_specs=[
                pl.BlockSpec((1, 1, TOP_K * tm), lambda i, zs: (i, 0, 0), memory_space=pltpu.SMEM),
                pl.BlockSpec((tm, d), lambda i, zs: (i, 0)),
                pl.BlockSpec((1, d), lambda i, zs: (0, 0)),
            ],
            out_specs=pl.BlockSpec(memory_space=pl.ANY),
            scratch_shapes=[pltpu.VMEM((ROW_TILE * bm, LANES), jnp.uint32),
                            pltpu.VMEM((2, ROW_TILE * tm, LANES), jnp.uint32),
                            pltpu.SemaphoreType.DMA((2,)), pltpu.SemaphoreType.DMA(())],
        ),
        out_shape=jax.ShapeDtypeStruct((ROW_TILE * n_rows, LANES), jnp.uint32),
        compiler_params=_cparams(("arbitrary",)),
        name="moe_dispatch",
    )(zero_start, idx3, x1, norm_w)


def _for_row_bucket(b, nu_ref, bq_ref, bm, compute):
    for quarters in range(1, MOE_BUCKETS + 1):
        @pl.when(jnp.logical_and(b < nu_ref[0], bq_ref[b] == quarters))
        def _():
            compute(quarters * (bm // MOE_BUCKETS))


def _expert_weights(b, tile, n_tiles, tabs, copies):
    be_ref, nu_ref, first_ref, run_ref, next_ref, nruns_ref = tabs
    run = run_ref[b]
    slot = lax.rem(tile * nruns_ref[0] + run, 2)
    first = jnp.logical_and(b < nu_ref[0], first_ref[b] == 1)
    nxt = next_ref[b]

    @pl.when(jnp.logical_and(first, jnp.logical_and(tile == 0, run == 0)))
    def _():
        for c in copies(be_ref[b], tile, slot):
            c.start()

    @pl.when(first)
    def _():
        for c in copies(be_ref[b], tile, slot):
            c.wait()

    @pl.when(jnp.logical_and(first, nxt >= 0))
    def _():
        for c in copies(nxt, tile, 1 - slot):
            c.start()

    @pl.when(jnp.logical_and(first, jnp.logical_and(nxt < 0, tile + 1 < n_tiles)))
    def _():
        for c in copies(be_ref[0], tile + 1, 1 - slot):
            c.start()

    return slot


def _moe_up_kernel(be_ref, nu_ref, bq_ref, first_ref, run_ref, next_ref, nruns_ref, x_ref, wg_hbm, bg_ref, wu_hbm,
                   bu_ref, o_ref, wbuf, wsem):
    bm = x_ref.shape[0] // ROW_TILE
    tn = o_ref.shape[1]

    def copies(e, tile, s):
        cols = pl.ds(pl.multiple_of(tile * tn, tn), tn)
        return [pltpu.make_async_copy(wg_hbm.at[e, :, cols], wbuf.at[s, 0], wsem.at[s, 0]),
                pltpu.make_async_copy(wu_hbm.at[e, :, cols], wbuf.at[s, 1], wsem.at[s, 1])]

    slot = _expert_weights(pl.program_id(1), pl.program_id(0), pl.num_programs(0),
                           (be_ref, nu_ref, first_ref, run_ref, next_ref, nruns_ref), copies)

    def compute(nv):
        words = jnp.concatenate([x_ref[pl.ds(c, nv, stride=ROW_TILE), :] for c in range(ROW_TILE)], axis=1)
        xa, xb = _unpack_bf16_pair(words)
        half = xa.shape[1]
        wg = wbuf[slot, 0].astype(BF16)
        wu = wbuf[slot, 1].astype(BF16)
        gate = (jnp.dot(xa, wg[:half], preferred_element_type=F32)
                + jnp.dot(xb, wg[half:], preferred_element_type=F32) + bg_ref[0])
        up = (jnp.dot(xa, wu[:half], preferred_element_type=F32)
              + jnp.dot(xb, wu[half:], preferred_element_type=F32) + bu_ref[0])
        gate = jnp.minimum(gate, SWIGLU_LIMIT)
        up = jnp.clip(up, -SWIGLU_LIMIT, SWIGLU_LIMIT)
        act = (up + 1.0) * gate * _sigmoid(SWIGLU_ALPHA * gate)
        o_ref[:nv, :] = act.astype(o_ref.dtype)

    _for_row_bucket(pl.program_id(1), nu_ref, bq_ref, bm, compute)


def _moe_down_kernel(be_ref, nu_ref, bq_ref, first_ref, run_ref, next_ref, nruns_ref, a_ref, wd_hbm, bd_ref, o_ref,
                     wbuf, wsem):
    def copies(e, tile, s):
        return [pltpu.make_async_copy(wd_hbm.at[e], wbuf.at[s], wsem.at[s])]

    slot = _expert_weights(pl.program_id(0), 0, 1, (be_ref, nu_ref, first_ref, run_ref, next_ref, nruns_ref), copies)

    def compute(nv):
        y = jnp.dot(a_ref[:nv, :], wbuf[slot].astype(BF16), preferred_element_type=F32) + bd_ref[0]
        d = y.shape[1]
        packed = _pack_bf16_pair(y[:, :d // 2], y[:, d // 2:])
        for c in range(ROW_TILE):
            o_ref[pl.ds(c, nv, stride=ROW_TILE), :] = packed[:, c * LANES:(c + 1) * LANES]

    _for_row_bucket(pl.program_id(0), nu_ref, bq_ref, a_ref.shape[0], compute)


def _blk(b, nu_ref):
    return jnp.minimum(b, nu_ref[0] - 1)


def _moe_up(tables, xs, w_gate, b_gate3, w_up, b_up3, bm, tn):
    n_rows = xs.shape[0] // ROW_TILE
    d, dff = w_gate.shape[1], w_gate.shape[2]
    nb = n_rows // bm
    hbm = pl.BlockSpec(memory_space=pl.ANY)
    bspec = pl.BlockSpec((1, 1, tn), lambda n, b, be, nu, *_: (be[_blk(b, nu)], 0, n))
    return pl.pallas_call(
        _moe_up_kernel,
        grid_spec=pltpu.PrefetchScalarGridSpec(
            num_scalar_prefetch=len(tables),
            grid=(dff // tn, nb),
            in_specs=[pl.BlockSpec((ROW_TILE * bm, LANES), lambda n, b, be, nu, *_: (_blk(b, nu), 0)),
                      hbm, bspec, hbm, bspec],
            out_specs=pl.BlockSpec((bm, tn), lambda n, b, be, nu, *_: (_blk(b, nu), n)),
            scratch_shapes=[pltpu.VMEM((2, 2, d, tn), F32), pltpu.SemaphoreType.DMA((2, 2))],
        ),
        out_shape=jax.ShapeDtypeStruct((n_rows, dff), BF16),
        compiler_params=_cparams(("arbitrary", "arbitrary")),
        name="moe_up",
    )(*tables, xs, w_gate, b_gate3, w_up, b_up3)


def _moe_down(tables, act, w_down, b_down3, bm):
    n_rows, dff = act.shape
    d = w_down.shape[2]
    assert d // 2 == ROW_TILE * LANES
    nb = n_rows // bm
    return pl.pallas_call(
        _moe_down_kernel,
        grid_spec=pltpu.PrefetchScalarGridSpec(
            num_scalar_prefetch=len(tables),
            grid=(nb,),
            in_specs=[
                pl.BlockSpec((bm, dff), lambda b, be, nu, *_: (_blk(b, nu), 0)),
                pl.BlockSpec(memory_space=pl.ANY),
                pl.BlockSpec((1, 1, d), lambda b, be, nu, *_: (be[_blk(b, nu)], 0, 0)),
            ],
            out_specs=pl.BlockSpec((ROW_TILE * bm, LANES), lambda b, be, nu, *_: (_blk(b, nu), 0)),
            scratch_shapes=[pltpu.VMEM((2, dff, d), F32), pltpu.SemaphoreType.DMA((2,))],
        ),
        out_shape=jax.ShapeDtypeStruct((ROW_TILE * n_rows, LANES), jnp.uint32),
        compiler_params=_cparams(("arbitrary",)),
        name="moe_down",
    )(*tables, act, w_down, b_down3)


def _combine_kernel(idx_ref, idxn_ref, y_hbm, x1_ref, g_ref, nw_ref, o_ref, buf, sem, *, tm):
    i = pl.program_id(0)
    slot = lax.rem(i, 2)
    rt = ROW_TILE

    def start_tile(ids_ref, s):
        def body(t, carry):
            for k in range(TOP_K):
                r = pl.multiple_of(ids_ref[0, 0, t * TOP_K + k], rt)
                dst = buf.at[s, pl.ds(pl.multiple_of((k * tm + t) * rt, rt), rt), :]
                pltpu.make_async_copy(y_hbm.at[pl.ds(r, rt), :], dst, sem.at[s]).start(priority=k % 2)
            return carry

        lax.fori_loop(0, tm, body, 0, unroll=4)

    @pl.when(i == 0)
    def _():
        start_tile(idx_ref, 0)

    @pl.when(i + 1 < pl.num_programs(0))
    def _():
        start_tile(idxn_ref, 1 - slot)

    pltpu.make_async_copy(y_hbm.at[pl.ds(0, rt * TOP_K * tm), :], buf.at[slot], sem.at[slot]).wait()
    acc = x1_ref[...]
    g = g_ref[...]
    for k in range(TOP_K):
        words = jnp.concatenate(
            [buf[slot, pl.ds(rt * k * tm + c, tm, stride=rt), :] for c in range(rt)], axis=1)
        y = jnp.concatenate([lax.bitcast_convert_type(words & jnp.uint32(_HI16), F32),
                             lax.bitcast_convert_type(lax.shift_left(words, jnp.uint32(16)), F32)], axis=1)
        acc = acc + g[:, k:k + 1] * y
    ms = jnp.mean(acc * acc, axis=-1, keepdims=True)
    o_ref[...] = acc * lax.rsqrt(ms + EPS) * nw_ref[...]


def _combine(dest, y_rows, x1, gates, norm_w, tm):
    t, d = x1.shape
    nt = t // tm
    idx3 = (dest * ROW_TILE).reshape(nt, 1, tm * TOP_K)
    kern = functools.partial(_combine_kernel, tm=tm)
    return pl.pallas_call(
        kern,
        grid=(nt,),
        in_specs=[
            pl.BlockSpec((1, 1, TOP_K * tm), lambda i: (i, 0, 0), memory_space=pltpu.SMEM),
            pl.BlockSpec((1, 1, TOP_K * tm), lambda i: (jnp.minimum(i + 1, nt - 1), 0, 0), memory_space=pltpu.SMEM),
            pl.BlockSpec(memory_space=pl.ANY),
            pl.BlockSpec((tm, d), lambda i: (i, 0)),
            pl.BlockSpec((tm, TOP_K), lambda i: (i, 0)),
            pl.BlockSpec((1, d), lambda i: (0, 0)),
        ],
        out_specs=pl.BlockSpec((tm, d), lambda i: (i, 0)),
        out_shape=jax.ShapeDtypeStruct((t, d), F32),
        scratch_shapes=[pltpu.VMEM((2, ROW_TILE * TOP_K * tm, LANES), jnp.uint32), pltpu.SemaphoreType.DMA((2,))],
        compiler_params=_cparams(("arbitrary",)),
        name="moe_combine",
    )(idx3, idx3, y_rows, x1, gates, norm_w)


def _route_tables(ids_ranks, counts, n_experts, bm, n_blocks):
    top_idx = ids_ranks[:, :TOP_K]
    rank = ids_ranks[:, TOP_K:2 * TOP_K]
    cnt = counts[0, :n_experts]
    padded = ((cnt + bm - 1) // bm) * bm
    padded_end = jnp.cumsum(padded)
    padded_start = padded_end - padded
    hot = top_idx[:, :, None] == jnp.arange(n_experts, dtype=jnp.int32)[None, None, :]
    dest = rank + jnp.sum(jnp.where(hot, padded_start[None, None, :], 0), axis=-1)
    n_used = (padded_end[-1] // bm).astype(jnp.int32)
    block_start = jnp.arange(n_blocks, dtype=jnp.int32) * bm
    block_expert = jnp.minimum(jnp.sum(padded_end[None, :] <= block_start[:, None], axis=1), n_experts - 1)
    zero_start = jnp.where(cnt > 0, padded_end - bm, -1).astype(jnp.int32)
    hot_b = block_expert[:, None] == jnp.arange(n_experts, dtype=block_expert.dtype)[None, :]
    group_end = jnp.sum(jnp.where(hot_b, (padded_start + cnt)[None, :], 0), axis=1)
    rows = jnp.clip(group_end - block_start, 0, bm)
    q = bm // MOE_BUCKETS
    block_quarters = jnp.clip((rows + q - 1) // q, 1, MOE_BUCKETS).astype(jnp.int32)
    block_expert = block_expert.astype(jnp.int32)
    ar = jnp.arange(n_blocks, dtype=jnp.int32)
    prev_e = jnp.concatenate([jnp.full((1,), -1, jnp.int32), block_expert[:-1]])
    first = jnp.logical_and(ar < n_used, block_expert != prev_e)
    run_id = (jnp.cumsum(first.astype(jnp.int32)) - 1).astype(jnp.int32)
    cand = jnp.where(first, ar, n_blocks)
    later = jnp.concatenate([lax.cummin(cand[::-1])[::-1][1:], jnp.full((1,), n_blocks, jnp.int32)])
    run_next = jnp.where(later < n_blocks, block_expert[jnp.minimum(later, n_blocks - 1)], -1).astype(jnp.int32)
    tables = (block_expert, n_used.reshape(1), block_quarters, first.astype(jnp.int32), run_id, run_next,
              jnp.sum(first.astype(jnp.int32)).reshape(1))
    return dest.astype(jnp.int32), zero_start, tables


def _pick(n, candidates):
    for c in candidates:
        if n % c == 0:
            return c
    return n


def _layer(x, norm_mix_w, w_in, conv_ssm_w, conv_ssm_b, dt_bias_fwd, dt_bias_bwd, a_log_fwd, a_log_bwd,
           d_skip, ssm_norm_w, conf_dw_w, conf_dw_b, conf_ln_w, conf_ln_b, w_out, norm_ffn_w, w_router,
           b_router, w_gate, b_gate, w_up, b_up, w_down, b_down):
    bsz, seq, d = x.shape
    t = bsz * seq
    n_heads = dt_bias_fwd.shape[0]
    d_ssm = n_heads * HEAD_DIM
    d_xbc = conv_ssm_w.shape[1]
    d_conf = conf_dw_w.shape[1]
    n_experts = w_gate.shape[0]
    row = lambda v: v.reshape(1, -1).astype(F32)

    tn = PROJ_TILE
    c_dt = d_ssm + d_xbc
    c_conf = c_dt + 2 * n_heads
    starts = ([s for s in range(0, d_ssm, tn)] + [c_conf + s for s in range(0, 2 * d_conf, tn)]
              + [d_ssm + s for s in range(0, d_xbc, tn)])
    w_tiles = jnp.stack([w_in[:, s:s + tn] for s in starts]).astype(BF16)
    wf = w_in[:, c_dt:c_dt + n_heads]
    wb = w_in[:, c_dt + n_heads:c_dt + 2 * n_heads]
    zpad = jnp.zeros((d, LANES - 3 * n_heads), F32)
    w_dt = jnp.concatenate([wf, wf, wf, zpad, wb, wb, wb, zpad], axis=1)
    rep3 = lambda v: jnp.concatenate([v, v, v, jnp.zeros((LANES - 3 * n_heads,), F32)])
    x2d = x.reshape(t, d)

    def split2(w):
        hi = w.astype(BF16)
        return hi, (w - hi.astype(F32)).astype(BF16)

    h, dt2, dtt2 = _norm_dt(x2d, row(norm_mix_w), *split2(w_dt), _pick(t, (512, 256, 128)))
    proj4 = _in_proj(h, w_tiles, _pick(t, (2048, 1024, 512, 256, 128))).reshape(-1, bsz, seq, tn)
    tile_conf = d_ssm // tn
    tile_xbc = (d_ssm + 2 * d_conf) // tn

    xbc = _conv_ssm(proj4, conv_ssm_w, row(conv_ssm_b), tile_xbc, _pick(seq, (1024, 512, 256, 128)), 512)
    y_conf = _conformer(proj4, conf_dw_w, row(conf_dw_b), row(conf_ln_w), row(conf_ln_b),
                        tile_conf, tile_conf + d_conf // tn, _pick(seq, (512, 256, 128)))

    e3 = ((jnp.arange(LANES)[:, None] % n_heads == jnp.arange(d_ssm)[None, :] // HEAD_DIM)
          & (jnp.arange(LANES)[:, None] < 3 * n_heads)).astype(BF16)
    prm = {}
    for name, bias, alog in (("f", dt_bias_fwd, a_log_fwd), ("b", dt_bias_bwd, a_log_bwd)):
        b3 = rep3(bias.astype(F32))
        a3 = rep3(alog.astype(F32))
        prm[name] = (b3.reshape(1, LANES), jnp.broadcast_to(b3[:, None], (LANES, CHUNK)),
                     a3.reshape(1, LANES), jnp.broadcast_to(a3[:, None], (LANES, CHUNK)))
    y_f = _ssd(xbc, dt2, dtt2, *prm["f"], e3, reverse=False, n_heads=n_heads)
    dskip_x = jnp.repeat(d_skip.astype(F32), HEAD_DIM).reshape(1, d_ssm)
    y_ssm = _ssd(xbc, dt2, dtt2, *prm["b"], e3, reverse=True, n_heads=n_heads,
                 final_args=(y_f, proj4, dskip_x, row(ssm_norm_w)))

    wr = jnp.zeros((d, LANES), F32).at[:, :n_experts].set(w_router.astype(F32))
    br = jnp.zeros((1, LANES), F32).at[0, :n_experts].set(b_router.astype(F32))
    x1, logits = _out_proj(y_ssm.reshape(t, d_ssm), y_conf.reshape(t, d_conf), w_out.astype(BF16), x2d,
                           row(norm_ffn_w), *split2(wr), br, _pick(t, (512, 256, 128)))

    bm = MOE_ROWS
    n_blocks = -(-(t * TOP_K + n_experts * (bm - 1)) // bm)
    ids_ranks, gates_x, counts = _router(logits, n_experts, _pick(t, (1024, 512, 256, 128)))
    gates = gates_x[:, :TOP_K]
    dest, zero_start, tables = _route_tables(ids_ranks, counts, n_experts, bm, n_blocks)
    xs = _dispatch(dest, zero_start, x1, row(norm_ffn_w), n_blocks * bm, _pick(t, (512, 256, 128)), bm)
    act = _moe_up(tables, xs, w_gate, b_gate[:, None, :], w_up, b_up[:, None, :], bm, 1024)
    y_rows = _moe_down(tables, act, w_down, b_down[:, None, :], bm)
    return x1, gates, dest, y_rows


def kernel(x, norm_mix_w, w_in, conv_ssm_w, conv_ssm_b, dt_bias_fwd, dt_bias_bwd, a_log_fwd, a_log_bwd, d_skip,
           ssm_norm_w, conf_dw_w, conf_dw_b, conf_ln_w, conf_ln_b, w_out, norm_ffn_w, w_router, b_router, w_gate,
           b_gate, w_up, b_up, w_down, b_down, norm_final_w):
    assert w_in.shape[0] == 1, "a single layer is supported"
    bsz, seq, d = x.shape
    t = bsz * seq
    x1, gates, dest, y_rows = _layer(
        x, norm_mix_w[0], w_in[0], conv_ssm_w[0], conv_ssm_b[0], dt_bias_fwd[0], dt_bias_bwd[0], a_log_fwd[0],
        a_log_bwd[0], d_skip[0], ssm_norm_w[0], conf_dw_w[0], conf_dw_b[0], conf_ln_w[0], conf_ln_b[0], w_out[0],
        norm_ffn_w[0], w_router[0], b_router[0], w_gate[0], b_gate[0], w_up[0], b_up[0], w_down[0], b_down[0])
    out = _combine(dest, y_rows, x1, gates, norm_final_w.reshape(1, d).astype(F32), _pick(t, (256, 128)))
    return out.reshape(bsz, seq, d)
```

```python
import functools

import jax
import jax.numpy as jnp
from jax import lax
from jax.experimental import pallas as pl
from jax.experimental.pallas import tpu as pltpu

F32 = jnp.float32
BF16 = jnp.bfloat16

EPS = 1e-5
HEAD_DIM = 64
SSM_GROUPS = 4
D_STATE = 128
CHUNK = 128
SSM_CONV = 7
CONF_KERNEL = 31
TOP_K = 4
SWIGLU_LIMIT = 7.0
SWIGLU_ALPHA = 1.702

LANES = 128
SUBLANES = 8
VMEM_LIMIT_BYTES = 56 * 1024 * 1024
HALO = 16
MOE_ROWS = 512
MOE_BUCKETS = 4
ROW_TILE = SUBLANES
PROJ_TILE = 1024


def _cparams(sem):
    return pltpu.CompilerParams(dimension_semantics=sem, vmem_limit_bytes=VMEM_LIMIT_BYTES)


def _sigmoid(v):
    return 1.0 / (1.0 + jnp.exp(-v))


def _silu(v):
    return v * _sigmoid(v)


def _softplus(v):
    return jnp.maximum(v, 0.0) + jnp.log(1.0 + jnp.exp(-jnp.abs(v)))


def _split2(v):
    hi = v.astype(BF16)
    lo = (v - hi.astype(F32)).astype(BF16)
    return hi, lo


def _dot_split(a_hi, a_lo, b_hi_ref, b_lo_ref):
    b_hi = b_hi_ref[...]
    acc = jnp.dot(a_hi, b_hi, preferred_element_type=F32)
    acc = acc + jnp.dot(a_lo, b_hi, preferred_element_type=F32)
    return acc + jnp.dot(a_hi, b_lo_ref[...], preferred_element_type=F32)


def _norm_dt_kernel(x_ref, nw_ref, wdt_hi_ref, wdt_lo_ref, h_ref, dt_ref, dtt_ref):
    x = x_ref[...]
    ms = jnp.mean(x * x, axis=-1, keepdims=True)
    h_hi, h_lo = _split2(x * lax.rsqrt(ms + EPS) * nw_ref[...])
    h_ref[...] = h_hi
    dt = _dot_split(h_hi, h_lo, wdt_hi_ref, wdt_lo_ref)
    dt_ref[...] = dt
    dtt_ref[...] = dt.T


def _norm_dt(x2d, norm_w, w_dt_hi, w_dt_lo, tm):
    t, d = x2d.shape
    n_dt = w_dt_hi.shape[1]
    return pl.pallas_call(
        _norm_dt_kernel,
        grid=(t // tm,),
        in_specs=[
            pl.BlockSpec((tm, d), lambda m: (m, 0)),
            pl.BlockSpec((1, d), lambda m: (0, 0)),
            pl.BlockSpec((d, n_dt), lambda m: (0, 0)),
            pl.BlockSpec((d, n_dt), lambda m: (0, 0)),
        ],
        out_specs=[
            pl.BlockSpec((tm, d), lambda m: (m, 0)),
            pl.BlockSpec((tm, n_dt), lambda m: (m, 0)),
            pl.BlockSpec((n_dt, tm), lambda m: (0, m)),
        ],
        out_shape=[
            jax.ShapeDtypeStruct((t, d), BF16),
            jax.ShapeDtypeStruct((t, n_dt), F32),
            jax.ShapeDtypeStruct((n_dt, t), F32),
        ],
        compiler_params=_cparams(("parallel",)),
        name="norm_dt",
    )(x2d, norm_w, w_dt_hi, w_dt_lo)


def _inproj_kernel(h_ref, w_ref, o_ref):
    o_ref[0] = jnp.dot(h_ref[...], w_ref[0], preferred_element_type=F32).astype(o_ref.dtype)


def _in_proj(h, w_tiles, tm):
    t, d = h.shape
    n_tiles, _, tn = w_tiles.shape
    return pl.pallas_call(
        _inproj_kernel,
        grid=(t // tm, n_tiles),
        in_specs=[
            pl.BlockSpec((tm, d), lambda m, n: (m, 0)),
            pl.BlockSpec((1, d, tn), lambda m, n: (n, 0, 0)),
        ],
        out_specs=pl.BlockSpec((1, tm, tn), lambda m, n: (n, m, 0)),
        out_shape=jax.ShapeDtypeStruct((n_tiles, t, tn), BF16),
        compiler_params=_cparams(("parallel", "arbitrary")),
        name="in_proj",
    )(h, w_tiles)


def _fill_halo_scratch(scr, cur, prev, nxt, lt):
    l = pl.program_id(1)
    last = pl.num_programs(1) - 1
    scr[HALO:HALO + lt, :] = cur
    scr[0:HALO, :] = jnp.where(l > 0, prev, 0.0)
    scr[HALO + lt:HALO + lt + HALO, :] = jnp.where(l < last, nxt, 0.0)


def _conv_ssm_kernel(cur_ref, prev_ref, next_ref, w_ref, b_ref, o_ref, scr, *, lt, rc):
    _fill_halo_scratch(scr, cur_ref[0, 0].astype(F32), prev_ref[0, 0].astype(F32), next_ref[0, 0].astype(F32), lt)
    pad = SSM_CONV // 2
    w = w_ref[...]
    bias = b_ref[...]

    def body(i, carry):
        r0 = pl.multiple_of(i * rc, rc)
        win = scr[pl.ds(r0 + (HALO - SUBLANES), rc + 2 * SUBLANES), :]
        acc = jnp.zeros((rc, scr.shape[1]), F32)
        for k in range(SSM_CONV):
            o = SUBLANES - pad + k
            acc = acc + w[k:k + 1, :] * win[o:o + rc, :]
        o_ref[0, pl.ds(r0, rc), :] = _silu(acc + bias).astype(o_ref.dtype)
        return carry

    lax.fori_loop(0, lt // rc, body, 0)


def _conv_ssm(proj4, conv_w, conv_b, tile0, lt, ct, rc=64):
    _, bsz, seq, tn = proj4.shape
    c = conv_w.shape[1]
    per = tn // ct
    hb = lt // HALO
    n_hb = seq // HALO
    kern = functools.partial(_conv_ssm_kernel, lt=lt, rc=rc)
    return pl.pallas_call(
        kern,
        grid=(bsz, seq // lt, c // ct),
        in_specs=[
            pl.BlockSpec((1, 1, lt, ct), lambda b, l, j: (tile0 + j // per, b, l, j % per)),
            pl.BlockSpec((1, 1, HALO, ct),
                         lambda b, l, j: (tile0 + j // per, b, jnp.maximum(l * hb - 1, 0), j % per)),
            pl.BlockSpec((1, 1, HALO, ct),
                         lambda b, l, j: (tile0 + j // per, b, jnp.minimum((l + 1) * hb, n_hb - 1), j % per)),
            pl.BlockSpec((SSM_CONV, ct), lambda b, l, j: (0, j)),
            pl.BlockSpec((1, ct), lambda b, l, j: (0, j)),
        ],
        out_specs=pl.BlockSpec((1, lt, ct), lambda b, l, j: (b, l, j)),
        out_shape=jax.ShapeDtypeStruct((bsz, seq, c), BF16),
        scratch_shapes=[pltpu.VMEM((lt + 2 * HALO, ct), F32)],
        compiler_params=_cparams(("parallel", "parallel", "parallel")),
        name="conv_ssm",
    )(proj4, proj4, proj4, conv_w, conv_b)


def _conformer_kernel(*refs, lt, rc, lc, n_t):
    tiles = refs[:6 * n_t]
    w_ref, b_ref, lnw_ref, lnb_ref, o_ref, scr, sh_scr, v_scr = refs[6 * n_t:]

    def glu(group):
        a = jnp.concatenate([r[0, 0].astype(F32) for r in tiles[2 * n_t * group:2 * n_t * group + n_t]], axis=1)
        g = jnp.concatenate([r[0, 0].astype(F32) for r in tiles[2 * n_t * group + n_t:2 * n_t * (group + 1)]], axis=1)
        return a * _sigmoid(g)

    _fill_halo_scratch(scr, glu(0), glu(1), glu(2), lt)
    c = scr.shape[1]
    pad = CONF_KERNEL // 2
    n_sh = lt + 2 * HALO - SUBLANES
    for r0 in range(0, n_sh, rc):
        n = min(rc, n_sh - r0)
        for j in range(c // lc):
            win = scr[r0:r0 + n + SUBLANES, j * lc:(j + 1) * lc]
            for m in range(1, SUBLANES):
                sh_scr[m - 1, r0:r0 + n, j * lc:(j + 1) * lc] = win[m:m + n, :]

    def conv_body(i, carry):
        r0 = pl.multiple_of(i * rc, rc)
        for j in range(c // lc):
            lanes = slice(j * lc, (j + 1) * lc)
            acc = jnp.zeros((rc, lc), F32)
            for m in range(SUBLANES):
                offs = [(k, HALO - pad + k - m) for k in range(CONF_KERNEL) if (HALO - pad + k) % SUBLANES == m]
                if not offs:
                    continue
                span = rc + offs[-1][1]
                if m == 0:
                    win = scr[pl.ds(r0, span), lanes]
                else:
                    win = sh_scr[m - 1, pl.ds(r0, span), lanes]
                for k, base in offs:
                    acc = acc + w_ref[k:k + 1, lanes] * win[base:base + rc, :]
            v_scr[pl.ds(r0, rc), lanes] = acc + b_ref[:, lanes]
        return carry

    lax.fori_loop(0, lt // rc, conv_body, 0)
    lnw = lnw_ref[...]
    lnb = lnb_ref[...]
    rn = 2 * SUBLANES

    def norm_body(i, carry):
        r0 = pl.multiple_of(i * rn, rn)
        v = v_scr[pl.ds(r0, rn), :]
        mu = jnp.mean(v, axis=-1, keepdims=True)
        vc = v - mu
        var = jnp.mean(vc * vc, axis=-1, keepdims=True)
        y = vc * lax.rsqrt(var + EPS) * lnw + lnb
        o_ref[0, pl.ds(r0, rn), :] = _silu(y).astype(o_ref.dtype)
        return carry

    lax.fori_loop(0, lt // rn, norm_body, 0, unroll=4)


def _conformer(proj4, dw_w, dw_b, ln_w, ln_b, tile_a, tile_g, lt, rc=64, lc=128):
    _, bsz, seq, tn = proj4.shape
    c = dw_w.shape[1]
    n_t = c // tn
    hb = lt // HALO
    n_hb = seq // HALO
    kern = functools.partial(_conformer_kernel, lt=lt, rc=rc, lc=lc, n_t=n_t)

    def cur(j):
        return pl.BlockSpec((1, 1, lt, tn), lambda b, l: (j, b, l, 0))

    def prev(j):
        return pl.BlockSpec((1, 1, HALO, tn), lambda b, l: (j, b, jnp.maximum(l * hb - 1, 0), 0))

    def nxt(j):
        return pl.BlockSpec((1, 1, HALO, tn), lambda b, l: (j, b, jnp.minimum((l + 1) * hb, n_hb - 1), 0))

    tile_specs = [mk(t0 + j) for mk in (cur, prev, nxt) for t0 in (tile_a, tile_g) for j in range(n_t)]
    vec = pl.BlockSpec((1, c), lambda b, l: (0, 0))
    return pl.pallas_call(
        kern,
        grid=(bsz, seq // lt),
        in_specs=tile_specs + [pl.BlockSpec((CONF_KERNEL, c), lambda b, l: (0, 0)), vec, vec, vec],
        out_specs=pl.BlockSpec((1, lt, c), lambda b, l: (b, l, 0)),
        out_shape=jax.ShapeDtypeStruct((bsz, seq, c), BF16),
        scratch_shapes=[pltpu.VMEM((lt + 2 * HALO, c), F32),
                        pltpu.VMEM((SUBLANES - 1, lt + 2 * HALO, c), F32),
                        pltpu.VMEM((lt, c), F32)],
        compiler_params=_cparams(("parallel", "parallel")),
        name="conformer",
    )(*([proj4] * len(tile_specs)), dw_w, dw_b, ln_w, ln_b)


def _split3(v):
    hi = v.astype(BF16).astype(F32)
    r = v - hi
    mid = r.astype(BF16).astype(F32)
    lo = (r - mid).astype(BF16).astype(F32)
    return hi, mid, lo


def _ssd_kernel(*refs, reverse, final, n_heads, nbs):
    xs_ref, bm_ref, cm_ref, dt_ref = refs[:4]
    dtt_refs = refs[4:4 + nbs]
    consts = refs[4 + nbs:11 + nbs]
    if final:
        yf_ref = refs[11 + nbs]
        z_refs = refs[12 + nbs:-4]
        dsk_ref, nw_ref, o_ref, state_scr = refs[-4:]
    else:
        yf_ref, z_refs, dsk_ref, nw_ref = None, (), None, None
        o_ref, state_scr = refs[-2:]

    @pl.when(pl.program_id(1) == 0)
    def _():
        state_scr[...] = jnp.zeros_like(state_scr)

    for s in range(nbs):
        _ssd_chunk(s, xs_ref, bm_ref, cm_ref, dt_ref, dtt_refs[s], consts, yf_ref, z_refs, dsk_ref, nw_ref, o_ref,
                   state_scr, reverse=reverse, final=final, n_heads=n_heads)


def _ssd_chunk(s, xs_ref, bm_ref, cm_ref, dt_ref, dtt_ref, consts, yf_ref, z_refs, dsk_ref, nw_ref, o_ref,
               state_scr, *, reverse, final, n_heads):
    dtb_ref, dtbt_ref, alog_ref, alogt_ref, tri_ref, trit_ref, e3_ref = consts
    q = CHUNK
    hg = n_heads
    d_ssm = n_heads * HEAD_DIM
    gw = d_ssm // SSM_GROUPS

    lane = lax.broadcasted_iota(jnp.int32, (q, LANES), 1)
    sub = lax.broadcasted_iota(jnp.int32, (LANES, q), 0)

    def sel3_lanes(v):
        hi, mid, lo = _split3(v)
        return jnp.where(lane < hg, hi, jnp.where(lane < 2 * hg, mid, jnp.where(lane < 3 * hg, lo, 0.0))).astype(BF16)

    def sel3_rows(v):
        hi, mid, lo = _split3(v)
        return jnp.where(sub < hg, hi, jnp.where(sub < 2 * hg, mid, jnp.where(sub < 3 * hg, lo, 0.0))).astype(BF16)

    a_row = -jnp.exp(alog_ref[...])
    dt = _softplus(dt_ref[s] + dtb_ref[...])
    da = dt * a_row
    a_col = -jnp.exp(alogt_ref[...])
    dat = _softplus(dtt_ref[...] + dtbt_ref[...]) * a_col

    tri = tri_ref[...]
    cum3 = jnp.dot(tri, sel3_lanes(da), preferred_element_type=F32)
    cum = cum3
    for r in (1, 2, 3):
        cum = cum + pltpu.roll(cum3, r * hg, axis=1)
    cumt3 = jnp.dot(sel3_rows(dat), trit_ref[...], preferred_element_type=F32)
    cumt = cumt3[0:hg] + cumt3[hg:2 * hg] + cumt3[2 * hg:3 * hg]

    tot_row = 0 if reverse else q - 1
    total = cum[tot_row:tot_row + 1, :]

    e3 = e3_ref[...]
    lhs = jnp.concatenate(
        [sel3_lanes(dt), sel3_lanes(jnp.exp(cum)), sel3_lanes(jnp.exp(total - cum)),
         sel3_lanes(jnp.broadcast_to(jnp.exp(total), (q, LANES)))], axis=0)
    ex = jnp.dot(lhs, e3, preferred_element_type=F32)
    dt_x = ex[0:q]
    ecum_x = ex[q:2 * q]
    edte_x = ex[2 * q:3 * q]
    cdec_x = ex[3 * q:3 * q + 1]

    xs = xs_ref[s].astype(F32)
    xdt = xs * dt_x
    xdt_b = xdt.astype(BF16)
    xdte_b = (xdt * edte_x).astype(BF16)
    bm = bm_ref[s]
    cm = cm_ref[s]

    li = lax.broadcasted_iota(jnp.int32, (q, q), 0)
    si = lax.broadcasted_iota(jnp.int32, (q, q), 1)
    mask = (li <= si) if reverse else (li >= si)
    lane_lo = lax.broadcasted_iota(jnp.int32, (q, LANES), 1) < HEAD_DIM

    hpg = n_heads // SSM_GROUPS
    y_parts = []
    for g in range(SSM_GROUPS):
        bg = bm[:, g * D_STATE:(g + 1) * D_STATE]
        cg = cm[:, g * D_STATE:(g + 1) * D_STATE]
        cb = lax.dot_general(cg, bg, (((1,), (1,)), ((), ())), preferred_element_type=F32)
        st = state_scr[s, :, g * gw:(g + 1) * gw]
        y_off = jnp.dot(cg, st.astype(BF16), preferred_element_type=F32) * ecum_x[:, g * gw:(g + 1) * gw]
        diag = []
        for pair in range(hpg // 2):
            h0 = g * hpg + 2 * pair
            ls = []
            for h in (h0, h0 + 1):
                seg = cum[:, h:h + 1] - cumt[h:h + 1, :]
                ls.append((cb * jnp.exp(jnp.where(mask, seg, -jnp.inf))).astype(BF16))
            l2 = jnp.concatenate(ls, axis=1)
            xp = xdt_b[:, h0 * HEAD_DIM:(h0 + 2) * HEAD_DIM]
            zero = jnp.zeros_like(xp)
            r2 = jnp.concatenate([jnp.where(lane_lo, xp, zero), jnp.where(lane_lo, zero, xp)], axis=0)
            diag.append(jnp.dot(l2, r2, preferred_element_type=F32))
        y_parts.append(jnp.concatenate(diag, axis=1) + y_off)
        contrib = lax.dot_general(bg, xdte_b[:, g * gw:(g + 1) * gw], (((0,), (0,)), ((), ())),
                                  preferred_element_type=F32)
        state_scr[s, :, g * gw:(g + 1) * gw] = st * cdec_x[:, g * gw:(g + 1) * gw] + contrib
    y = jnp.concatenate(y_parts, axis=1)

    if not final:
        o_ref[s] = y
    else:
        y = y + yf_ref[s] + dsk_ref[...] * xs
        y = y * _silu(jnp.concatenate([r[0, s].astype(F32) for r in z_refs], axis=1))
        outs = []
        for g in range(SSM_GROUPS):
            yg = y[:, g * gw:(g + 1) * gw]
            ms = jnp.mean(yg * yg, axis=-1, keepdims=True)
            outs.append(yg * lax.rsqrt(ms + EPS))
        o_ref[s] = (jnp.concatenate(outs, axis=1) * nw_ref[...]).astype(o_ref.dtype)


def _ssd(xbc3, dt2, dtt2, dtb, dtbt, alog, alogt, e3, *, reverse, n_heads, final_args=None):
    bsz, seq, _ = xbc3.shape
    d_ssm = n_heads * HEAD_DIM
    gn = SSM_GROUPS * D_STATE
    nc = seq // CHUNK
    q = CHUNK
    d = 1 if reverse else 0
    final = final_args is not None

    def cidx(c):
        return (nc - 1 - c) if reverse else c

    li = jnp.arange(q)[:, None]
    ji = jnp.arange(q)[None, :]
    tri = ((ji >= li) if reverse else (ji <= li)).astype(BF16)
    trit = tri.T

    nbs = 2 if bsz % 2 == 0 else 1
    dt3 = dt2.reshape(bsz, seq, dt2.shape[1])
    const = lambda shape: pl.BlockSpec(shape, lambda b, c: (0,) * len(shape))
    in_specs = [
        pl.BlockSpec((nbs, q, d_ssm), lambda b, c: (b, cidx(c), 0)),
        pl.BlockSpec((nbs, q, gn), lambda b, c: (b, cidx(c), d_ssm // gn)),
        pl.BlockSpec((nbs, q, gn), lambda b, c: (b, cidx(c), d_ssm // gn + 1)),
        pl.BlockSpec((nbs, q, LANES), lambda b, c: (b, cidx(c), d)),
    ]
    in_specs += [pl.BlockSpec((LANES, q), lambda b, c, s=s: (d, (b * nbs + s) * nc + cidx(c))) for s in range(nbs)]
    in_specs += [const((1, LANES)), const((LANES, q)), const((1, LANES)), const((LANES, q)),
                 const((q, q)), const((q, q)), const((LANES, d_ssm))]
    args = [xbc3, xbc3, xbc3, dt3] + [dtt2] * nbs + [dtb, dtbt, alog, alogt, tri, trit, e3]
    if final:
        yf3, proj4, dskip_x, norm_w = final_args
        tn = proj4.shape[3]
        z_specs = [pl.BlockSpec((1, nbs, q, tn), lambda b, c, j=j: (j, b, cidx(c), 0)) for j in range(d_ssm // tn)]
        in_specs += [pl.BlockSpec((nbs, q, d_ssm), lambda b, c: (b, cidx(c), 0))] + z_specs
        in_specs += [const((1, d_ssm)), const((1, d_ssm))]
        args += [yf3] + [proj4] * len(z_specs) + [dskip_x, norm_w]
    kern = functools.partial(_ssd_kernel, reverse=reverse, final=final, n_heads=n_heads, nbs=nbs)
    return pl.pallas_call(
        kern,
        grid=(bsz // nbs, nc),
        in_specs=in_specs,
        out_specs=pl.BlockSpec((nbs, q, d_ssm), lambda b, c: (b, cidx(c), 0)),
        out_shape=jax.ShapeDtypeStruct((bsz, seq, d_ssm), BF16 if final else F32),
        scratch_shapes=[pltpu.VMEM((nbs, D_STATE, d_ssm), F32)],
        compiler_params=_cparams(("parallel", "arbitrary")),
        name="ssd_bwd" if reverse else "ssd_fwd",
    )(*args)


def _outproj_kernel(ys_ref, yc_ref, wa_ref, wb_ref, x_ref, nw_ref, wr_hi_ref, wr_lo_ref, br_ref, x1_ref, lg_ref):
    acc = jnp.dot(ys_ref[...], wa_ref[...], preferred_element_type=F32)
    acc = acc + jnp.dot(yc_ref[...], wb_ref[...], preferred_element_type=F32)
    x1 = x_ref[...] + acc
    x1_ref[...] = x1
    ms = jnp.mean(x1 * x1, axis=-1, keepdims=True)
    h_hi, h_lo = _split2(x1 * lax.rsqrt(ms + EPS) * nw_ref[...])
    lg_ref[...] = _dot_split(h_hi, h_lo, wr_hi_ref, wr_lo_ref) + br_ref[...]


def _out_proj(y_ssm, y_conf, w_out, x2d, norm_w, wr_hi, wr_lo, b_router, tm):
    t, d = x2d.shape
    ka = y_ssm.shape[1]
    ne = wr_hi.shape[1]
    resident = lambda shape, idx: pl.BlockSpec(shape, lambda m: idx, pipeline_mode=pl.Buffered(1))
    return pl.pallas_call(
        _outproj_kernel,
        grid=(t // tm,),
        in_specs=[
            pl.BlockSpec((tm, ka), lambda m: (m, 0)),
            pl.BlockSpec((tm, ka), lambda m: (m, 0)),
            resident((ka, d), (0, 0)),
            resident((ka, d), (1, 0)),
            pl.BlockSpec((tm, d), lambda m: (m, 0)),
            pl.BlockSpec((1, d), lambda m: (0, 0)),
            resident((d, ne), (0, 0)),
            resident((d, ne), (0, 0)),
            pl.BlockSpec((1, ne), lambda m: (0, 0)),
        ],
        out_specs=[
            pl.BlockSpec((tm, d), lambda m: (m, 0)),
            pl.BlockSpec((tm, ne), lambda m: (m, 0)),
        ],
        out_shape=[
            jax.ShapeDtypeStruct((t, d), F32),
            jax.ShapeDtypeStruct((t, ne), F32),
        ],
        compiler_params=_cparams(("parallel",)),
        name="out_proj",
    )(y_ssm, y_conf, w_out, w_out, x2d, norm_w, wr_hi, wr_lo, b_router)


def _router_kernel(lg_ref, tri_ref, ir_ref, gate_ref, cnt_ref, run_scr, *, n_experts):
    @pl.when(pl.program_id(0) == 0)
    def _():
        run_scr[...] = jnp.zeros_like(run_scr)

    tr = lg_ref.shape[0]
    lane = lax.broadcasted_iota(jnp.int32, (tr, LANES), 1)
    lane_f = lane.astype(F32)
    lg = jnp.where(lane < n_experts, lg_ref[...], -jnp.inf)
    vals, ids, hots = [], [], []
    for _ in range(TOP_K):
        m = jnp.max(lg, axis=1, keepdims=True)
        idx = jnp.min(jnp.where(lg == m, lane_f, float(LANES)), axis=1, keepdims=True)
        hot = lane_f == idx
        vals.append(m)
        ids.append(idx)
        hots.append(hot)
        lg = jnp.where(hot, -jnp.inf, lg)
    es = [jnp.exp(v - vals[0]) for v in vals]
    den = es[0]
    for e in es[1:]:
        den = den + e
    hot_any = hots[0].astype(F32)
    for hot in hots[1:]:
        hot_any = hot_any + hot.astype(F32)
    before = jnp.dot(tri_ref[...], hot_any.astype(BF16), preferred_element_type=F32) + run_scr[...]
    run_scr[...] = run_scr[...] + jnp.sum(hot_any, axis=0, keepdims=True)
    cnt_ref[...] = run_scr[...].astype(jnp.int32)
    out_i = jnp.zeros((tr, LANES), F32)
    out_g = jnp.zeros((tr, LANES), F32)
    for k in range(TOP_K):
        rank = jnp.sum(jnp.where(hots[k], before, 0.0), axis=1, keepdims=True)
        out_i = jnp.where(lane == k, ids[k], out_i)
        out_i = jnp.where(lane == TOP_K + k, rank, out_i)
        out_g = jnp.where(lane == k, es[k] / den, out_g)
    ir_ref[...] = out_i.astype(jnp.int32)
    gate_ref[...] = out_g


def _router(logits, n_experts, tr):
    t = logits.shape[0]
    tri = (jnp.arange(tr)[None, :] < jnp.arange(tr)[:, None]).astype(BF16)
    kern = functools.partial(_router_kernel, n_experts=n_experts)
    return pl.pallas_call(
        kern,
        grid=(t // tr,),
        in_specs=[pl.BlockSpec((tr, LANES), lambda i: (i, 0)), pl.BlockSpec((tr, tr), lambda i: (0, 0))],
        out_specs=[pl.BlockSpec((tr, LANES), lambda i: (i, 0)), pl.BlockSpec((tr, LANES), lambda i: (i, 0)),
                   pl.BlockSpec((1, LANES), lambda i: (0, 0))],
        out_shape=[jax.ShapeDtypeStruct((t, LANES), jnp.int32), jax.ShapeDtypeStruct((t, LANES), F32),
                   jax.ShapeDtypeStruct((1, LANES), jnp.int32)],
        scratch_shapes=[pltpu.VMEM((1, LANES), F32)],
        compiler_params=_cparams(("arbitrary",)),
        name="moe_router",
    )(logits, tri)


_HI16 = 0xFFFF0000


def _pack_bf16_pair(a, b):
    au = lax.bitcast_convert_type(a.astype(BF16).astype(F32), jnp.uint32)
    bu = lax.bitcast_convert_type(b.astype(BF16).astype(F32), jnp.uint32)
    return (au & jnp.uint32(_HI16)) | lax.shift_right_logical(bu, jnp.uint32(16))


def _unpack_bf16_pair(w):
    a = lax.bitcast_convert_type(w & jnp.uint32(_HI16), F32).astype(BF16)
    b = lax.bitcast_convert_type(lax.shift_left(w, jnp.uint32(16)), F32).astype(BF16)
    return a, b


def _dispatch_kernel(zs_ref, idx_ref, x1_ref, nw_ref, xs_hbm, zero_scr, pk, sem, zsem, *, tm, bm, n_experts):
    i = pl.program_id(0)
    last = pl.num_programs(0) - 1
    slot = lax.rem(i, 2)
    half = x1_ref.shape[1] // 2
    rt = ROW_TILE

    def zero_copy(e):
        dst = xs_hbm.at[pl.ds(pl.multiple_of(zs_ref[e], rt * bm), rt * bm), :]
        return pltpu.make_async_copy(zero_scr, dst, zsem)

    @pl.when(i == 0)
    def _():
        zero_scr[...] = jnp.zeros_like(zero_scr)
        for e in range(n_experts):
            @pl.when(zs_ref[e] >= 0)
            def _():
                zero_copy(e).start()
        for e in range(n_experts):
            @pl.when(zs_ref[e] >= 0)
            def _():
                zero_copy(e).wait()

    def wait_rows(s):
        rows = xs_hbm.at[pl.ds(0, rt * TOP_K * tm), :]
        pltpu.make_async_copy(rows, rows, sem.at[s]).wait()

    @pl.when(i >= 2)
    def _():
        wait_rows(slot)

    x = x1_ref[...]
    ms = jnp.mean(x * x, axis=-1, keepdims=True)
    h = x * lax.rsqrt(ms + EPS) * nw_ref[...]
    packed = _pack_bf16_pair(h[:, :half], h[:, half:])
    for c in range(rt):
        pk[slot, pl.ds(c, tm, stride=rt), :] = packed[:, c * LANES:(c + 1) * LANES]

    def body(t, carry):
        src = pk.at[slot, pl.ds(pl.multiple_of(t * rt, rt), rt), :]
        for k in range(TOP_K):
            r = pl.multiple_of(idx_ref[0, 0, t * TOP_K + k], rt)
            pltpu.make_async_copy(src, xs_hbm.at[pl.ds(r, rt), :], sem.at[slot]).start(priority=k % 2)
        return carry

    lax.fori_loop(0, tm, body, 0, unroll=4)

    @pl.when(i == last)
    def _():
        wait_rows(slot)

    @pl.when(jnp.logical_and(i == last, i >= 1))
    def _():
        wait_rows(1 - slot)


def _dispatch(dest, zero_start, x1, norm_w, n_rows, tm, bm):
    t, d = x1.shape
    assert d // 2 == ROW_TILE * LANES
    nt = t // tm
    n_experts = zero_start.shape[0]
    idx3 = (dest * ROW_TILE).reshape(nt, 1, tm * TOP_K)
    zero_start = zero_start * ROW_TILE
    kern = functools.partial(_dispatch_kernel, tm=tm, bm=bm, n_experts=n_experts)
    return pl.pallas_call(
        kern,
        grid_spec=pltpu.PrefetchScalarGridSpec(
            num_scalar_prefetch=1,
            grid=(nt,),
            in_specs=[
                pl.BlockSpec((1, 1, TOP_K * tm), lambda i, zs: (i, 0, 0), memory_space=pltpu.SMEM),
                pl.BlockSpec((tm, d), lambda i, zs: (i, 0)),
                pl.BlockSpec((1, d), lambda i, zs: (0, 0)),
            ],
            out_specs=pl.BlockSpec(memory_space=pl.ANY),
            scratch_shapes=[pltpu.VMEM((ROW_TILE * bm, LANES), jnp.uint32),
                            pltpu.VMEM((2, ROW_TILE * tm, LANES), jnp.uint32),
                            pltpu.SemaphoreType.DMA((2,)), pltpu.SemaphoreType.DMA(())],
        ),
        out_shape=jax.ShapeDtypeStruct((ROW_TILE * n_rows, LANES), jnp.uint32),
        compiler_params=_cparams(("arbitrary",)),
        name="moe_dispatch",
    )(zero_start, idx3, x1, norm_w)


def _for_row_bucket(b, nu_ref, bq_ref, bm, compute):
    for quarters in range(1, MOE_BUCKETS + 1):
        @pl.when(jnp.logical_and(b < nu_ref[0], bq_ref[b] == quarters))
        def _():
            compute(quarters * (bm // MOE_BUCKETS))


def _expert_weights(b, tile, n_tiles, tabs, copies):
    be_ref, nu_ref, first_ref, run_ref, next_ref, nruns_ref = tabs
    run = run_ref[b]
    slot = lax.rem(tile * nruns_ref[0] + run, 2)
    first = jnp.logical_and(b < nu_ref[0], first_ref[b] == 1)
    nxt = next_ref[b]

    @pl.when(jnp.logical_and(first, jnp.logical_and(tile == 0, run == 0)))
    def _():
        for c in copies(be_ref[b], tile, slot):
            c.start()

    @pl.when(first)
    def _():
        for c in copies(be_ref[b], tile, slot):
            c.wait()

    @pl.when(jnp.logical_and(first, nxt >= 0))
    def _():
        for c in copies(nxt, tile, 1 - slot):
            c.start()

    @pl.when(jnp.logical_and(first, jnp.logical_and(nxt < 0, tile + 1 < n_tiles)))
    def _():
        for c in copies(be_ref[0], tile + 1, 1 - slot):
            c.start()

    return slot, first


def _moe_up_kernel(be_ref, nu_ref, bq_ref, first_ref, run_ref, next_ref, nruns_ref, x_ref, wg_hbm, bg_ref, wu_hbm,
                   bu_ref, o_ref, wbuf, wsem, wbf):
    bm = x_ref.shape[0] // ROW_TILE
    tn = o_ref.shape[1]

    def copies(e, tile, s):
        cols = pl.ds(pl.multiple_of(tile * tn, tn), tn)
        return [pltpu.make_async_copy(wg_hbm.at[e, :, cols], wbuf.at[s, 0], wsem.at[s, 0]),
                pltpu.make_async_copy(wu_hbm.at[e, :, cols], wbuf.at[s, 1], wsem.at[s, 1])]

    slot, first = _expert_weights(pl.program_id(1), pl.program_id(0), pl.num_programs(0),
                                  (be_ref, nu_ref, first_ref, run_ref, next_ref, nruns_ref), copies)

    @pl.when(first)
    def _():
        wbf[0] = wbuf[slot, 0].astype(BF16)
        wbf[1] = wbuf[slot, 1].astype(BF16)

    def compute(nv):
        words = jnp.concatenate([x_ref[pl.ds(c, nv, stride=ROW_TILE), :] for c in range(ROW_TILE)], axis=1)
        xa, xb = _unpack_bf16_pair(words)
        half = xa.shape[1]
        wg = wbf[0]
        wu = wbf[1]
        gate = (jnp.dot(xa, wg[:half], preferred_element_type=F32)
                + jnp.dot(xb, wg[half:], preferred_element_type=F32) + bg_ref[0])
        up = (jnp.dot(xa, wu[:half], preferred_element_type=F32)
              + jnp.dot(xb, wu[half:], preferred_element_type=F32) + bu_ref[0])
        gate = jnp.minimum(gate, SWIGLU_LIMIT)
        up = jnp.clip(up, -SWIGLU_LIMIT, SWIGLU_LIMIT)
        act = (up + 1.0) * gate * _sigmoid(SWIGLU_ALPHA * gate)
        o_ref[:nv, :] = act.astype(o_ref.dtype)

    _for_row_bucket(pl.program_id(1), nu_ref, bq_ref, bm, compute)


def _moe_down_kernel(be_ref, nu_ref, bq_ref, first_ref, run_ref, next_ref, nruns_ref, a_ref, wd_hbm, bd_ref, o_ref,
                     wbuf, wsem, wbf):
    def copies(e, tile, s):
        return [pltpu.make_async_copy(wd_hbm.at[e], wbuf.at[s], wsem.at[s])]

    slot, first = _expert_weights(pl.program_id(0), 0, 1,
                                  (be_ref, nu_ref, first_ref, run_ref, next_ref, nruns_ref), copies)

    @pl.when(first)
    def _():
        wbf[...] = wbuf[slot].astype(BF16)

    def compute(nv):
        y = jnp.dot(a_ref[:nv, :], wbf[...], preferred_element_type=F32) + bd_ref[0]
        d = y.shape[1]
        packed = _pack_bf16_pair(y[:, :d // 2], y[:, d // 2:])
        for c in range(ROW_TILE):
            o_ref[pl.ds(c, nv, stride=ROW_TILE), :] = packed[:, c * LANES:(c + 1) * LANES]

    _for_row_bucket(pl.program_id(0), nu_ref, bq_ref, a_ref.shape[0], compute)


def _blk(b, nu_ref):
    return jnp.minimum(b, nu_ref[0] - 1)


def _moe_up(tables, xs, w_gate, b_gate3, w_up, b_up3, bm, tn):
    n_rows = xs.shape[0] // ROW_TILE
    d, dff = w_gate.shape[1], w_gate.shape[2]
    nb = n_rows // bm
    hbm = pl.BlockSpec(memory_space=pl.ANY)
    bspec = pl.BlockSpec((1, 1, tn), lambda n, b, be, nu, *_: (be[_blk(b, nu)], 0, n))
    return pl.pallas_call(
        _moe_up_kernel,
        grid_spec=pltpu.PrefetchScalarGridSpec(
            num_scalar_prefetch=len(tables),
            grid=(dff // tn, nb),
            in_specs=[pl.BlockSpec((ROW_TILE * bm, LANES), lambda n, b, be, nu, *_: (_blk(b, nu), 0)),
                      hbm, bspec, hbm, bspec],
            out_specs=pl.BlockSpec((bm, tn), lambda n, b, be, nu, *_: (_blk(b, nu), n)),
            scratch_shapes=[pltpu.VMEM((2, 2, d, tn), F32), pltpu.SemaphoreType.DMA((2, 2)),
                            pltpu.VMEM((2, d, tn), BF16)],
        ),
        out_shape=jax.ShapeDtypeStruct((n_rows, dff), BF16),
        compiler_params=_cparams(("arbitrary", "arbitrary")),
        name="moe_up",
    )(*tables, xs, w_gate, b_gate3, w_up, b_up3)


def _moe_down(tables, act, w_down, b_down3, bm):
    n_rows, dff = act.shape
    d = w_down.shape[2]
    assert d // 2 == ROW_TILE * LANES
    nb = n_rows // bm
    return pl.pallas_call(
        _moe_down_kernel,
        grid_spec=pltpu.PrefetchScalarGridSpec(
            num_scalar_prefetch=len(tables),
            grid=(nb,),
            in_specs=[
                pl.BlockSpec((bm, dff), lambda b, be, nu, *_: (_blk(b, nu), 0)),
                pl.BlockSpec(memory_space=pl.ANY),
                pl.BlockSpec((1, 1, d), lambda b, be, nu, *_: (be[_blk(b, nu)], 0, 0)),
            ],
            out_specs=pl.BlockSpec((ROW_TILE * bm, LANES), lambda b, be, nu, *_: (_blk(b, nu), 0)),
            scratch_shapes=[pltpu.VMEM((2, dff, d), F32), pltpu.SemaphoreType.DMA((2,)),
                            pltpu.VMEM((dff, d), BF16)],
        ),
        out_shape=jax.ShapeDtypeStruct((ROW_TILE * n_rows, LANES), jnp.uint32),
        compiler_params=_cparams(("arbitrary",)),
        name="moe_down",
    )(*tables, act, w_down, b_down3)


def _combine_kernel(idx_ref, idxn_ref, y_hbm, x1_ref, g_ref, nw_ref, o_ref, buf, sem, *, tm):
    i = pl.program_id(0)
    slot = lax.rem(i, 2)
    rt = ROW_TILE

    def start_tile(ids_ref, s):
        def body(t, carry):
            for k in range(TOP_K):
                r = pl.multiple_of(ids_ref[0, 0, t * TOP_K + k], rt)
                dst = buf.at[s, pl.ds(pl.multiple_of((k * tm + t) * rt, rt), rt), :]
                pltpu.make_async_copy(y_hbm.at[pl.ds(r, rt), :], dst, sem.at[s]).start(priority=k % 2)
            return carry

        lax.fori_loop(0, tm, body, 0, unroll=4)

    @pl.when(i == 0)
    def _():
        start_tile(idx_ref, 0)

    @pl.when(i + 1 < pl.num_programs(0))
    def _():
        start_tile(idxn_ref, 1 - slot)

    pltpu.make_async_copy(y_hbm.at[pl.ds(0, rt * TOP_K * tm), :], buf.at[slot], sem.at[slot]).wait()
    acc = x1_ref[...]
    g = g_ref[...]
    for k in range(TOP_K):
        words = jnp.concatenate(
            [buf[slot, pl.ds(rt * k * tm + c, tm, stride=rt), :] for c in range(rt)], axis=1)
        y = jnp.concatenate([lax.bitcast_convert_type(words & jnp.uint32(_HI16), F32),
                             lax.bitcast_convert_type(lax.shift_left(words, jnp.uint32(16)), F32)], axis=1)
        acc = acc + g[:, k:k + 1] * y
    ms = jnp.mean(acc * acc, axis=-1, keepdims=True)
    o_ref[...] = acc * lax.rsqrt(ms + EPS) * nw_ref[...]


def _combine(dest, y_rows, x1, gates, norm_w, tm):
    t, d = x1.shape
    nt = t // tm
    idx3 = (dest * ROW_TILE).reshape(nt, 1, tm * TOP_K)
    kern = functools.partial(_combine_kernel, tm=tm)
    return pl.pallas_call(
        kern,
        grid=(nt,),
        in_specs=[
            pl.BlockSpec((1, 1, TOP_K * tm), lambda i: (i, 0, 0), memory_space=pltpu.SMEM),
            pl.BlockSpec((1, 1, TOP_K * tm), lambda i: (jnp.minimum(i + 1, nt - 1), 0, 0), memory_space=pltpu.SMEM),
            pl.BlockSpec(memory_space=pl.ANY),
            pl.BlockSpec((tm, d), lambda i: (i, 0)),
            pl.BlockSpec((tm, TOP_K), lambda i: (i, 0)),
            pl.BlockSpec((1, d), lambda i: (0, 0)),
        ],
        out_specs=pl.BlockSpec((tm, d), lambda i: (i, 0)),
        out_shape=jax.ShapeDtypeStruct((t, d), F32),
        scratch_shapes=[pltpu.VMEM((2, ROW_TILE * TOP_K * tm, LANES), jnp.uint32), pltpu.SemaphoreType.DMA((2,))],
        compiler_params=_cparams(("arbitrary",)),
        name="moe_combine",
    )(idx3, idx3, y_rows, x1, gates, norm_w)


def _route_tables(ids_ranks, counts, n_experts, bm, n_blocks):
    top_idx = ids_ranks[:, :TOP_K]
    rank = ids_ranks[:, TOP_K:2 * TOP_K]
    cnt = counts[0, :n_experts]
    padded = ((cnt + bm - 1) // bm) * bm
    padded_end = jnp.cumsum(padded)
    padded_start = padded_end - padded
    hot = top_idx[:, :, None] == jnp.arange(n_experts, dtype=jnp.int32)[None, None, :]
    dest = rank + jnp.sum(jnp.where(hot, padded_start[None, None, :], 0), axis=-1)
    n_used = (padded_end[-1] // bm).astype(jnp.int32)
    block_start = jnp.arange(n_blocks, dtype=jnp.int32) * bm
    block_expert = jnp.minimum(jnp.sum(padded_end[None, :] <= block_start[:, None], axis=1), n_experts - 1)
    zero_start = jnp.where(cnt > 0, padded_end - bm, -1).astype(jnp.int32)
    hot_b = block_expert[:, None] == jnp.arange(n_experts, dtype=block_expert.dtype)[None, :]
    group_end = jnp.sum(jnp.where(hot_b, (padded_start + cnt)[None, :], 0), axis=1)
    rows = jnp.clip(group_end - block_start, 0, bm)
    q = bm // MOE_BUCKETS
    block_quarters = jnp.clip((rows + q - 1) // q, 1, MOE_BUCKETS).astype(jnp.int32)
    block_expert = block_expert.astype(jnp.int32)
    ar = jnp.arange(n_blocks, dtype=jnp.int32)
    prev_e = jnp.concatenate([jnp.full((1,), -1, jnp.int32), block_expert[:-1]])
    first = jnp.logical_and(ar < n_used, block_expert != prev_e)
    run_id = (jnp.cumsum(first.astype(jnp.int32)) - 1).astype(jnp.int32)
    cand = jnp.where(first, ar, n_blocks)
    later = jnp.concatenate([lax.cummin(cand[::-1])[::-1][1:], jnp.full((1,), n_blocks, jnp.int32)])
    run_next = jnp.where(later < n_blocks, block_expert[jnp.minimum(later, n_blocks - 1)], -1).astype(jnp.int32)
    tables = (block_expert, n_used.reshape(1), block_quarters, first.astype(jnp.int32), run_id, run_next,
              jnp.sum(first.astype(jnp.int32)).reshape(1))
    return dest.astype(jnp.int32), zero_start, tables


def _pick(n, candidates):
    for c in candidates:
        if n % c == 0:
            return c
    return n


def _layer(x, norm_mix_w, w_in, conv_ssm_w, conv_ssm_b, dt_bias_fwd, dt_bias_bwd, a_log_fwd, a_log_bwd,
           d_skip, ssm_norm_w, conf_dw_w, conf_dw_b, conf_ln_w, conf_ln_b, w_out, norm_ffn_w, w_router,
           b_router, w_gate, b_gate, w_up, b_up, w_down, b_down):
    bsz, seq, d = x.shape
    t = bsz * seq
    n_heads = dt_bias_fwd.shape[0]
    d_ssm = n_heads * HEAD_DIM
    d_xbc = conv_ssm_w.shape[1]
    d_conf = conf_dw_w.shape[1]
    n_experts = w_gate.shape[0]
    row = lambda v: v.reshape(1, -1).astype(F32)

    tn = PROJ_TILE
    c_dt = d_ssm + d_xbc
    c_conf = c_dt + 2 * n_heads
    starts = ([s for s in range(0, d_ssm, tn)] + [c_conf + s for s in range(0, 2 * d_conf, tn)]
              + [d_ssm + s for s in range(0, d_xbc, tn)])
    w_tiles = jnp.stack([w_in[:, s:s + tn] for s in starts]).astype(BF16)
    wf = w_in[:, c_dt:c_dt + n_heads]
    wb = w_in[:, c_dt + n_heads:c_dt + 2 * n_heads]
    zpad = jnp.zeros((d, LANES - 3 * n_heads), F32)
    w_dt = jnp.concatenate([wf, wf, wf, zpad, wb, wb, wb, zpad], axis=1)
    rep3 = lambda v: jnp.concatenate([v, v, v, jnp.zeros((LANES - 3 * n_heads,), F32)])
    x2d = x.reshape(t, d)

    def split2(w):
        hi = w.astype(BF16)
        return hi, (w - hi.astype(F32)).astype(BF16)

    h, dt2, dtt2 = _norm_dt(x2d, row(norm_mix_w), *split2(w_dt), _pick(t, (512, 256, 128)))
    proj4 = _in_proj(h, w_tiles, _pick(t, (2048, 1024, 512, 256, 128))).reshape(-1, bsz, seq, tn)
    tile_conf = d_ssm // tn
    tile_xbc = (d_ssm + 2 * d_conf) // tn

    xbc = _conv_ssm(proj4, conv_ssm_w, row(conv_ssm_b), tile_xbc, _pick(seq, (1024, 512, 256, 128)), 512)
    y_conf = _conformer(proj4, conf_dw_w, row(conf_dw_b), row(conf_ln_w), row(conf_ln_b),
                        tile_conf, tile_conf + d_conf // tn, _pick(seq, (512, 256, 128)))

    e3 = ((jnp.arange(LANES)[:, None] % n_heads == jnp.arange(d_ssm)[None, :] // HEAD_DIM)
          & (jnp.arange(LANES)[:, None] < 3 * n_heads)).astype(BF16)
    prm = {}
    for name, bias, alog in (("f", dt_bias_fwd, a_log_fwd), ("b", dt_bias_bwd, a_log_bwd)):
        b3 = rep3(bias.astype(F32))
        a3 = rep3(alog.astype(F32))
        prm[name] = (b3.reshape(1, LANES), jnp.broadcast_to(b3[:, None], (LANES, CHUNK)),
                     a3.reshape(1, LANES), jnp.broadcast_to(a3[:, None], (LANES, CHUNK)))
    y_f = _ssd(xbc, dt2, dtt2, *prm["f"], e3, reverse=False, n_heads=n_heads)
    dskip_x = jnp.repeat(d_skip.astype(F32), HEAD_DIM).reshape(1, d_ssm)
    y_ssm = _ssd(xbc, dt2, dtt2, *prm["b"], e3, reverse=True, n_heads=n_heads,
                 final_args=(y_f, proj4, dskip_x, row(ssm_norm_w)))

    wr = jnp.zeros((d, LANES), F32).at[:, :n_experts].set(w_router.astype(F32))
    br = jnp.zeros((1, LANES), F32).at[0, :n_experts].set(b_router.astype(F32))
    x1, logits = _out_proj(y_ssm.reshape(t, d_ssm), y_conf.reshape(t, d_conf), w_out.astype(BF16), x2d,
                           row(norm_ffn_w), *split2(wr), br, _pick(t, (512, 256, 128)))

    bm = MOE_ROWS
    n_blocks = -(-(t * TOP_K + n_experts * (bm - 1)) // bm)
    ids_ranks, gates_x, counts = _router(logits, n_experts, _pick(t, (1024, 512, 256, 128)))
    gates = gates_x[:, :TOP_K]
    dest, zero_start, tables = _route_tables(ids_ranks, counts, n_experts, bm, n_blocks)
    xs = _dispatch(dest, zero_start, x1, row(norm_ffn_w), n_blocks * bm, _pick(t, (512, 256, 128)), bm)
    act = _moe_up(tables, xs, w_gate, b_gate[:, None, :], w_up, b_up[:, None, :], bm, 1024)
    y_rows = _moe_down(tables, act, w_down, b_down[:, None, :], bm)
    return x1, gates, dest, y_rows


def kernel(x, norm_mix_w, w_in, conv_ssm_w, conv_ssm_b, dt_bias_fwd, dt_bias_bwd, a_log_fwd, a_log_bwd, d_skip,
           ssm_norm_w, conf_dw_w, conf_dw_b, conf_ln_w, conf_ln_b, w_out, norm_ffn_w, w_router, b_router, w_gate,
           b_gate, w_up, b_up, w_down, b_down, norm_final_w):
    assert w_in.shape[0] == 1, "a single layer is supported"
    bsz, seq, d = x.shape
    t = bsz * seq
    x1, gates, dest, y_rows = _layer(
        x, norm_mix_w[0], w_in[0], conv_ssm_w[0], conv_ssm_b[0], dt_bias_fwd[0], dt_bias_bwd[0], a_log_fwd[0],
        a_log_bwd[0], d_skip[0], ssm_norm_w[0], conf_dw_w[0], conf_dw_b[0], conf_ln_w[0], conf_ln_b[0], w_out[0],
        norm_ffn_w[0], w_router[0], b_router[0], w_gate[0], b_gate[0], w_up[0], b_up[0], w_down[0], b_down[0])
    out = _combine(dest, y_rows, x1, gates, norm_final_w.reshape(1, d).astype(F32), _pick(t, (256, 128)))
    return out.reshape(bsz, seq, d)
```
